```python
import math
import jax, jax.numpy as jnp
from jax import lax
import numpy as np


D_MODEL = 1024
BATCH = 16
SEQ = 2048
DEPTH = 2

N_BRANCH = 3
BRANCH_WIDTH = 512
POOL_GROUPS = 4
POOL_WINDOWS = (2, 4, 8, 16)
POOL_WIDTH = 512
POOL_GC = POOL_WIDTH // POOL_GROUPS
DN_HEADS = 4
DN_HEAD_DIM = 128
DN_WIDTH = DN_HEADS * DN_HEAD_DIM
DN_CONV = 4
DN_CHUNK = 64
MOBA_HEADS = 8
MOBA_HEAD_DIM = 64
MOBA_WIDTH = MOBA_HEADS * MOBA_HEAD_DIM
MOBA_BLOCK = 256
MOBA_TOPK = 3
MOBA_Q_CHUNK = 16
REL_BUCKETS = 32
REL_MAX_DIST = 128
FFN_DIM = 2816
FFN_CONV = 3
PLE_DIM = 256
NORM_EPS = 1e-6
NEG_INF = -1e30
SPLIT_SIZES = (POOL_WIDTH, 3 * DN_WIDTH, DN_WIDTH, DN_HEADS, DN_HEADS, 3 * MOBA_WIDTH, N_BRANCH * D_MODEL)
IN_COLS = 7176

kernel_name = 'hybrid_pool_deltanet_moba_block'


def rmsnorm(x, w):
    xf = x.astype(jnp.float32)
    y = xf * lax.rsqrt(jnp.mean(xf * xf, axis=-1, keepdims=True) + NORM_EPS)
    return (y * w.astype(jnp.float32)).astype(x.dtype)


def l2norm(x):
    return x * lax.rsqrt(jnp.sum(x * x, axis=-1, keepdims=True) + NORM_EPS)


def split_cols(z):
    out, start = [], 0
    for n in SPLIT_SIZES:
        out.append(z[..., start:start + n])
        start += n
    return out


def causal_dwconv(x, w):
    k = w.shape[0]
    return lax.conv_general_dilated(x, w[:, None, :].astype(x.dtype), window_strides=(1,),
                                    padding=[(k - 1, 0)], dimension_numbers=('NWC', 'WIO', 'NWC'),
                                    feature_group_count=x.shape[-1])


def pool_mixer(u, w_group, scale):
    b, s, _ = u.shape
    ug = u.reshape(b, s, POOL_GROUPS, POOL_GC).astype(jnp.float32)
    c = jnp.pad(jnp.cumsum(ug, axis=1), ((0, 0), (1, 0), (0, 0), (0, 0)))
    t = jnp.arange(s)[:, None]
    win = jnp.array(POOL_WINDOWS, jnp.int32)[None, :]
    lo = jnp.maximum(t + 1 - win, 0)
    gi = jnp.arange(POOL_GROUPS)[None, :]
    total = c[:, 1:] - c[:, lo, gi]
    cnt = (t + 1 - lo).astype(jnp.float32)
    mixed = (total / cnt[None, :, :, None] - ug).astype(u.dtype)
    y = jnp.einsum('bsgc,gcd->bsgd', mixed, w_group).reshape(b, s, POOL_WIDTH)
    return y * scale


def chunk_gated_delta_rule(q, k, v, g, beta):
    b, h, s, dk = q.shape
    dv = v.shape[-1]
    n = s // DN_CHUNK
    q = q * dk ** -0.5
    chunks = lambda t: t.reshape(b, h, n, DN_CHUNK, *t.shape[3:])
    qc, kc, vc = chunks(q), chunks(k), chunks(v)
    gc = jnp.cumsum(chunks(g), axis=-1)
    bc = chunks(beta)
    kb = kc * bc[..., None]
    vb = vc * bc[..., None]
    incl = jnp.tril(jnp.ones((DN_CHUNK, DN_CHUNK), bool))
    strict = jnp.tril(jnp.ones((DN_CHUNK, DN_CHUNK), bool), -1)
    diff = gc[..., :, None] - gc[..., None, :]
    decay = jnp.where(incl, jnp.exp(jnp.where(incl, diff, 0.0)), 0.0)
    a_kk = jnp.where(strict, jnp.einsum('bhnid,bhnjd->bhnij', kb, kc) * decay, 0.0)
    eye = jnp.eye(DN_CHUNK, dtype=jnp.float32)
    t_mat = lax.linalg.triangular_solve(eye + a_kk, jnp.broadcast_to(eye, a_kk.shape),
                                        left_side=True, lower=True)
    u = t_mat @ vb
    w = t_mat @ (kb * jnp.exp(gc)[..., None])
    a_qk = jnp.where(incl, jnp.einsum('bhnid,bhnjd->bhnij', qc, kc) * decay, 0.0)
    q_dec = qc * jnp.exp(gc)[..., None]
    g_last = gc[..., -1]
    k_dec = kc * jnp.exp(g_last[..., None] - gc)[..., None]
    xs = tuple(jnp.moveaxis(t, 2, 0) for t in (u, w, a_qk, q_dec, k_dec, g_last))

    def step(state, inp):
        u_n, w_n, aqk_n, qd_n, kd_n, gl_n = inp
        v_new = u_n - w_n @ state
        o_n = qd_n @ state + aqk_n @ v_new
        state = state * jnp.exp(gl_n)[..., None, None] + jnp.einsum('bhcd,bhce->bhde', kd_n, v_new)
        return state, o_n

    state0 = jnp.zeros((b, h, dk, dv), jnp.float32)
    _, o = lax.scan(step, state0, xs)
    return jnp.moveaxis(o, 0, 2).reshape(b, h, s, dv)


def gated_deltanet(qkv, z, b_raw, a_raw, conv_w, a_log, dt_bias, norm_w):
    dt = qkv.dtype
    bsz, s, _ = qkv.shape
    act = jax.nn.silu(causal_dwconv(qkv, conv_w)).astype(jnp.float32)
    q, k, v = jnp.split(act, 3, axis=-1)
    heads = lambda t: t.reshape(bsz, s, DN_HEADS, DN_HEAD_DIM).transpose(0, 2, 1, 3)
    q, k, v = l2norm(heads(q)), l2norm(heads(k)), heads(v)
    beta = jax.nn.sigmoid(b_raw.astype(jnp.float32)).transpose(0, 2, 1)
    g = (-jnp.exp(a_log.astype(jnp.float32))
         * jax.nn.softplus(a_raw.astype(jnp.float32) + dt_bias.astype(jnp.float32))).transpose(0, 2, 1)
    o = chunk_gated_delta_rule(q, k, v, g, beta).transpose(0, 2, 1, 3)
    o = rmsnorm(o, norm_w) * jax.nn.silu(z.astype(jnp.float32).reshape(bsz, s, DN_HEADS, DN_HEAD_DIM))
    return o.reshape(bsz, s, DN_WIDTH).astype(dt)


def t5_bucket(rel):
    n = jnp.maximum(-rel, 0)
    exact = REL_BUCKETS // 2
    nf = jnp.maximum(n, 1).astype(jnp.float32)
    large = exact + (jnp.log(nf / exact) / math.log(REL_MAX_DIST / exact)
                     * (REL_BUCKETS - exact)).astype(jnp.int32)
    large = jnp.minimum(large, REL_BUCKETS - 1)
    return jnp.where(n < exact, n, large)


def moba_attention(q, k, v, rel_bias):
    b, h, s, dh = q.shape
    nb = -(-s // MOBA_BLOCK)
    sp = nb * MOBA_BLOCK
    padw = ((0, 0), (0, 0), (0, sp - s), (0, 0))
    q, k, v = jnp.pad(q, padw), jnp.pad(k, padw), jnp.pad(v, padw)
    kb = k.reshape(b, h, nb, MOBA_BLOCK, dh)
    vb = v.reshape(b, h, nb, MOBA_BLOCK, dh)
    k_mean = jnp.mean(kb.astype(jnp.float32), axis=3)
    q_blk = jnp.arange(sp) // MOBA_BLOCK
    gate = jnp.einsum('bhsd,bhnd->bhsn', q.astype(jnp.float32), k_mean)
    past = jnp.arange(nb)[None, :] < q_blk[:, None]
    gate = jnp.where(past, gate, NEG_INF)
    topk = min(MOBA_TOPK, nb)
    _, sel = lax.top_k(gate, topk)
    valid = sel < q_blk[:, None]
    nq = sp // MOBA_Q_CHUNK
    to_chunks = lambda t: jnp.moveaxis(t.reshape(b, h, nq, MOBA_Q_CHUNK, *t.shape[3:]), 2, 0)
    starts = jnp.arange(nq, dtype=jnp.int32) * MOBA_Q_CHUNK
    bi = jnp.arange(b)[:, None, None, None]
    hi = jnp.arange(h)[None, :, None, None]
    bias_ht = rel_bias.T.astype(jnp.float32)
    scale = dh ** -0.5
    offs = jnp.arange(MOBA_BLOCK, dtype=jnp.int32)

    def chunk(args):
        qc, selc, validc, start = args
        qpos = start + jnp.arange(MOBA_Q_CHUNK, dtype=jnp.int32)
        own = start // MOBA_BLOCK
        k_own = lax.dynamic_index_in_dim(kb, own, axis=2, keepdims=False)
        v_own = lax.dynamic_index_in_dim(vb, own, axis=2, keepdims=False)
        k_sel = kb[bi, hi, selc]
        v_sel = vb[bi, hi, selc]
        s_sel = jnp.einsum('bhqd,bhqnkd->bhqnk', qc, k_sel).astype(jnp.float32) * scale
        s_own = jnp.einsum('bhqd,bhkd->bhqk', qc, k_own).astype(jnp.float32) * scale
        kpos_sel = selc[..., None] * MOBA_BLOCK + offs
        kpos_own = own * MOBA_BLOCK + offs
        bias_sel = bias_ht[hi[..., None], t5_bucket(kpos_sel - qpos[:, None, None])]
        bias_own = bias_ht[:, t5_bucket(kpos_own[None, :] - qpos[:, None])]
        s_sel = jnp.where(validc[..., None], s_sel + bias_sel, NEG_INF)
        s_own = jnp.where(kpos_own[None, :] <= qpos[:, None], s_own + bias_own, NEG_INF)
        logits = jnp.concatenate([s_sel.reshape(b, h, MOBA_Q_CHUNK, topk * MOBA_BLOCK), s_own], axis=-1)
        probs = jax.nn.softmax(logits, axis=-1).astype(v.dtype)
        p_sel = probs[..., :topk * MOBA_BLOCK].reshape(b, h, MOBA_Q_CHUNK, topk, MOBA_BLOCK)
        p_own = probs[..., topk * MOBA_BLOCK:]
        return (jnp.einsum('bhqnk,bhqnkd->bhqd', p_sel, v_sel)
                + jnp.einsum('bhqk,bhkd->bhqd', p_own, v_own))

    out = lax.map(chunk, (to_chunks(q), to_chunks(sel), to_chunks(valid), starts))
    out = jnp.moveaxis(out, 0, 2).reshape(b, h, sp, dh)
    return out[:, :, :s]


def channel_mixer(h, w_up, w_conv, w_down):
    up = causal_dwconv(h @ w_up, w_conv)
    a, v = jnp.split(up, 2, axis=-1)
    return (jax.nn.gelu(a, approximate=True) * v) @ w_down


def setup_inputs(seed: int = 0) -> dict:
    key = jax.random.key(seed)
    ks = jax.random.split(key, 21)
    f32 = jnp.float32

    def nrm(k, shape, sc):
        return jax.random.normal(k, shape, f32) * sc

    def gain(k, shape):
        return 1.0 + 0.02 * jax.random.normal(k, shape, f32)

    dt = jnp.exp(jax.random.uniform(ks[9], (DEPTH, DN_HEADS), f32, math.log(1e-3), math.log(1e-1)))
    return {
        'x': nrm(ks[0], (BATCH, SEQ, D_MODEL), 1.0),
        'p': nrm(ks[1], (DEPTH, BATCH, SEQ, PLE_DIM), 1.0),
        'rel_bias': nrm(ks[2], (REL_BUCKETS, MOBA_HEADS), 0.5),
        'norm_mix': gain(ks[3], (DEPTH, D_MODEL)),
        'w_in': nrm(ks[4], (DEPTH, D_MODEL, IN_COLS), D_MODEL ** -0.5),
        'pool_w': nrm(ks[5], (DEPTH, POOL_GROUPS, POOL_GC, POOL_GC), POOL_GC ** -0.5),
        'pool_scale': gain(ks[6], (DEPTH, POOL_WIDTH)),
        'dn_conv': nrm(ks[7], (DEPTH, DN_CONV, 3 * DN_WIDTH), DN_CONV ** -0.5),
        'dn_a_log': jnp.log(jax.random.uniform(ks[8], (DEPTH, DN_HEADS), f32, 1.0, 16.0)),
        'dn_dt_bias': dt + jnp.log(-jnp.expm1(-dt)),
        'dn_norm': gain(ks[10], (DEPTH, DN_HEAD_DIM)),
        'w_branch': nrm(ks[11], (DEPTH, N_BRANCH, BRANCH_WIDTH, D_MODEL), BRANCH_WIDTH ** -0.5),
        'w_out': nrm(ks[12], (DEPTH, D_MODEL, D_MODEL), D_MODEL ** -0.5),
        'norm_ffn': gain(ks[13], (DEPTH, D_MODEL)),
        'ffn_up': nrm(ks[14], (DEPTH, D_MODEL, 2 * FFN_DIM), D_MODEL ** -0.5),
        'ffn_conv': nrm(ks[15], (DEPTH, FFN_CONV, 2 * FFN_DIM), FFN_CONV ** -0.5),
        'ffn_down': nrm(ks[16], (DEPTH, FFN_DIM, D_MODEL), FFN_DIM ** -0.5),
        'norm_ple': gain(ks[17], (DEPTH, D_MODEL)),
        'ple_gate': nrm(ks[18], (DEPTH, D_MODEL, D_MODEL), D_MODEL ** -0.5),
        'ple_proj': nrm(ks[19], (DEPTH, PLE_DIM, D_MODEL), PLE_DIM ** -0.5),
        'norm_final': gain(ks[20], (D_MODEL,)),
    }


def reference(x, p, rel_bias, norm_mix, w_in, pool_w, pool_scale, dn_conv, dn_a_log, dn_dt_bias,
              dn_norm, w_branch, w_out, norm_ffn, ffn_up, ffn_conv, ffn_down, norm_ple, ple_gate,
              ple_proj, norm_final):
    bsz, s, d = x.shape
    for i in range(DEPTH):
        h = rmsnorm(x, norm_mix[i])
        z = h @ w_in[i]
        u_pool, dn_qkv, dn_z, dn_b, dn_a, mb_qkv, gate_raw = split_cols(z)
        y_a = pool_mixer(u_pool, pool_w[i], pool_scale[i])
        y_b = gated_deltanet(dn_qkv, dn_z, dn_b, dn_a, dn_conv[i], dn_a_log[i], dn_dt_bias[i], dn_norm[i])
        mq, mk, mv = (t.reshape(bsz, s, MOBA_HEADS, MOBA_HEAD_DIM).transpose(0, 2, 1, 3)
                      for t in jnp.split(mb_qkv, 3, axis=-1))
        y_c = moba_attention(mq, mk, mv, rel_bias).transpose(0, 2, 1, 3).reshape(bsz, s, MOBA_WIDTH)
        branches = jnp.stack([y_a, y_b, y_c], axis=2)
        proj = jnp.einsum('bsnc,ncd->bsnd', branches, w_branch[i])
        gates = jax.nn.sigmoid(gate_raw.reshape(bsz, s, N_BRANCH, d))
        merged = jnp.sum(gates * proj, axis=2)
        x = x + merged @ w_out[i]
        x = x + channel_mixer(rmsnorm(x, norm_ffn[i]), ffn_up[i], ffn_conv[i], ffn_down[i])
        ple_g = jax.nn.sigmoid(rmsnorm(x, norm_ple[i]) @ ple_gate[i])
        x = x + (p[i] @ ple_proj[i]) * ple_g
    return rmsnorm(x, norm_final)
```

```python
import functools
import math

import jax
import jax.numpy as jnp
from jax import lax
from jax.experimental import pallas as pl
from jax.experimental.pallas import tpu as pltpu

F32 = jnp.float32
BF16 = jnp.bfloat16
HIGHEST = lax.Precision.HIGHEST

D_MODEL = 1024
BRANCH_WIDTH = 512
POOL_WINDOWS = (2, 4, 8, 16)
POOL_GC = 128
POOL_HALO = 16
DN_HEADS = 4
DN_HEAD_DIM = 128
DN_CONV = 4
DN_CHUNK = 64
MOBA_HEADS = 8
MOBA_HEAD_DIM = 64
MOBA_BLOCK = 256
MOBA_TOPK = 3
REL_BUCKETS = 32
REL_MAX_DIST = 128
FFN_DIM = 2816
FFN_CONV = 3
PLE_DIM = 256
NORM_EPS = 1e-6
NEG_INF = -1e30
LANES = 128
SUBLANES = 8

COL_POOL = 0
COL_DNQ = 512
COL_DNK = 1024
COL_DNV = 1536
COL_DNZ = 2048
COL_MQ = 2560
COL_MK = 3072
COL_MV = 3584
COL_GATE = 4096
Z_COLS = 7168

VMEM_LIMIT = 56 * 1024 * 1024


def _cparams(sem):
    return pltpu.CompilerParams(dimension_semantics=sem, vmem_limit_bytes=VMEM_LIMIT)


def _dot(a, b, precision=None):
    return lax.dot_general(a, b, (((1,), (0,)), ((), ())), precision=precision,
                           preferred_element_type=F32)


def _dot_nt(a, b, precision=None):
    return lax.dot_general(a, b, (((1,), (1,)), ((), ())), precision=precision,
                           preferred_element_type=F32)


def _dot_tn(a, b, precision=None):
    return lax.dot_general(a, b, (((0,), (0,)), ((), ())), precision=precision,
                           preferred_element_type=F32)


def _rms(x, w):
    return x * lax.rsqrt(jnp.mean(x * x, axis=-1, keepdims=True) + NORM_EPS) * w


def _in_proj_kernel(x_ref, nw_ref, w_ref, ws_ref, z_ref, zs_ref, h_ref):
    @pl.when(pl.program_id(1) == 0)
    def _():
        hb = _rms(x_ref[...], nw_ref[...]).astype(BF16)
        h_ref[...] = hb
        zs_ref[...] = _dot(hb, ws_ref[...])

    z_ref[...] = _dot(h_ref[...], w_ref[...])


def _in_proj(x, nw, w_main, w_small, tm=1024, tn=1024):
    t = x.shape[0]
    return pl.pallas_call(
        _in_proj_kernel,
        grid=(t // tm, Z_COLS // tn),
        in_specs=[
            pl.BlockSpec((tm, D_MODEL), lambda i, j: (i, 0)),
            pl.BlockSpec((1, D_MODEL), lambda i, j: (0, 0)),
            pl.BlockSpec((D_MODEL, tn), lambda i, j: (0, j)),
            pl.BlockSpec((D_MODEL, LANES), lambda i, j: (0, 0)),
        ],
        out_specs=[
            pl.BlockSpec((tm, tn), lambda i, j: (i, j)),
            pl.BlockSpec((tm, LANES), lambda i, j: (i, 0)),
        ],
        out_shape=[jax.ShapeDtypeStruct((t, Z_COLS), F32),
                   jax.ShapeDtypeStruct((t, LANES), F32)],
        scratch_shapes=[pltpu.VMEM((tm, D_MODEL), BF16)],
        compiler_params=_cparams(("parallel", "arbitrary")),
        name="in_proj",
    )(x, nw, w_main, w_small)


def _pool_kernel(u_ref, up_ref, w_ref, sc_ref, y_ref, pad_ref, *, tp):
    j = pl.program_id(1)
    pad_ref[0:POOL_HALO, :] = jnp.where(j == 0, 0.0, up_ref[...])
    pad_ref[POOL_HALO:POOL_HALO + tp, :] = u_ref[...]
    t = j * tp + lax.broadcasted_iota(jnp.int32, (tp, POOL_GC), 0)
    for g, win in enumerate(POOL_WINDOWS):
        cols = slice(g * POOL_GC, (g + 1) * POOL_GC)
        tot = pad_ref[POOL_HALO:POOL_HALO + tp, cols]
        for d in range(1, win):
            tot = tot + pad_ref[POOL_HALO - d:POOL_HALO - d + tp, cols]
        cnt = jnp.minimum(t + 1, win).astype(F32)
        mixed = tot / cnt - u_ref[:, cols]
        y = _dot(mixed.astype(BF16), w_ref[g])
        y_ref[:, cols] = y * sc_ref[:, cols]


def _pool(z, pool_w, pool_scale, bsz, seq, tp=512):
    nt = seq // tp
    hb = tp // POOL_HALO
    return pl.pallas_call(
        functools.partial(_pool_kernel, tp=tp),
        grid=(bsz, nt),
        in_specs=[
            pl.BlockSpec((tp, BRANCH_WIDTH), lambda b, j: (b * nt + j, 0)),
            pl.BlockSpec((POOL_HALO, BRANCH_WIDTH),
                         lambda b, j: (jnp.maximum((b * nt + j) * hb - 1, 0), 0)),
            pl.BlockSpec((4, POOL_GC, POOL_GC), lambda b, j: (0, 0, 0)),
            pl.BlockSpec((1, BRANCH_WIDTH), lambda b, j: (0, 0)),
        ],
        out_specs=pl.BlockSpec((tp, BRANCH_WIDTH), lambda b, j: (b * nt + j, 0)),
        out_shape=jax.ShapeDtypeStruct((bsz * seq, BRANCH_WIDTH), F32),
        scratch_shapes=[pltpu.VMEM((POOL_HALO + tp, BRANCH_WIDTH), F32)],
        compiler_params=_cparams(("parallel", "arbitrary")),
        name="pool",
    )(z, z, pool_w, pool_scale)


def _dn_kernel(q_ref, k_ref, v_ref, zg_ref, zs_ref, cwq_ref, cwk_ref, cwv_ref, hp_ref, nw_ref,
               y_ref, pad_ref, halo_ref, state_ref, qs_ref, ks_ref, vs_ref, gs_ref, bs_ref,
               os_ref, *, tc):
    c = pl.program_id(1)

    @pl.when(c == 0)
    def _():
        halo_ref[...] = jnp.zeros_like(halo_ref)
        state_ref[...] = jnp.zeros_like(state_ref)

    for part, (src, cw, dst) in enumerate(((q_ref, cwq_ref, qs_ref), (k_ref, cwk_ref, ks_ref),
                                          (v_ref, cwv_ref, vs_ref))):
        pad_ref[0:SUBLANES, :] = halo_ref[part]
        pad_ref[SUBLANES:SUBLANES + tc, :] = src[...]
        halo_ref[part] = src[tc - SUBLANES:tc, :]
        conv = cw[DN_CONV - 1:DN_CONV, :] * src[...]
        for tap in range(DN_CONV - 1):
            off = SUBLANES - (DN_CONV - 1) + tap
            conv = conv + cw[tap:tap + 1, :] * pad_ref[off:off + tc, :]
        dst[...] = conv * jax.nn.sigmoid(conv)

    zs = zs_ref[...]
    for h in range(DN_HEADS):
        cols = slice(h * DN_HEAD_DIM, (h + 1) * DN_HEAD_DIM)
        q = qs_ref[:, cols]
        k = ks_ref[:, cols]
        q = q * lax.rsqrt(jnp.sum(q * q, axis=-1, keepdims=True) + NORM_EPS)
        k = k * lax.rsqrt(jnp.sum(k * k, axis=-1, keepdims=True) + NORM_EPS)
        qs_ref[:, cols] = q * (DN_HEAD_DIM ** -0.5)
        ks_ref[:, cols] = k
        b_raw = jnp.broadcast_to(zs[:, h:h + 1], (tc, LANES))
        a_raw = jnp.broadcast_to(zs[:, DN_HEADS + h:DN_HEADS + h + 1], (tc, LANES))
        bs_ref[h] = jax.nn.sigmoid(b_raw)
        a_log = hp_ref[h, 0:1, :]
        dt_bias = hp_ref[h, 1:2, :]
        gs_ref[h] = -jnp.exp(a_log) * jnp.logaddexp(a_raw + dt_bias, 0.0)

    ri = lax.broadcasted_iota(jnp.int32, (DN_CHUNK, DN_CHUNK), 0)
    ci = lax.broadcasted_iota(jnp.int32, (DN_CHUNK, DN_CHUNK), 1)
    incl = ri >= ci
    strict = ri > ci
    lower_f = incl.astype(F32)
    upper_f = (ri <= ci).astype(F32)
    ones_f = jnp.ones((DN_CHUNK, DN_CHUNK), F32)
    eye_f = (ri == ci).astype(F32)

    def chunk_body(n, carry):
        rows = pl.ds(pl.multiple_of(n * DN_CHUNK, DN_CHUNK), DN_CHUNK)
        for h in range(DN_HEADS):
            cols = slice(h * DN_HEAD_DIM, (h + 1) * DN_HEAD_DIM)
            q = qs_ref[rows, cols]
            k = ks_ref[rows, cols]
            v = vs_ref[rows, cols]
            g = gs_ref[h, rows, :]
            beta = bs_ref[h, rows, :]
            gc = _dot(lower_f, g, HIGHEST)
            gc_row = _dot(ones_f, g[:, :DN_CHUNK] * upper_f, HIGHEST)
            diff = gc[:, :DN_CHUNK] - gc_row
            decay = jnp.where(incl, jnp.exp(jnp.where(incl, diff, 0.0)), 0.0)
            kb = k * beta
            vb = v * beta
            a_kk = jnp.where(strict, _dot_nt(kb, k, HIGHEST) * decay, 0.0)
            m = -a_kk
            t_mat = eye_f + m
            for _ in range(5):
                m = _dot(m, m, HIGHEST)
                t_mat = t_mat + _dot(t_mat, m, HIGHEST)
            egc = jnp.exp(gc)
            u = _dot(t_mat, vb, HIGHEST)
            w = _dot(t_mat, kb * egc, HIGHEST)
            a_qk = jnp.where(incl, _dot_nt(q, k, HIGHEST) * decay, 0.0)
            q_dec = q * egc
            g_last = gc[DN_CHUNK - 1:DN_CHUNK, :]
            k_dec = k * jnp.exp(g_last - gc)
            state = state_ref[h]
            v_new = u - _dot(w, state, HIGHEST)
            os_ref[rows, cols] = _dot(q_dec, state, HIGHEST) + _dot(a_qk, v_new, HIGHEST)
            state_ref[h] = state * jnp.exp(g_last) + _dot_tn(k_dec, v_new, HIGHEST)
        return carry

    lax.fori_loop(0, tc // DN_CHUNK, chunk_body, 0)

    for h in range(DN_HEADS):
        cols = slice(h * DN_HEAD_DIM, (h + 1) * DN_HEAD_DIM)
        zg = zg_ref[:, cols]
        y_ref[:, cols] = _rms(os_ref[:, cols], nw_ref[...]) * (zg * jax.nn.sigmoid(zg))


def _deltanet(z, zs, conv_w, head_params, norm_w, bsz, seq, tc=512):
    nt = seq // tc
    wide = DN_HEADS * DN_HEAD_DIM

    def col(cb):
        return pl.BlockSpec((tc, wide), lambda b, c: (b * nt + c, cb))

    def cw(cb):
        return pl.BlockSpec((DN_CONV, wide), lambda b, c: (0, cb))

    return pl.pallas_call(
        functools.partial(_dn_kernel, tc=tc),
        grid=(bsz, nt),
        in_specs=[
            col(COL_DNQ // wide), col(COL_DNK // wide), col(COL_DNV // wide), col(COL_DNZ // wide),
            pl.BlockSpec((tc, LANES), lambda b, c: (b * nt + c, 0)),
            cw(0), cw(1), cw(2),
            pl.BlockSpec((DN_HEADS, SUBLANES, LANES), lambda b, c: (0, 0, 0)),
            pl.BlockSpec((1, DN_HEAD_DIM), lambda b, c: (0, 0)),
        ],
        out_specs=pl.BlockSpec((tc, wide), lambda b, c: (b * nt + c, 0)),
        out_shape=jax.ShapeDtypeStruct((bsz * seq, wide), F32),
        scratch_shapes=[
            pltpu.VMEM((SUBLANES + tc, wide), F32),
            pltpu.VMEM((3, SUBLANES, wide), F32),
            pltpu.VMEM((DN_HEADS, DN_HEAD_DIM, DN_HEAD_DIM), F32),
            pltpu.VMEM((tc, wide), F32), pltpu.VMEM((tc, wide), F32), pltpu.VMEM((tc, wide), F32),
            pltpu.VMEM((DN_HEADS, tc, LANES), F32), pltpu.VMEM((DN_HEADS, tc, LANES), F32),
            pltpu.VMEM((tc, wide), F32),
        ],
        compiler_params=_cparams(("parallel", "arbitrary")),
        name="deltanet",
    )(z, z, z, z, zs, conv_w, conv_w, conv_w, head_params, norm_w)


def _moba_kernel(q_ref, k_ref, v_ref, bown_ref, bprev_ref, bfar_ref, o_ref, kmean_ref, *, nb):
    i = pl.program_id(2)

    @pl.when(i == 0)
    def _():
        for j in range(nb):
            blk = k_ref[j * MOBA_BLOCK:(j + 1) * MOBA_BLOCK, :]
            kmean_ref[j:j + 1, :] = jnp.mean(blk, axis=0, keepdims=True)

    lane = lax.broadcasted_iota(jnp.int32, (MOBA_BLOCK, LANES), 1)
    blk_id = lax.broadcasted_iota(jnp.int32, (MOBA_BLOCK, nb), 1)
    qpos = lax.broadcasted_iota(jnp.int32, (MOBA_BLOCK, MOBA_BLOCK), 0)
    kpos = lax.broadcasted_iota(jnp.int32, (MOBA_BLOCK, MOBA_BLOCK), 1)
    scale = MOBA_HEAD_DIM ** -0.5
    qf = q_ref[...]
    outs = []
    for hh in range(2):
        half = (lane >= hh * MOBA_HEAD_DIM) & (lane < (hh + 1) * MOBA_HEAD_DIM)
        qh = jnp.where(half, qf, 0.0)
        qb = qh.astype(BF16)
        gate = _dot_nt(qh, kmean_ref[...], HIGHEST)
        gate = jnp.where(blk_id < i, gate, NEG_INF)
        sel = jnp.zeros((MOBA_BLOCK, nb), F32)
        for j in range(nb):
            gj = gate[:, j:j + 1]
            ahead = (gate > gj) | ((gate == gj) & (blk_id < j))
            rank = jnp.sum(ahead.astype(F32), axis=-1, keepdims=True)
            chosen = (rank < MOBA_TOPK) & (j < i)
            sel = jnp.where((blk_id == j) & chosen, 1.0, sel)

        start = pl.multiple_of(i * MOBA_BLOCK, MOBA_BLOCK)
        kj = k_ref[pl.ds(start, MOBA_BLOCK), :].astype(BF16)
        vj = v_ref[pl.ds(start, MOBA_BLOCK), :].astype(BF16)
        s = _dot_nt(qb, kj) * scale + bown_ref[hh]
        s = jnp.where(kpos <= qpos, s, NEG_INF)
        m0 = jnp.max(s, axis=-1, keepdims=True)
        p = jnp.exp(s - m0)
        l0 = jnp.sum(p, axis=-1, keepdims=True)
        acc0 = _dot(p.astype(BF16), vj)

        bfar = bfar_ref[hh, 0:1, :]

        def past_body(j, carry, qb=qb, sel=sel, hh=hh, bfar=bfar):
            m, l, acc = carry
            st = pl.multiple_of(j * MOBA_BLOCK, MOBA_BLOCK)
            kj = k_ref[pl.ds(st, MOBA_BLOCK), :].astype(BF16)
            vj = v_ref[pl.ds(st, MOBA_BLOCK), :].astype(BF16)
            far = jnp.concatenate([bfar, bfar], axis=-1)
            bias = jnp.where(j == i - 1, bprev_ref[hh], far)
            s = _dot_nt(qb, kj) * scale + bias
            picked = jnp.sum(jnp.where(blk_id == j, sel, 0.0), axis=-1, keepdims=True)
            s = jnp.where(picked > 0.5, s, NEG_INF)
            m_new = jnp.maximum(m, jnp.max(s, axis=-1, keepdims=True))
            alpha = jnp.exp(m - m_new)
            p = jnp.exp(s - m_new)
            l = alpha * l + jnp.sum(p, axis=-1, keepdims=True)
            acc = alpha * acc + _dot(p.astype(BF16), vj)
            return m_new, l, acc

        m, l, acc = lax.fori_loop(0, i, past_body, (m0, l0, acc0))
        outs.append((half, acc / l))
    o_ref[...] = jnp.where(outs[0][0], outs[0][1], outs[1][1])


def _moba(z, bias_own, bias_prev, bias_far, bsz, seq):
    nb = seq // MOBA_BLOCK
    pairs = MOBA_HEADS // 2
    qcb = COL_MQ // LANES
    kcb = COL_MK // LANES
    vcb = COL_MV // LANES
    return pl.pallas_call(
        functools.partial(_moba_kernel, nb=nb),
        grid=(bsz, pairs, nb),
        in_specs=[
            pl.BlockSpec((MOBA_BLOCK, LANES), lambda b, p, i: (b * nb + i, qcb + p)),
            pl.BlockSpec((seq, LANES), lambda b, p, i: (b, kcb + p)),
            pl.BlockSpec((seq, LANES), lambda b, p, i: (b, vcb + p)),
            pl.BlockSpec((2, MOBA_BLOCK, MOBA_BLOCK), lambda b, p, i: (p, 0, 0)),
            pl.BlockSpec((2, MOBA_BLOCK, MOBA_BLOCK), lambda b, p, i: (p, 0, 0)),
            pl.BlockSpec((2, SUBLANES, LANES), lambda b, p, i: (p, 0, 0)),
        ],
        out_specs=pl.BlockSpec((MOBA_BLOCK, LANES), lambda b, p, i: (b * nb + i, p)),
        out_shape=jax.ShapeDtypeStruct((bsz * seq, MOBA_HEADS * MOBA_HEAD_DIM), F32),
        scratch_shapes=[pltpu.VMEM((nb, LANES), F32)],
        compiler_params=_cparams(("parallel", "parallel", "arbitrary")),
        name="moba",
    )(z, z, z, bias_own, bias_prev, bias_far)


def _t5_bucket(rel):
    n = jnp.maximum(-rel, 0)
    exact = REL_BUCKETS // 2
    nf = jnp.maximum(n, 1).astype(F32)
    large = exact + (jnp.log(nf / exact) / math.log(REL_MAX_DIST / exact)
                     * (REL_BUCKETS - exact)).astype(jnp.int32)
    large = jnp.minimum(large, REL_BUCKETS - 1)
    return jnp.where(n < exact, n, large)


def _moba_bias_tables(rel_bias):
    bias_ht = rel_bias.T.astype(F32)
    qpos = jnp.arange(MOBA_BLOCK, dtype=jnp.int32)[:, None]
    kpos = jnp.arange(MOBA_BLOCK, dtype=jnp.int32)[None, :]
    own = bias_ht[:, _t5_bucket(kpos - qpos)]
    prev = bias_ht[:, _t5_bucket(kpos - MOBA_BLOCK - qpos)]
    far = bias_ht[:, _t5_bucket(jnp.int32(-2 * MOBA_BLOCK))]
    far = jnp.broadcast_to(far[:, None, None], (MOBA_HEADS, SUBLANES, LANES))
    return own, prev, far


def _merge_kernel(ya_ref, yb_ref, yc_ref, ga_ref, gb_ref, gc_ref, x_ref, wb_ref, wo_ref, o_ref):
    merged = None
    for n, (y_ref, g_ref) in enumerate(((ya_ref, ga_ref), (yb_ref, gb_ref), (yc_ref, gc_ref))):
        proj = _dot(y_ref[...].astype(BF16), wb_ref[n])
        gate = jax.nn.sigmoid(g_ref[...])
        term = gate * proj
        merged = term if merged is None else merged + term
    o_ref[...] = x_ref[...] + _dot(merged.astype(BF16), wo_ref[...])


def _merge(ya, yb, yc, z, x, w_branch, w_out, tm=512):
    t = x.shape[0]
    gcb = COL_GATE // D_MODEL
    row = lambda i: (i, 0)
    return pl.pallas_call(
        _merge_kernel,
        grid=(t // tm,),
        in_specs=[
            pl.BlockSpec((tm, BRANCH_WIDTH), row),
            pl.BlockSpec((tm, BRANCH_WIDTH), row),
            pl.BlockSpec((tm, BRANCH_WIDTH), row),
            pl.BlockSpec((tm, D_MODEL), lambda i: (i, gcb)),
            pl.BlockSpec((tm, D_MODEL), lambda i: (i, gcb + 1)),
            pl.BlockSpec((tm, D_MODEL), lambda i: (i, gcb + 2)),
            pl.BlockSpec((tm, D_MODEL), row),
            pl.BlockSpec((3, BRANCH_WIDTH, D_MODEL), lambda i: (0, 0, 0)),
            pl.BlockSpec((D_MODEL, D_MODEL), lambda i: (0, 0)),
        ],
        out_specs=pl.BlockSpec((tm, D_MODEL), row),
        out_shape=jax.ShapeDtypeStruct((t, D_MODEL), F32),
        compiler_params=_cparams(("parallel",)),
        name="merge",
    )(ya, yb, yc, z, z, z, x, w_branch, w_out)


def _ffn_kernel(x_ref, nw_ref, wa_ref, wv_ref, cwa_ref, cwv_ref, wd_ref, o_ref,
                h_ref, acc_ref, carry_ref, pad_ref, *, tm, nf):
    j = pl.program_id(1)
    f = pl.program_id(2)

    @pl.when(f == 0)
    def _():
        h_ref[...] = _rms(x_ref[...], nw_ref[...]).astype(BF16)
        acc_ref[...] = jnp.zeros_like(acc_ref)

    h = h_ref[...]
    conv = []
    for part, (w_ref, cw_ref) in enumerate(((wa_ref, cwa_ref), (wv_ref, cwv_ref))):
        up = _dot(h, w_ref[...])
        pad_ref[part, 0:SUBLANES, :] = jnp.where(j == 0, 0.0, carry_ref[f, part])
        pad_ref[part, SUBLANES:SUBLANES + tm, :] = up
        carry_ref[f, part] = up[tm - SUBLANES:tm, :]
        y = cw_ref[FFN_CONV - 1:FFN_CONV, :] * up
        for tap in range(FFN_CONV - 1):
            off = SUBLANES - (FFN_CONV - 1) + tap
            y = y + cw_ref[tap:tap + 1, :] * pad_ref[part, off:off + tm, :]
        conv.append(y)
    act = jax.nn.gelu(conv[0], approximate=True) * conv[1]
    acc_ref[...] += _dot(act.astype(BF16), wd_ref[...])

    @pl.when(f == nf - 1)
    def _():
        o_ref[...] = x_ref[...] + acc_ref[...]


def _ffn(x, nw, w_up, w_conv, w_down, bsz, seq, tm=1024, fc=256):
    nt = seq // tm
    nf = FFN_DIM // fc
    return pl.pallas_call(
        functools.partial(_ffn_kernel, tm=tm, nf=nf),
        grid=(bsz, nt, nf),
        in_specs=[
            pl.BlockSpec((tm, D_MODEL), lambda b, j, f: (b * nt + j, 0)),
            pl.BlockSpec((1, D_MODEL), lambda b, j, f: (0, 0)),
            pl.BlockSpec((D_MODEL, fc), lambda b, j, f: (0, f)),
            pl.BlockSpec((D_MODEL, fc), lambda b, j, f: (0, nf + f)),
            pl.BlockSpec((FFN_CONV, fc), lambda b, j, f: (0, f)),
            pl.BlockSpec((FFN_CONV, fc), lambda b, j, f: (0, nf + f)),
            pl.BlockSpec((fc, D_MODEL), lambda b, j, f: (f, 0)),
        ],
        out_specs=pl.BlockSpec((tm, D_MODEL), lambda b, j, f: (b * nt + j, 0)),
        out_shape=jax.ShapeDtypeStruct((bsz * seq, D_MODEL), F32),
        scratch_shapes=[
            pltpu.VMEM((tm, D_MODEL), BF16),
            pltpu.VMEM((tm, D_MODEL), F32),
            pltpu.VMEM((nf, 2, SUBLANES, fc), F32),
            pltpu.VMEM((2, SUBLANES + tm, fc), F32),
        ],
        compiler_params=_cparams(("parallel", "arbitrary", "arbitrary")),
        name="ffn",
    )(x, nw, w_up, w_up, w_conv, w_conv, w_down)


def _ple_kernel(x_ref, p_ref, nw_ref, wg_ref, wp_ref, fw_ref, o_ref, *, final):
    x = x_ref[...]
    gate = jax.nn.sigmoid(_dot(_rms(x, nw_ref[...]).astype(BF16), wg_ref[...]))
    x = x + _dot(p_ref[...].astype(BF16), wp_ref[...]) * gate
    if final:
        x = _rms(x, fw_ref[...])
    o_ref[...] = x


def _ple(x, p, nw, w_gate, w_proj, final_w, final, tm=1024):
    t = x.shape[0]
    row = lambda i: (i, 0)
    fixed = lambda i: (0, 0)
    return pl.pallas_call(
        functools.partial(_ple_kernel, final=final),
        grid=(t // tm,),
        in_specs=[
            pl.BlockSpec((tm, D_MODEL), row),
            pl.BlockSpec((tm, PLE_DIM), row),
            pl.BlockSpec((1, D_MODEL), fixed),
            pl.BlockSpec((D_MODEL, D_MODEL), fixed),
            pl.BlockSpec((PLE_DIM, D_MODEL), fixed),
            pl.BlockSpec((1, D_MODEL), fixed),
        ],
        out_specs=pl.BlockSpec((tm, D_MODEL), row),
        out_shape=jax.ShapeDtypeStruct((t, D_MODEL), F32),
        compiler_params=_cparams(("parallel",)),
        name="ple",
    )(x, p, nw, w_gate, w_proj, final_w)


def _split_w_in(w):
    pool = w[:, 0:512]
    dn_qkv = w[:, 512:2048]
    dn_z = w[:, 2048:2560]
    dn_ba = w[:, 2560:2568]
    mb_qkv = w[:, 2568:4104]
    gates = w[:, 4104:7176]
    main = jnp.concatenate([pool, dn_qkv, dn_z, mb_qkv, gates], axis=1).astype(BF16)
    small = jnp.pad(dn_ba, ((0, 0), (0, LANES - 2 * DN_HEADS))).astype(BF16)
    return main, small


def kernel(x, p, rel_bias, norm_mix, w_in, pool_w, pool_scale, dn_conv, dn_a_log, dn_dt_bias,
           dn_norm, w_branch, w_out, norm_ffn, ffn_up, ffn_conv, ffn_down, norm_ple, ple_gate,
           ple_proj, norm_final):
    bsz, seq, d = x.shape
    depth = w_in.shape[0]
    t = bsz * seq
    xt = x.reshape(t, d)
    bias_own, bias_prev, bias_far = _moba_bias_tables(rel_bias)
    for i in range(depth):
        w_main, w_small = _split_w_in(w_in[i])
        z, zs = _in_proj(xt, norm_mix[i][None, :], w_main, w_small)
        y_a = _pool(z, pool_w[i].astype(BF16), pool_scale[i][None, :], bsz, seq)
        head_params = jnp.zeros((DN_HEADS, SUBLANES, LANES), F32)
        head_params = head_params.at[:, 0, :].set(dn_a_log[i][:, None])
        head_params = head_params.at[:, 1, :].set(dn_dt_bias[i][:, None])
        y_b = _deltanet(z, zs, dn_conv[i], head_params, dn_norm[i][None, :], bsz, seq)
        y_c = _moba(z, bias_own, bias_prev, bias_far, bsz, seq)
        xt = _merge(y_a, y_b, y_c, z, xt, w_branch[i].astype(BF16), w_out[i].astype(BF16))
        xt = _ffn(xt, norm_ffn[i][None, :], ffn_up[i].astype(BF16), ffn_conv[i],
                  ffn_down[i].astype(BF16), bsz, seq)
        xt = _ple(xt, p[i].reshape(t, PLE_DIM), norm_ple[i][None, :], ple_gate[i].astype(BF16),
                  ple_proj[i].astype(BF16), norm_final[None, :], final=(i == depth - 1))
    return xt.reshape(bsz, seq, d)
```

```python
import functools
import math

import jax
import jax.numpy as jnp
from jax import lax
from jax.experimental import pallas as pl
from jax.experimental.pallas import tpu as pltpu

F32 = jnp.float32
BF16 = jnp.bfloat16
HIGHEST = lax.Precision.HIGHEST

D_MODEL = 1024
BRANCH_WIDTH = 512
POOL_WINDOWS = (2, 4, 8, 16)
POOL_GC = 128
POOL_HALO = 16
DN_HEADS = 4
DN_HEAD_DIM = 128
DN_CONV = 4
DN_CHUNK = 64
MOBA_HEADS = 8
MOBA_HEAD_DIM = 64
MOBA_BLOCK = 256
MOBA_TOPK = 3
REL_BUCKETS = 32
REL_MAX_DIST = 128
FFN_DIM = 2816
FFN_CONV = 3
PLE_DIM = 256
NORM_EPS = 1e-6
NEG_INF = -1e30
LANES = 128
SUBLANES = 8

COL_POOL = 0
COL_DNQ = 512
COL_DNK = 1024
COL_DNV = 1536
COL_DNZ = 2048
COL_MQ = 2560
COL_MK = 3072
COL_MV = 3584
COL_GATE = 4096
Z_COLS = 7168

VMEM_LIMIT = 56 * 1024 * 1024


def _cparams(sem):
    return pltpu.CompilerParams(dimension_semantics=sem, vmem_limit_bytes=VMEM_LIMIT)


def _dot(a, b, precision=None):
    return lax.dot_general(a, b, (((1,), (0,)), ((), ())), precision=precision,
                           preferred_element_type=F32)


def _dot_nt(a, b, precision=None):
    return lax.dot_general(a, b, (((1,), (1,)), ((), ())), precision=precision,
                           preferred_element_type=F32)


def _dot_tn(a, b, precision=None):
    return lax.dot_general(a, b, (((0,), (0,)), ((), ())), precision=precision,
                           preferred_element_type=F32)


def _split3(x):
    hi = x.astype(BF16)
    rest = x - hi.astype(F32)
    mid = rest.astype(BF16)
    lo = (rest - mid.astype(F32)).astype(BF16)
    return hi, mid, lo


def _mm_dims(a, b, dims, passes):
    dot = lambda x, y: lax.dot_general(x, y, dims, preferred_element_type=F32)
    a_hi = a.astype(BF16)
    b_hi = b.astype(BF16)
    if passes == 1:
        return dot(a_hi, b_hi)
    a_lo = (a - a_hi.astype(F32)).astype(BF16)
    b_lo = (b - b_hi.astype(F32)).astype(BF16)
    return dot(a_hi, b_hi) + (dot(a_hi, b_lo) + dot(a_lo, b_hi))


def _mm(a, b, passes):
    return _mm_dims(a, b, (((1,), (0,)), ((), ())), passes)


def _mm_nt(a, b, passes):
    return _mm_dims(a, b, (((1,), (1,)), ((), ())), passes)


def _mm_tn(a, b, passes):
    return _mm_dims(a, b, (((0,), (0,)), ((), ())), passes)


DN_PASSES_QK = 1
DN_PASSES_INV = 1
DN_PASSES_STATE = 1
DN_INV_BASE = 8


def _rms(x, w):
    return x * lax.rsqrt(jnp.mean(x * x, axis=-1, keepdims=True) + NORM_EPS) * w


def _in_proj_kernel(x_ref, nw_ref, w_ref, ws_ref, z_ref, zs_ref, h_ref):
    @pl.when(pl.program_id(1) == 0)
    def _():
        hb = _rms(x_ref[...], nw_ref[...]).astype(BF16)
        h_ref[...] = hb
        zs_ref[...] = _dot(hb, ws_ref[...])

    z_ref[...] = _dot(h_ref[...], w_ref[...])


def _in_proj(x, nw, w_main, w_small, tm=1024, tn=1024):
    t = x.shape[0]
    return pl.pallas_call(
        _in_proj_kernel,
        grid=(t // tm, Z_COLS // tn),
        in_specs=[
            pl.BlockSpec((tm, D_MODEL), lambda i, j: (i, 0)),
            pl.BlockSpec((1, D_MODEL), lambda i, j: (0, 0)),
            pl.BlockSpec((D_MODEL, tn), lambda i, j: (0, j)),
            pl.BlockSpec((D_MODEL, LANES), lambda i, j: (0, 0)),
        ],
        out_specs=[
            pl.BlockSpec((tm, tn), lambda i, j: (i, j)),
            pl.BlockSpec((tm, LANES), lambda i, j: (i, 0)),
        ],
        out_shape=[jax.ShapeDtypeStruct((t, Z_COLS), F32),
                   jax.ShapeDtypeStruct((t, LANES), F32)],
        scratch_shapes=[pltpu.VMEM((tm, D_MODEL), BF16)],
        compiler_params=_cparams(("parallel", "arbitrary")),
        name="in_proj",
    )(x, nw, w_main, w_small)


def _pool_kernel(u_ref, up_ref, w_ref, sc_ref, y_ref, pad_ref, *, tp):
    j = pl.program_id(1)
    pad_ref[0:POOL_HALO, :] = jnp.where(j == 0, 0.0, up_ref[...])
    pad_ref[POOL_HALO:POOL_HALO + tp, :] = u_ref[...]
    t = j * tp + lax.broadcasted_iota(jnp.int32, (tp, POOL_GC), 0)
    for g, win in enumerate(POOL_WINDOWS):
        cols = slice(g * POOL_GC, (g + 1) * POOL_GC)
        tot = pad_ref[POOL_HALO:POOL_HALO + tp, cols]
        for d in range(1, win):
            tot = tot + pad_ref[POOL_HALO - d:POOL_HALO - d + tp, cols]
        cnt = jnp.minimum(t + 1, win).astype(F32)
        mixed = tot / cnt - u_ref[:, cols]
        y = _dot(mixed.astype(BF16), w_ref[g])
        y_ref[:, cols] = y * sc_ref[:, cols]


def _pool(z, pool_w, pool_scale, bsz, seq, tp=512):
    nt = seq // tp
    hb = tp // POOL_HALO
    return pl.pallas_call(
        functools.partial(_pool_kernel, tp=tp),
        grid=(bsz, nt),
        in_specs=[
            pl.BlockSpec((tp, BRANCH_WIDTH), lambda b, j: (b * nt + j, 0)),
            pl.BlockSpec((POOL_HALO, BRANCH_WIDTH),
                         lambda b, j: (jnp.maximum((b * nt + j) * hb - 1, 0), 0)),
            pl.BlockSpec((4, POOL_GC, POOL_GC), lambda b, j: (0, 0, 0)),
            pl.BlockSpec((1, BRANCH_WIDTH), lambda b, j: (0, 0)),
        ],
        out_specs=pl.BlockSpec((tp, BRANCH_WIDTH), lambda b, j: (b * nt + j, 0)),
        out_shape=jax.ShapeDtypeStruct((bsz * seq, BRANCH_WIDTH), F32),
        scratch_shapes=[pltpu.VMEM((POOL_HALO + tp, BRANCH_WIDTH), F32)],
        compiler_params=_cparams(("parallel", "arbitrary")),
        name="pool",
    )(z, z, pool_w, pool_scale)


def _dn_kernel(q_ref, k_ref, v_ref, zg_ref, zs_ref, cwq_ref, cwk_ref, cwv_ref, hp_ref, nw_ref,
               y_ref, pad_ref, halo_ref, state_ref, qs_ref, ks_ref, vs_ref, gs_ref, bs_ref,
               os_ref, *, tc):
    c = pl.program_id(1)

    @pl.when(c == 0)
    def _():
        halo_ref[...] = jnp.zeros_like(halo_ref)
        state_ref[...] = jnp.zeros_like(state_ref)

    for part, (src, cw, dst) in enumerate(((q_ref, cwq_ref, qs_ref), (k_ref, cwk_ref, ks_ref),
                                          (v_ref, cwv_ref, vs_ref))):
        pad_ref[0:SUBLANES, :] = halo_ref[part]
        pad_ref[SUBLANES:SUBLANES + tc, :] = src[...]
        halo_ref[part] = src[tc - SUBLANES:tc, :]
        conv = cw[DN_CONV - 1:DN_CONV, :] * src[...]
        for tap in range(DN_CONV - 1):
            off = SUBLANES - (DN_CONV - 1) + tap
            conv = conv + cw[tap:tap + 1, :] * pad_ref[off:off + tc, :]
        dst[...] = conv * jax.nn.sigmoid(conv)

    zs = zs_ref[...]
    for h in range(DN_HEADS):
        cols = slice(h * DN_HEAD_DIM, (h + 1) * DN_HEAD_DIM)
        q = qs_ref[:, cols]
        k = ks_ref[:, cols]
        q = q * lax.rsqrt(jnp.sum(q * q, axis=-1, keepdims=True) + NORM_EPS)
        k = k * lax.rsqrt(jnp.sum(k * k, axis=-1, keepdims=True) + NORM_EPS)
        qs_ref[:, cols] = q * (DN_HEAD_DIM ** -0.5)
        ks_ref[:, cols] = k
        b_raw = jnp.broadcast_to(zs[:, h:h + 1], (tc, LANES))
        a_raw = jnp.broadcast_to(zs[:, DN_HEADS + h:DN_HEADS + h + 1], (tc, LANES))
        bs_ref[h] = jax.nn.sigmoid(b_raw)
        a_log = hp_ref[h, 0:1, :]
        dt_bias = hp_ref[h, 1:2, :]
        gs_ref[h] = -jnp.exp(a_log) * jnp.logaddexp(a_raw + dt_bias, 0.0)

    hc = DN_HEADS * DN_CHUNK
    ri = lax.broadcasted_iota(jnp.int32, (hc, hc), 0)
    ci = lax.broadcasted_iota(jnp.int32, (hc, hc), 1)
    same_head = (ri // DN_CHUNK) == (ci // DN_CHUNK)
    incl = same_head & (ri >= ci)
    lower_b = incl.astype(BF16)
    level_masks = [((ri // DN_INV_BASE) == (ci // DN_INV_BASE)) & (ri > ci)]
    size = DN_INV_BASE
    while size < DN_CHUNK:
        level_masks.append(((ri // (2 * size)) == (ci // (2 * size)))
                           & ((ri % (2 * size)) >= size) & ((ci % (2 * size)) < size))
        size *= 2
    eye_f = (ri == ci).astype(F32)
    first_lane = (lax.broadcasted_iota(jnp.int32, (hc, LANES), 1) == 0).astype(BF16)

    def stack(ref, rows):
        return jnp.concatenate([ref[rows, h * DN_HEAD_DIM:(h + 1) * DN_HEAD_DIM]
                                for h in range(DN_HEADS)], axis=0)

    def chunk_body(n, carry):
        rows = pl.ds(pl.multiple_of(n * DN_CHUNK, DN_CHUNK), DN_CHUNK)
        q = stack(qs_ref, rows)
        k = stack(ks_ref, rows)
        v = stack(vs_ref, rows)
        g = jnp.concatenate([gs_ref[h, rows, :] for h in range(DN_HEADS)], axis=0)
        beta = jnp.concatenate([bs_ref[h, rows, :] for h in range(DN_HEADS)], axis=0)
        gc = sum(_dot(lower_b, part) for part in _split3(g))
        gc_row = sum(_dot_nt(first_lane, part) for part in _split3(gc))
        diff = jnp.concatenate([gc, gc], axis=1) - gc_row
        decay = jnp.exp(jnp.where(incl, diff, NEG_INF))
        kb = k * beta
        vb = v * beta
        a_kk = _mm_nt(kb, k, DN_PASSES_QK) * decay
        m = -jnp.where(level_masks[0], a_kk, 0.0)
        t_mat = eye_f + m
        for _ in range(2):
            m = _mm(m, m, DN_PASSES_INV)
            t_mat = t_mat + _mm(t_mat, m, DN_PASSES_INV)
        for mask in level_masks[1:]:
            low = jnp.where(mask, a_kk, 0.0)
            t_mat = t_mat - _mm(_mm(t_mat, low, DN_PASSES_INV), t_mat, DN_PASSES_INV)
        egc = jnp.exp(gc)
        uw = _mm(t_mat, jnp.concatenate([vb, kb * egc], axis=1), DN_PASSES_INV)
        u = uw[:, :DN_HEAD_DIM]
        w = uw[:, DN_HEAD_DIM:]
        a_qk = _mm_nt(q, k, DN_PASSES_QK) * decay
        q_dec = q * egc
        g_last = jnp.concatenate(
            [jnp.broadcast_to(gc[(h + 1) * DN_CHUNK - 1:(h + 1) * DN_CHUNK, :], (DN_CHUNK, LANES))
             for h in range(DN_HEADS)], axis=0)
        k_dec = k * jnp.exp(g_last - gc)
        ws, qs_ = [], []
        for h in range(DN_HEADS):
            hr = slice(h * DN_CHUNK, (h + 1) * DN_CHUNK)
            both = _mm(jnp.concatenate([w[hr], q_dec[hr]], axis=0), state_ref[h], DN_PASSES_STATE)
            ws.append(both[:DN_CHUNK])
            qs_.append(both[DN_CHUNK:])
        v_new = u - jnp.concatenate(ws, axis=0)
        o = jnp.concatenate(qs_, axis=0) + _mm(a_qk, v_new, DN_PASSES_STATE)
        for h in range(DN_HEADS):
            hr = slice(h * DN_CHUNK, (h + 1) * DN_CHUNK)
            os_ref[rows, h * DN_HEAD_DIM:(h + 1) * DN_HEAD_DIM] = o[hr]
            decay_last = jnp.exp(gc[(h + 1) * DN_CHUNK - 1:(h + 1) * DN_CHUNK, :])
            state_ref[h] = state_ref[h] * decay_last + _mm_tn(k_dec[hr], v_new[hr], DN_PASSES_STATE)
        return carry

    lax.fori_loop(0, tc // DN_CHUNK, chunk_body, 0)

    for h in range(DN_HEADS):
        cols = slice(h * DN_HEAD_DIM, (h + 1) * DN_HEAD_DIM)
        zg = zg_ref[:, cols]
        y_ref[:, cols] = _rms(os_ref[:, cols], nw_ref[...]) * (zg * jax.nn.sigmoid(zg))


def _deltanet(z, zs, conv_w, head_params, norm_w, bsz, seq, tc=512):
    nt = seq // tc
    wide = DN_HEADS * DN_HEAD_DIM

    def col(cb):
        return pl.BlockSpec((tc, wide), lambda b, c: (b * nt + c, cb))

    def cw(cb):
        return pl.BlockSpec((DN_CONV, wide), lambda b, c: (0, cb))

    return pl.pallas_call(
        functools.partial(_dn_kernel, tc=tc),
        grid=(bsz, nt),
        in_specs=[
            col(COL_DNQ // wide), col(COL_DNK // wide), col(COL_DNV // wide), col(COL_DNZ // wide),
            pl.BlockSpec((tc, LANES), lambda b, c: (b * nt + c, 0)),
            cw(0), cw(1), cw(2),
            pl.BlockSpec((DN_HEADS, SUBLANES, LANES), lambda b, c: (0, 0, 0)),
            pl.BlockSpec((1, DN_HEAD_DIM), lambda b, c: (0, 0)),
        ],
        out_specs=pl.BlockSpec((tc, wide), lambda b, c: (b * nt + c, 0)),
        out_shape=jax.ShapeDtypeStruct((bsz * seq, wide), F32),
        scratch_shapes=[
            pltpu.VMEM((SUBLANES + tc, wide), F32),
            pltpu.VMEM((3, SUBLANES, wide), F32),
            pltpu.VMEM((DN_HEADS, DN_HEAD_DIM, DN_HEAD_DIM), F32),
            pltpu.VMEM((tc, wide), F32), pltpu.VMEM((tc, wide), F32), pltpu.VMEM((tc, wide), F32),
            pltpu.VMEM((DN_HEADS, tc, LANES), F32), pltpu.VMEM((DN_HEADS, tc, LANES), F32),
            pltpu.VMEM((tc, wide), F32),
        ],
        compiler_params=_cparams(("parallel", "arbitrary")),
        name="deltanet",
    )(z, z, z, z, zs, conv_w, conv_w, conv_w, head_params, norm_w)


def _moba_kernel(q_ref, k_ref, v_ref, d_ref, o_ref, kb_ref, vt_ref, bown_ref, bprev_ref, *, nb):
    blk = MOBA_BLOCK
    half = LANES // 2

    kmeans = []
    for j in range(nb):
        kj = k_ref[j * blk:(j + 1) * blk, :]
        kmeans.append(jnp.mean(kj, axis=0, keepdims=True))
        kb_ref[j * blk:(j + 1) * blk, :] = kj.astype(BF16)
        vt_ref[j] = v_ref[j * blk:(j + 1) * blk, :].T.astype(BF16)
    kmean = jnp.concatenate(kmeans, axis=0)

    r = lax.broadcasted_iota(jnp.int32, (LANES, LANES), 0)
    c = lax.broadcasted_iota(jnp.int32, (LANES, LANES), 1)
    for hh in range(2):
        far = jnp.broadcast_to(d_ref[hh, 1:2, :], (LANES, LANES))
        band = pltpu.roll(jnp.broadcast_to(d_ref[hh, 0:1, :], (LANES, LANES)), 0, 1,
                          stride=1, stride_axis=0)
        diag = jnp.where(c >= r, band, NEG_INF)
        off = jnp.where(c < r, band, far)
        lo = hh * blk
        bown_ref[0:LANES, lo:lo + LANES] = diag
        bown_ref[0:LANES, lo + LANES:lo + blk] = off
        bown_ref[LANES:blk, lo:lo + LANES] = jnp.full((LANES, LANES), NEG_INF, F32)
        bown_ref[LANES:blk, lo + LANES:lo + blk] = diag
        bprev_ref[0:LANES, lo:lo + LANES] = far
        bprev_ref[0:LANES, lo + LANES:lo + blk] = far
        bprev_ref[LANES:blk, lo:lo + LANES] = off
        bprev_ref[LANES:blk, lo + LANES:lo + blk] = far
    far_row = jnp.concatenate([jnp.broadcast_to(d_ref[0, 1:2, :], (1, LANES))] * 2
                              + [jnp.broadcast_to(d_ref[1, 1:2, :], (1, LANES))] * 2, axis=1)

    lane = lax.broadcasted_iota(jnp.int32, (blk, LANES), 1)
    blk_id = lax.broadcasted_iota(jnp.int32, (nb, 2 * blk), 0)
    row = lax.broadcasted_iota(jnp.int32, (LANES, blk), 0)
    for i in range(nb):
        qf = q_ref[i * blk:(i + 1) * blk, :] * (MOBA_HEAD_DIM ** -0.5)
        q2 = jnp.concatenate([jnp.where(lane < half, qf, 0.0), jnp.where(lane >= half, qf, 0.0)],
                             axis=0)
        q2b = q2.astype(BF16)

        sel = None
        if i > MOBA_TOPK:
            gate = jnp.where(blk_id < i, _dot_nt(kmean, q2, HIGHEST), NEG_INF)
            sel = []
            for j in range(i):
                gj = gate[j:j + 1, :]
                ahead = (gate > gj) | ((gate == gj) & (blk_id < j))
                rank = jnp.sum(ahead.astype(F32), axis=0, keepdims=True)
                sel.append(rank < MOBA_TOPK)

        scores = []
        for j in range(i + 1):
            s = _dot_nt(kb_ref[j * blk:(j + 1) * blk, :], q2b)
            if j == i:
                s = s + bown_ref[...]
            elif j == i - 1:
                s = s + bprev_ref[...]
                if sel is not None:
                    s = s + jnp.where(sel[j], 0.0, NEG_INF)
            elif sel is not None:
                s = s + jnp.where(sel[j], far_row, NEG_INF)
            else:
                s = s + far_row
            scores.append(s)
        m = functools.reduce(jnp.maximum, [jnp.max(s, axis=0, keepdims=True) for s in scores])
        l = None
        acc = None
        for j, s in enumerate(scores):
            p = jnp.exp(s - m)
            p_sum = jnp.sum(p, axis=0, keepdims=True)
            pv = _dot(vt_ref[j], p.astype(BF16))
            l = p_sum if l is None else l + p_sum
            acc = pv if acc is None else acc + pv
        out_t = acc / l
        out_t = jnp.where(row < half, out_t[:, :blk], out_t[:, blk:])
        o_ref[i * blk:(i + 1) * blk, :] = out_t.T


def _moba(z, dist_bias, bsz, seq):
    nb = seq // MOBA_BLOCK
    pairs = MOBA_HEADS // 2
    qcb = COL_MQ // LANES
    kcb = COL_MK // LANES
    vcb = COL_MV // LANES
    return pl.pallas_call(
        functools.partial(_moba_kernel, nb=nb),
        grid=(bsz, pairs),
        in_specs=[
            pl.BlockSpec((seq, LANES), lambda b, p: (b, qcb + p)),
            pl.BlockSpec((seq, LANES), lambda b, p: (b, kcb + p)),
            pl.BlockSpec((seq, LANES), lambda b, p: (b, vcb + p)),
            pl.BlockSpec((2, SUBLANES, LANES), lambda b, p: (p, 0, 0)),
        ],
        out_specs=pl.BlockSpec((seq, LANES), lambda b, p: (b, p)),
        out_shape=jax.ShapeDtypeStruct((bsz * seq, MOBA_HEADS * MOBA_HEAD_DIM), F32),
        scratch_shapes=[
            pltpu.VMEM((seq, LANES), BF16),
            pltpu.VMEM((nb, LANES, MOBA_BLOCK), BF16),
            pltpu.VMEM((MOBA_BLOCK, 2 * MOBA_BLOCK), F32),
            pltpu.VMEM((MOBA_BLOCK, 2 * MOBA_BLOCK), F32),
        ],
        compiler_params=_cparams(("parallel", "parallel")),
        name="moba",
    )(z, z, z, dist_bias)


def _t5_bucket(rel):
    n = jnp.maximum(-rel, 0)
    exact = REL_BUCKETS // 2
    nf = jnp.maximum(n, 1).astype(F32)
    large = exact + (jnp.log(nf / exact) / math.log(REL_MAX_DIST / exact)
                     * (REL_BUCKETS - exact)).astype(jnp.int32)
    large = jnp.minimum(large, REL_BUCKETS - 1)
    return jnp.where(n < exact, n, large)


def _moba_dist_bias(rel_bias):
    bias_ht = rel_bias.T.astype(F32)
    near = bias_ht[:, _t5_bucket(-jnp.arange(LANES, dtype=jnp.int32))]
    far = bias_ht[:, _t5_bucket(jnp.full((LANES,), -REL_MAX_DIST, jnp.int32))]
    rows = jnp.zeros((MOBA_HEADS, SUBLANES, LANES), F32)
    return rows.at[:, 0, :].set(near).at[:, 1, :].set(far)


def _merge_kernel(ya_ref, yb_ref, yc_ref, ga_ref, gb_ref, gc_ref, x_ref, wb_ref, wo_ref, o_ref):
    merged = None
    for n, (y_ref, g_ref) in enumerate(((ya_ref, ga_ref), (yb_ref, gb_ref), (yc_ref, gc_ref))):
        proj = _dot(y_ref[...].astype(BF16), wb_ref[n])
        gate = jax.nn.sigmoid(g_ref[...])
        term = gate * proj
        merged = term if merged is None else merged + term
    o_ref[...] = x_ref[...] + _dot(merged.astype(BF16), wo_ref[...])


def _merge(ya, yb, yc, z, x, w_branch, w_out, tm=512):
    t = x.shape[0]
    gcb = COL_GATE // D_MODEL
    row = lambda i: (i, 0)
    return pl.pallas_call(
        _merge_kernel,
        grid=(t // tm,),
        in_specs=[
            pl.BlockSpec((tm, BRANCH_WIDTH), row),
            pl.BlockSpec((tm, BRANCH_WIDTH), row),
            pl.BlockSpec((tm, BRANCH_WIDTH), row),
            pl.BlockSpec((tm, D_MODEL), lambda i: (i, gcb)),
            pl.BlockSpec((tm, D_MODEL), lambda i: (i, gcb + 1)),
            pl.BlockSpec((tm, D_MODEL), lambda i: (i, gcb + 2)),
            pl.BlockSpec((tm, D_MODEL), row),
            pl.BlockSpec((3, BRANCH_WIDTH, D_MODEL), lambda i: (0, 0, 0)),
            pl.BlockSpec((D_MODEL, D_MODEL), lambda i: (0, 0)),
        ],
        out_specs=pl.BlockSpec((tm, D_MODEL), row),
        out_shape=jax.ShapeDtypeStruct((t, D_MODEL), F32),
        compiler_params=_cparams(("parallel",)),
        name="merge",
    )(ya, yb, yc, z, z, z, x, w_branch, w_out)


def _ffn_kernel(x_ref, nw_ref, wa_ref, wv_ref, cwa_ref, cwv_ref, wd_ref, o_ref,
                h_ref, acc_ref, carry_ref, pad_ref, *, tm, nf):
    j = pl.program_id(1)
    f = pl.program_id(2)

    @pl.when(f == 0)
    def _():
        h_ref[...] = _rms(x_ref[...], nw_ref[...]).astype(BF16)
        acc_ref[...] = jnp.zeros_like(acc_ref)

    h = h_ref[...]
    conv = []
    for part, (w_ref, cw_ref) in enumerate(((wa_ref, cwa_ref), (wv_ref, cwv_ref))):
        up = _dot(h, w_ref[...])
        pad_ref[part, 0:SUBLANES, :] = jnp.where(j == 0, 0.0, carry_ref[f, part])
        pad_ref[part, SUBLANES:SUBLANES + tm, :] = up
        carry_ref[f, part] = up[tm - SUBLANES:tm, :]
        y = cw_ref[FFN_CONV - 1:FFN_CONV, :] * up
        for tap in range(FFN_CONV - 1):
            off = SUBLANES - (FFN_CONV - 1) + tap
            y = y + cw_ref[tap:tap + 1, :] * pad_ref[part, off:off + tm, :]
        conv.append(y)
    act = jax.nn.gelu(conv[0], approximate=True) * conv[1]
    acc_ref[...] += _dot(act.astype(BF16), wd_ref[...])

    @pl.when(f == nf - 1)
    def _():
        o_ref[...] = x_ref[...] + acc_ref[...]


def _ffn(x, nw, w_up, w_conv, w_down, bsz, seq, tm=1024, fc=256):
    nt = seq // tm
    nf = FFN_DIM // fc
    return pl.pallas_call(
        functools.partial(_ffn_kernel, tm=tm, nf=nf),
        grid=(bsz, nt, nf),
        in_specs=[
            pl.BlockSpec((tm, D_MODEL), lambda b, j, f: (b * nt + j, 0)),
            pl.BlockSpec((1, D_MODEL), lambda b, j, f: (0, 0)),
            pl.BlockSpec((D_MODEL, fc), lambda b, j, f: (0, f)),
            pl.BlockSpec((D_MODEL, fc), lambda b, j, f: (0, nf + f)),
            pl.BlockSpec((FFN_CONV, fc), lambda b, j, f: (0, f)),
            pl.BlockSpec((FFN_CONV, fc), lambda b, j, f: (0, nf + f)),
            pl.BlockSpec((fc, D_MODEL), lambda b, j, f: (f, 0)),
        ],
        out_specs=pl.BlockSpec((tm, D_MODEL), lambda b, j, f: (b * nt + j, 0)),
        out_shape=jax.ShapeDtypeStruct((bsz * seq, D_MODEL), F32),
        scratch_shapes=[
            pltpu.VMEM((tm, D_MODEL), BF16),
            pltpu.VMEM((tm, D_MODEL), F32),
            pltpu.VMEM((nf, 2, SUBLANES, fc), F32),
            pltpu.VMEM((2, SUBLANES + tm, fc), F32),
        ],
        compiler_params=_cparams(("parallel", "arbitrary", "arbitrary")),
        name="ffn",
    )(x, nw, w_up, w_up, w_conv, w_conv, w_down)


def _ple_kernel(x_ref, p_ref, nw_ref, wg_ref, wp_ref, fw_ref, o_ref, *, final):
    x = x_ref[...]
    gate = jax.nn.sigmoid(_dot(_rms(x, nw_ref[...]).astype(BF16), wg_ref[...]))
    x = x + _dot(p_ref[...].astype(BF16), wp_ref[...]) * gate
    if final:
        x = _rms(x, fw_ref[...])
    o_ref[...] = x


def _ple(x, p, nw, w_gate, w_proj, final_w, final, tm=1024):
    t = x.shape[0]
    row = lambda i: (i, 0)
    fixed = lambda i: (0, 0)
    return pl.pallas_call(
        functools.partial(_ple_kernel, final=final),
        grid=(t // tm,),
        in_specs=[
            pl.BlockSpec((tm, D_MODEL), row),
            pl.BlockSpec((tm, PLE_DIM), row),
            pl.BlockSpec((1, D_MODEL), fixed),
            pl.BlockSpec((D_MODEL, D_MODEL), fixed),
            pl.BlockSpec((PLE_DIM, D_MODEL), fixed),
            pl.BlockSpec((1, D_MODEL), fixed),
        ],
        out_specs=pl.BlockSpec((tm, D_MODEL), row),
        out_shape=jax.ShapeDtypeStruct((t, D_MODEL), F32),
        compiler_params=_cparams(("parallel",)),
        name="ple",
    )(x, p, nw, w_gate, w_proj, final_w)


def _split_w_in(w):
    pool = w[:, 0:512]
    dn_qkv = w[:, 512:2048]
    dn_z = w[:, 2048:2560]
    dn_ba = w[:, 2560:2568]
    mb_qkv = w[:, 2568:4104]
    gates = w[:, 4104:7176]
    main = jnp.concatenate([pool, dn_qkv, dn_z, mb_qkv, gates], axis=1).astype(BF16)
    small = jnp.pad(dn_ba, ((0, 0), (0, LANES - 2 * DN_HEADS))).astype(BF16)
    return main, small


def kernel(x, p, rel_bias, norm_mix, w_in, pool_w, pool_scale, dn_conv, dn_a_log, dn_dt_bias,
           dn_norm, w_branch, w_out, norm_ffn, ffn_up, ffn_conv, ffn_down, norm_ple, ple_gate,
           ple_proj, norm_final):
    bsz, seq, d = x.shape
    depth = w_in.shape[0]
    t = bsz * seq
    xt = x.reshape(t, d)
    dist_bias = _moba_dist_bias(rel_bias)
    for i in range(depth):
        w_main, w_small = _split_w_in(w_in[i])
        z, zs = _in_proj(xt, norm_mix[i][None, :], w_main, w_small)
        y_a = _pool(z, pool_w[i].astype(BF16), pool_scale[i][None, :], bsz, seq)
        head_params = jnp.zeros((DN_HEADS, SUBLANES, LANES), F32)
        head_params = head_params.at[:, 0, :].set(dn_a_log[i][:, None])
        head_params = head_params.at[:, 1, :].set(dn_dt_bias[i][:, None])
        y_b = _deltanet(z, zs, dn_conv[i], head_params, dn_norm[i][None, :], bsz, seq)
        y_c = _moba(z, dist_bias, bsz, seq)
        xt = _merge(y_a, y_b, y_c, z, xt, w_branch[i].astype(BF16), w_out[i].astype(BF16))
        xt = _ffn(xt, norm_ffn[i][None, :], ffn_up[i].astype(BF16), ffn_conv[i],
                  ffn_down[i].astype(BF16), bsz, seq)
        xt = _ple(xt, p[i].reshape(t, PLE_DIM), norm_ple[i][None, :], ple_gate[i].astype(BF16),
                  ple_proj[i].astype(BF16), norm_final[None, :], final=(i == depth - 1))
    return xt.reshape(bsz, seq, d)
```

```python
import functools
import math

import jax
import jax.numpy as jnp
from jax import lax
from jax.experimental import pallas as pl
from jax.experimental.pallas import tpu as pltpu

F32 = jnp.float32
BF16 = jnp.bfloat16
HIGHEST = lax.Precision.HIGHEST
ACT = jnp.bfloat16

D_MODEL = 1024
BRANCH_WIDTH = 512
POOL_WINDOWS = (2, 4, 8, 16)
POOL_GC = 128
POOL_HALO = 16
DN_HEADS = 4
DN_HEAD_DIM = 128
DN_CONV = 4
DN_CHUNK = 64
MOBA_HEADS = 8
MOBA_HEAD_DIM = 64
MOBA_BLOCK = 256
MOBA_TOPK = 3
REL_BUCKETS = 32
REL_MAX_DIST = 128
FFN_DIM = 2816
FFN_CONV = 3
PLE_DIM = 256
NORM_EPS = 1e-6
NEG_INF = -1e30
LANES = 128
SUBLANES = 8

COL_POOL = 0
COL_DNQ = 512
COL_DNK = 1024
COL_DNV = 1536
COL_DNZ = 2048
COL_MV = 2560
COL_MQ = 3072
COL_MK = 3584
COL_GATE = 4096
Z_COLS = 7168

VMEM_LIMIT = 56 * 1024 * 1024


def _cparams(sem):
    return pltpu.CompilerParams(dimension_semantics=sem, vmem_limit_bytes=VMEM_LIMIT)


def _dot(a, b, precision=None):
    return lax.dot_general(a, b, (((1,), (0,)), ((), ())), precision=precision,
                           preferred_element_type=F32)


def _dot_nt(a, b, precision=None):
    return lax.dot_general(a, b, (((1,), (1,)), ((), ())), precision=precision,
                           preferred_element_type=F32)


def _dot_tn(a, b, precision=None):
    return lax.dot_general(a, b, (((0,), (0,)), ((), ())), precision=precision,
                           preferred_element_type=F32)


def _split3(x):
    hi = x.astype(BF16)
    rest = x - hi.astype(F32)
    mid = rest.astype(BF16)
    lo = (rest - mid.astype(F32)).astype(BF16)
    return hi, mid, lo


def _mm_dims(a, b, dims, passes):
    dot = lambda x, y: lax.dot_general(x, y, dims, preferred_element_type=F32)
    a_hi = a.astype(BF16)
    b_hi = b.astype(BF16)
    if passes == 1:
        return dot(a_hi, b_hi)
    a_lo = (a - a_hi.astype(F32)).astype(BF16)
    b_lo = (b - b_hi.astype(F32)).astype(BF16)
    return dot(a_hi, b_hi) + (dot(a_hi, b_lo) + dot(a_lo, b_hi))


def _mm(a, b, passes):
    return _mm_dims(a, b, (((1,), (0,)), ((), ())), passes)


def _mm_nt(a, b, passes):
    return _mm_dims(a, b, (((1,), (1,)), ((), ())), passes)


def _mm_tn(a, b, passes):
    return _mm_dims(a, b, (((0,), (0,)), ((), ())), passes)


DN_PASSES_QK = 1
DN_PASSES_INV = 1
DN_PASSES_STATE = 1
DN_INV_BASE = 8
DN_GROUP = 4


def _rms(x, w):
    return x * lax.rsqrt(jnp.mean(x * x, axis=-1, keepdims=True) + NORM_EPS) * w


def _in_proj_kernel(x_ref, nw_ref, w_ref, ws_ref, z_ref, zs_ref, zqk_ref, h_ref, *, qk_block):
    j = pl.program_id(1)

    @pl.when(j == 0)
    def _():
        hb = _rms(x_ref[...], nw_ref[...]).astype(BF16)
        h_ref[...] = hb
        zs_ref[...] = _dot(hb, ws_ref[...])

    z = _dot(h_ref[...], w_ref[...])
    z_ref[...] = z.astype(ACT)

    @pl.when(j == qk_block)
    def _():
        zqk_ref[...] = z


def _in_proj(x, nw, w_main, w_small, tm=1024, tn=1024):
    t = x.shape[0]
    assert COL_MQ % tn == 0 and COL_MK == COL_MQ + tn // 2
    return pl.pallas_call(
        functools.partial(_in_proj_kernel, qk_block=COL_MQ // tn),
        grid=(t // tm, Z_COLS // tn),
        in_specs=[
            pl.BlockSpec((tm, D_MODEL), lambda i, j: (i, 0)),
            pl.BlockSpec((1, D_MODEL), lambda i, j: (0, 0)),
            pl.BlockSpec((D_MODEL, tn), lambda i, j: (0, j)),
            pl.BlockSpec((D_MODEL, LANES), lambda i, j: (0, 0)),
        ],
        out_specs=[
            pl.BlockSpec((tm, tn), lambda i, j: (i, j)),
            pl.BlockSpec((tm, LANES), lambda i, j: (i, 0)),
            pl.BlockSpec((tm, tn), lambda i, j: (i, 0)),
        ],
        out_shape=[jax.ShapeDtypeStruct((t, Z_COLS), ACT),
                   jax.ShapeDtypeStruct((t, LANES), F32),
                   jax.ShapeDtypeStruct((t, tn), F32)],
        scratch_shapes=[pltpu.VMEM((tm, D_MODEL), BF16)],
        compiler_params=_cparams(("parallel", "arbitrary")),
        name="in_proj",
    )(x, nw, w_main, w_small)


def _pool_kernel(u_ref, up_ref, w_ref, sc_ref, y_ref, pad_ref, *, tp):
    j = pl.program_id(1)
    pad_ref[0:POOL_HALO, :] = jnp.where(j == 0, 0.0, up_ref[...].astype(F32))
    pad_ref[POOL_HALO:POOL_HALO + tp, :] = u_ref[...].astype(F32)
    t = j * tp + lax.broadcasted_iota(jnp.int32, (tp, POOL_GC), 0)
    for g, win in enumerate(POOL_WINDOWS):
        cols = slice(g * POOL_GC, (g + 1) * POOL_GC)
        tot = pad_ref[POOL_HALO:POOL_HALO + tp, cols]
        for d in range(1, win):
            tot = tot + pad_ref[POOL_HALO - d:POOL_HALO - d + tp, cols]
        cnt = jnp.minimum(t + 1, win).astype(F32)
        mixed = tot / cnt - pad_ref[POOL_HALO:POOL_HALO + tp, cols]
        y = _dot(mixed.astype(BF16), w_ref[g])
        y_ref[:, cols] = (y * sc_ref[:, cols]).astype(ACT)


def _pool(z, pool_w, pool_scale, bsz, seq, tp=512):
    nt = seq // tp
    hb = tp // POOL_HALO
    return pl.pallas_call(
        functools.partial(_pool_kernel, tp=tp),
        grid=(bsz, nt),
        in_specs=[
            pl.BlockSpec((tp, BRANCH_WIDTH), lambda b, j: (b * nt + j, 0)),
            pl.BlockSpec((POOL_HALO, BRANCH_WIDTH),
                         lambda b, j: (jnp.maximum((b * nt + j) * hb - 1, 0), 0)),
            pl.BlockSpec((4, POOL_GC, POOL_GC), lambda b, j: (0, 0, 0)),
            pl.BlockSpec((1, BRANCH_WIDTH), lambda b, j: (0, 0)),
        ],
        out_specs=pl.BlockSpec((tp, BRANCH_WIDTH), lambda b, j: (b * nt + j, 0)),
        out_shape=jax.ShapeDtypeStruct((bsz * seq, BRANCH_WIDTH), ACT),
        scratch_shapes=[pltpu.VMEM((POOL_HALO + tp, BRANCH_WIDTH), F32)],
        compiler_params=_cparams(("parallel", "arbitrary")),
        name="pool",
    )(z, z, pool_w, pool_scale)


def _dn_kernel(q_ref, k_ref, v_ref, zg_ref, zs_ref, cwq_ref, cwk_ref, cwv_ref, hp_ref, nw_ref,
               y_ref, pad_ref, halo_ref, state_ref, qs_ref, ks_ref, vs_ref, gs_ref, bs_ref,
               os_ref, *, tc):
    c = pl.program_id(1)

    @pl.when(c == 0)
    def _():
        halo_ref[...] = jnp.zeros_like(halo_ref)
        state_ref[...] = jnp.zeros_like(state_ref)

    for part, (src, cw, dst) in enumerate(((q_ref, cwq_ref, qs_ref), (k_ref, cwk_ref, ks_ref),
                                          (v_ref, cwv_ref, vs_ref))):
        pad_ref[0:SUBLANES, :] = halo_ref[part]
        x = src[...].astype(F32)
        pad_ref[SUBLANES:SUBLANES + tc, :] = x
        halo_ref[part] = x[tc - SUBLANES:tc, :]
        conv = cw[DN_CONV - 1:DN_CONV, :] * x
        for tap in range(DN_CONV - 1):
            off = SUBLANES - (DN_CONV - 1) + tap
            conv = conv + cw[tap:tap + 1, :] * pad_ref[off:off + tc, :]
        dst[...] = conv * jax.nn.sigmoid(conv)

    zs = zs_ref[...]
    for h in range(DN_HEADS):
        cols = slice(h * DN_HEAD_DIM, (h + 1) * DN_HEAD_DIM)
        q = qs_ref[:, cols]
        k = ks_ref[:, cols]
        q = q * lax.rsqrt(jnp.sum(q * q, axis=-1, keepdims=True) + NORM_EPS)
        k = k * lax.rsqrt(jnp.sum(k * k, axis=-1, keepdims=True) + NORM_EPS)
        qs_ref[:, cols] = q * (DN_HEAD_DIM ** -0.5)
        ks_ref[:, cols] = k
        b_raw = jnp.broadcast_to(zs[:, h:h + 1], (tc, LANES))
        a_raw = jnp.broadcast_to(zs[:, DN_HEADS + h:DN_HEADS + h + 1], (tc, LANES))
        bs_ref[h] = jax.nn.sigmoid(b_raw)
        a_log = hp_ref[h, 0:1, :]
        dt_bias = hp_ref[h, 1:2, :]
        gs_ref[h] = -jnp.exp(a_log) * jnp.logaddexp(a_raw + dt_bias, 0.0)

    hc = DN_HEADS * DN_CHUNK
    ri = lax.broadcasted_iota(jnp.int32, (hc, hc), 0)
    ci = lax.broadcasted_iota(jnp.int32, (hc, hc), 1)
    same_head = (ri // DN_CHUNK) == (ci // DN_CHUNK)
    incl = same_head & (ri >= ci)
    lower_b = incl.astype(BF16)
    level_masks = [((ri // DN_INV_BASE) == (ci // DN_INV_BASE)) & (ri > ci)]
    size = DN_INV_BASE
    while size < DN_CHUNK:
        level_masks.append(((ri // (2 * size)) == (ci // (2 * size)))
                           & ((ri % (2 * size)) >= size) & ((ci % (2 * size)) < size))
        size *= 2
    eye_f = (ri == ci).astype(F32)

    def stack(ref, rows):
        return jnp.concatenate([ref[rows, h * DN_HEAD_DIM:(h + 1) * DN_HEAD_DIM]
                                for h in range(DN_HEADS)], axis=0)

    heads = range(DN_HEADS)
    last = lambda x, h: x[(h + 1) * DN_CHUNK - 1:(h + 1) * DN_CHUNK, :]
    state = [state_ref[h] for h in heads]
    n_chunks = tc // DN_CHUNK
    for first in range(0, n_chunks, DN_GROUP):
        grp = range(first, min(first + DN_GROUP, n_chunks))
        rows = [slice(n * DN_CHUNK, (n + 1) * DN_CHUNK) for n in grp]
        q = [stack(qs_ref, r) for r in rows]
        k = [stack(ks_ref, r) for r in rows]
        v = [stack(vs_ref, r) for r in rows]
        g = [jnp.concatenate([gs_ref[h, r, :] for h in heads], axis=0) for r in rows]
        beta = [jnp.concatenate([bs_ref[h, r, :] for h in heads], axis=0) for r in rows]
        gc = [sum(_dot(lower_b, part) for part in _split3(x)) for x in g]
        gc_t = [x.T for x in gc]
        decay = [jnp.exp(jnp.where(incl, jnp.concatenate([x, x], axis=1)
                                   - jnp.concatenate([xt, xt], axis=0), NEG_INF))
                 for x, xt in zip(gc, gc_t)]
        kb = [x * b for x, b in zip(k, beta)]
        vb = [x * b for x, b in zip(v, beta)]
        a_kk = [_mm_nt(x, y, DN_PASSES_QK) * d for x, y, d in zip(kb, k, decay)]
        m = [-jnp.where(level_masks[0], a, 0.0) for a in a_kk]
        t_mat = [eye_f + x for x in m]
        for _ in range(2):
            m = [_mm(x, x, DN_PASSES_INV) for x in m]
            t_mat = [t + _mm(t, x, DN_PASSES_INV) for t, x in zip(t_mat, m)]
        for mask in level_masks[1:]:
            low = [_mm(t, jnp.where(mask, a, 0.0), DN_PASSES_INV) for t, a in zip(t_mat, a_kk)]
            t_mat = [t - _mm(x, t, DN_PASSES_INV) for t, x in zip(t_mat, low)]
        egc = [jnp.exp(x) for x in gc]
        uw = [_mm(t, jnp.concatenate([x, y * e], axis=1), DN_PASSES_INV)
              for t, x, y, e in zip(t_mat, vb, kb, egc)]
        a_qk = [_mm_nt(x, y, DN_PASSES_QK) * d for x, y, d in zip(q, k, decay)]
        q_dec = [x * e for x, e in zip(q, egc)]
        k_dec = [x * jnp.exp(jnp.concatenate([jnp.broadcast_to(last(c, h), (DN_CHUNK, LANES))
                                              for h in heads], axis=0) - c)
                 for x, c in zip(k, gc)]
        for i, r in enumerate(rows):
            u = uw[i][:, :DN_HEAD_DIM]
            w = uw[i][:, DN_HEAD_DIM:]
            ws, qs_ = [], []
            for h in heads:
                hr = slice(h * DN_CHUNK, (h + 1) * DN_CHUNK)
                both = _mm(jnp.concatenate([w[hr], q_dec[i][hr]], axis=0), state[h], DN_PASSES_STATE)
                ws.append(both[:DN_CHUNK])
                qs_.append(both[DN_CHUNK:])
            v_new = u - jnp.concatenate(ws, axis=0)
            o = jnp.concatenate(qs_, axis=0) + _mm(a_qk[i], v_new, DN_PASSES_STATE)
            for h in heads:
                hr = slice(h * DN_CHUNK, (h + 1) * DN_CHUNK)
                os_ref[r, h * DN_HEAD_DIM:(h + 1) * DN_HEAD_DIM] = o[hr]
                state[h] = (state[h] * jnp.exp(last(gc[i], h))
                            + _mm_tn(k_dec[i][hr], v_new[hr], DN_PASSES_STATE))
    for h in heads:
        state_ref[h] = state[h]

    for h in range(DN_HEADS):
        cols = slice(h * DN_HEAD_DIM, (h + 1) * DN_HEAD_DIM)
        zg = zg_ref[:, cols].astype(F32)
        y_ref[:, cols] = (_rms(os_ref[:, cols], nw_ref[...]) * (zg * jax.nn.sigmoid(zg))).astype(ACT)


def _deltanet(z, zs, conv_w, head_params, norm_w, bsz, seq, tc=512):
    nt = seq // tc
    wide = DN_HEADS * DN_HEAD_DIM

    def col(cb):
        return pl.BlockSpec((tc, wide), lambda b, c: (b * nt + c, cb))

    def cw(cb):
        return pl.BlockSpec((DN_CONV, wide), lambda b, c: (0, cb))

    return pl.pallas_call(
        functools.partial(_dn_kernel, tc=tc),
        grid=(bsz, nt),
        in_specs=[
            col(COL_DNQ // wide), col(COL_DNK // wide), col(COL_DNV // wide), col(COL_DNZ // wide),
            pl.BlockSpec((tc, LANES), lambda b, c: (b * nt + c, 0)),
            cw(0), cw(1), cw(2),
            pl.BlockSpec((DN_HEADS, SUBLANES, LANES), lambda b, c: (0, 0, 0)),
            pl.BlockSpec((1, DN_HEAD_DIM), lambda b, c: (0, 0)),
        ],
        out_specs=pl.BlockSpec((tc, wide), lambda b, c: (b * nt + c, 0)),
        out_shape=jax.ShapeDtypeStruct((bsz * seq, wide), ACT),
        scratch_shapes=[
            pltpu.VMEM((SUBLANES + tc, wide), F32),
            pltpu.VMEM((3, SUBLANES, wide), F32),
            pltpu.VMEM((DN_HEADS, DN_HEAD_DIM, DN_HEAD_DIM), F32),
            pltpu.VMEM((tc, wide), F32), pltpu.VMEM((tc, wide), F32), pltpu.VMEM((tc, wide), F32),
            pltpu.VMEM((DN_HEADS, tc, LANES), F32), pltpu.VMEM((DN_HEADS, tc, LANES), F32),
            pltpu.VMEM((tc, wide), F32),
        ],
        compiler_params=_cparams(("parallel", "arbitrary")),
        name="deltanet",
    )(z, z, z, z, zs, conv_w, conv_w, conv_w, head_params, norm_w)


def _moba_kernel(q_ref, k_ref, v_ref, d_ref, o_ref, kb_ref, vt_ref, bown_ref, bprev_ref, *, nb):
    blk = MOBA_BLOCK
    half = LANES // 2

    kmeans = []
    for j in range(nb):
        kj = k_ref[j * blk:(j + 1) * blk, :]
        kmeans.append(jnp.mean(kj.astype(F32), axis=0, keepdims=True))
        kb_ref[j * blk:(j + 1) * blk, :] = kj.astype(BF16)
        vt_ref[j] = v_ref[j * blk:(j + 1) * blk, :].astype(F32).T.astype(BF16)
    kmean = jnp.concatenate(kmeans, axis=0)

    r = lax.broadcasted_iota(jnp.int32, (LANES, LANES), 0)
    c = lax.broadcasted_iota(jnp.int32, (LANES, LANES), 1)
    for hh in range(2):
        far = jnp.broadcast_to(d_ref[hh, 1:2, :], (LANES, LANES))
        band = pltpu.roll(jnp.broadcast_to(d_ref[hh, 0:1, :], (LANES, LANES)), 0, 1,
                          stride=1, stride_axis=0)
        diag = jnp.where(c >= r, band, NEG_INF)
        off = jnp.where(c < r, band, far)
        lo = hh * blk
        bown_ref[0:LANES, lo:lo + LANES] = diag
        bown_ref[0:LANES, lo + LANES:lo + blk] = off
        bown_ref[LANES:blk, lo:lo + LANES] = jnp.full((LANES, LANES), NEG_INF, F32)
        bown_ref[LANES:blk, lo + LANES:lo + blk] = diag
        bprev_ref[0:LANES, lo:lo + LANES] = far
        bprev_ref[0:LANES, lo + LANES:lo + blk] = far
        bprev_ref[LANES:blk, lo:lo + LANES] = off
        bprev_ref[LANES:blk, lo + LANES:lo + blk] = far
    far_row = jnp.concatenate([jnp.broadcast_to(d_ref[0, 1:2, :], (1, LANES))] * 2
                              + [jnp.broadcast_to(d_ref[1, 1:2, :], (1, LANES))] * 2, axis=1)

    lane = lax.broadcasted_iota(jnp.int32, (blk, LANES), 1)
    blk_id = lax.broadcasted_iota(jnp.int32, (nb, 2 * blk), 0)
    row = lax.broadcasted_iota(jnp.int32, (LANES, blk), 0)
    for i in range(nb):
        qf = q_ref[i * blk:(i + 1) * blk, :].astype(F32) * (MOBA_HEAD_DIM ** -0.5)
        q2 = jnp.concatenate([jnp.where(lane < half, qf, 0.0), jnp.where(lane >= half, qf, 0.0)],
                             axis=0)
        q2b = q2.astype(BF16)

        sel = None
        if i > MOBA_TOPK:
            gate = jnp.where(blk_id < i, _dot_nt(kmean, q2, HIGHEST), NEG_INF)
            sel = []
            for j in range(i):
                gj = gate[j:j + 1, :]
                ahead = (gate > gj) | ((gate == gj) & (blk_id < j))
                rank = jnp.sum(ahead.astype(F32), axis=0, keepdims=True)
                sel.append(rank < MOBA_TOPK)

        scores = []
        for j in range(i + 1):
            s = _dot_nt(kb_ref[j * blk:(j + 1) * blk, :], q2b)
            if j == i:
                s = s + bown_ref[...]
            elif j == i - 1:
                s = s + bprev_ref[...]
                if sel is not None:
                    s = s + jnp.where(sel[j], 0.0, NEG_INF)
            elif sel is not None:
                s = s + jnp.where(sel[j], far_row, NEG_INF)
            else:
                s = s + far_row
            scores.append(s)
        m = functools.reduce(jnp.maximum, [jnp.max(s, axis=0, keepdims=True) for s in scores])
        l = None
        acc = None
        for j, s in enumerate(scores):
            p = jnp.exp(s - m)
            p_sum = jnp.sum(p, axis=0, keepdims=True)
            pv = _dot(vt_ref[j], p.astype(BF16))
            l = p_sum if l is None else l + p_sum
            acc = pv if acc is None else acc + pv
        out_t = acc / l
        out_t = jnp.where(row < half, out_t[:, :blk], out_t[:, blk:])
        o_ref[i * blk:(i + 1) * blk, :] = out_t.T.astype(ACT)


def _moba(z, zqk, dist_bias, bsz, seq):
    nb = seq // MOBA_BLOCK
    pairs = MOBA_HEADS // 2
    kcb = (COL_MK - COL_MQ) // LANES
    vcb = COL_MV // LANES
    return pl.pallas_call(
        functools.partial(_moba_kernel, nb=nb),
        grid=(bsz, pairs),
        in_specs=[
            pl.BlockSpec((seq, LANES), lambda b, p: (b, p)),
            pl.BlockSpec((seq, LANES), lambda b, p: (b, kcb + p)),
            pl.BlockSpec((seq, LANES), lambda b, p: (b, vcb + p)),
            pl.BlockSpec((2, SUBLANES, LANES), lambda b, p: (p, 0, 0)),
        ],
        out_specs=pl.BlockSpec((seq, LANES), lambda b, p: (b, p)),
        out_shape=jax.ShapeDtypeStruct((bsz * seq, MOBA_HEADS * MOBA_HEAD_DIM), ACT),
        scratch_shapes=[
            pltpu.VMEM((seq, LANES), BF16),
            pltpu.VMEM((nb, LANES, MOBA_BLOCK), BF16),
            pltpu.VMEM((MOBA_BLOCK, 2 * MOBA_BLOCK), F32),
            pltpu.VMEM((MOBA_BLOCK, 2 * MOBA_BLOCK), F32),
        ],
        compiler_params=_cparams(("parallel", "parallel")),
        name="moba",
    )(zqk, zqk, z, dist_bias)


def _t5_bucket(rel):
    n = jnp.maximum(-rel, 0)
    exact = REL_BUCKETS // 2
    nf = jnp.maximum(n, 1).astype(F32)
    large = exact + (jnp.log(nf / exact) / math.log(REL_MAX_DIST / exact)
                     * (REL_BUCKETS - exact)).astype(jnp.int32)
    large = jnp.minimum(large, REL_BUCKETS - 1)
    return jnp.where(n < exact, n, large)


def _moba_dist_bias(rel_bias):
    bias_ht = rel_bias.T.astype(F32)
    near = bias_ht[:, _t5_bucket(-jnp.arange(LANES, dtype=jnp.int32))]
    far = bias_ht[:, _t5_bucket(jnp.full((LANES,), -REL_MAX_DIST, jnp.int32))]
    rows = jnp.zeros((MOBA_HEADS, SUBLANES, LANES), F32)
    return rows.at[:, 0, :].set(near).at[:, 1, :].set(far)


def _merge_kernel(ya_ref, yb_ref, yc_ref, ga_ref, gb_ref, gc_ref, x_ref, wb_ref, wo_ref, o_ref):
    merged = None
    for n, (y_ref, g_ref) in enumerate(((ya_ref, ga_ref), (yb_ref, gb_ref), (yc_ref, gc_ref))):
        proj = _dot(y_ref[...], wb_ref[n])
        gate = jax.nn.sigmoid(g_ref[...].astype(F32))
        term = gate * proj
        merged = term if merged is None else merged + term
    o_ref[...] = x_ref[...] + _dot(merged.astype(BF16), wo_ref[...])


def _merge(ya, yb, yc, z, x, w_branch, w_out, tm=512):
    t = x.shape[0]
    gcb = COL_GATE // D_MODEL
    row = lambda i: (i, 0)
    return pl.pallas_call(
        _merge_kernel,
        grid=(t // tm,),
        in_specs=[
            pl.BlockSpec((tm, BRANCH_WIDTH), row),
            pl.BlockSpec((tm, BRANCH_WIDTH), row),
            pl.BlockSpec((tm, BRANCH_WIDTH), row),
            pl.BlockSpec((tm, D_MODEL), lambda i: (i, gcb)),
            pl.BlockSpec((tm, D_MODEL), lambda i: (i, gcb + 1)),
            pl.BlockSpec((tm, D_MODEL), lambda i: (i, gcb + 2)),
            pl.BlockSpec((tm, D_MODEL), row),
            pl.BlockSpec((3, BRANCH_WIDTH, D_MODEL), lambda i: (0, 0, 0)),
            pl.BlockSpec((D_MODEL, D_MODEL), lambda i: (0, 0)),
        ],
        out_specs=pl.BlockSpec((tm, D_MODEL), row),
        out_shape=jax.ShapeDtypeStruct((t, D_MODEL), F32),
        compiler_params=_cparams(("parallel",)),
        name="merge",
    )(ya, yb, yc, z, z, z, x, w_branch, w_out)


def _ffn_kernel(x_ref, nw_ref, wu_ref, cw_ref, wd_ref, o_ref, act_ref, carry_ref, pad_ref, *, tm, fc):
    j = pl.program_id(1)
    x = x_ref[...]
    h = _rms(x, nw_ref[...]).astype(BF16)
    for f in range(FFN_DIM // fc):
        conv = []
        for part in range(2):
            cols = slice(part * FFN_DIM + f * fc, part * FFN_DIM + (f + 1) * fc)
            buf = 2 * (f % 2) + part
            up = _dot(h, wu_ref[:, cols])
            pad_ref[buf, 0:SUBLANES, :] = jnp.where(j == 0, 0.0, carry_ref[:, cols])
            pad_ref[buf, SUBLANES:SUBLANES + tm, :] = up
            carry_ref[:, cols] = up[tm - SUBLANES:tm, :]
            y = cw_ref[FFN_CONV - 1:FFN_CONV, cols] * up
            for tap in range(FFN_CONV - 1):
                off = SUBLANES - (FFN_CONV - 1) + tap
                y = y + cw_ref[tap:tap + 1, cols] * pad_ref[buf, off:off + tm, :]
            conv.append(y)
        act_ref[:, f * fc:(f + 1) * fc] = (jax.nn.gelu(conv[0], approximate=True) * conv[1]).astype(BF16)
    o_ref[...] = x + _dot(act_ref[...], wd_ref[...])


def _ffn(x, nw, w_up, w_conv, w_down, bsz, seq, tm=512, fc=256):
    nt = seq // tm
    resident = dict(pipeline_mode=pl.Buffered(1))
    return pl.pallas_call(
        functools.partial(_ffn_kernel, tm=tm, fc=fc),
        grid=(bsz, nt),
        in_specs=[
            pl.BlockSpec((tm, D_MODEL), lambda b, j: (b * nt + j, 0)),
            pl.BlockSpec((1, D_MODEL), lambda b, j: (0, 0)),
            pl.BlockSpec((D_MODEL, 2 * FFN_DIM), lambda b, j: (0, 0), **resident),
            pl.BlockSpec((FFN_CONV, 2 * FFN_DIM), lambda b, j: (0, 0)),
            pl.BlockSpec((FFN_DIM, D_MODEL), lambda b, j: (0, 0), **resident),
        ],
        out_specs=pl.BlockSpec((tm, D_MODEL), lambda b, j: (b * nt + j, 0)),
        out_shape=jax.ShapeDtypeStruct((bsz * seq, D_MODEL), F32),
        scratch_shapes=[
            pltpu.VMEM((tm, FFN_DIM), BF16),
            pltpu.VMEM((SUBLANES, 2 * FFN_DIM), F32),
            pltpu.VMEM((4, SUBLANES + tm, fc), F32),
        ],
        compiler_params=_cparams(("parallel", "arbitrary")),
        name="ffn",
    )(x, nw, w_up, w_conv, w_down)


def _ple_kernel(x_ref, p_ref, nw_ref, wg_ref, wp_ref, fw_ref, o_ref, *, final):
    x = x_ref[...]
    gate = jax.nn.sigmoid(_dot(_rms(x, nw_ref[...]).astype(BF16), wg_ref[...]))
    x = x + _dot(p_ref[...].astype(BF16), wp_ref[...]) * gate
    if final:
        x = _rms(x, fw_ref[...])
    o_ref[...] = x


def _ple(x, p, nw, w_gate, w_proj, final_w, final, tm=1024):
    t = x.shape[0]
    row = lambda i: (i, 0)
    fixed = lambda i: (0, 0)
    return pl.pallas_call(
        functools.partial(_ple_kernel, final=final),
        grid=(t // tm,),
        in_specs=[
            pl.BlockSpec((tm, D_MODEL), row),
            pl.BlockSpec((tm, PLE_DIM), row),
            pl.BlockSpec((1, D_MODEL), fixed),
            pl.BlockSpec((D_MODEL, D_MODEL), fixed),
            pl.BlockSpec((PLE_DIM, D_MODEL), fixed),
            pl.BlockSpec((1, D_MODEL), fixed),
        ],
        out_specs=pl.BlockSpec((tm, D_MODEL), row),
        out_shape=jax.ShapeDtypeStruct((t, D_MODEL), F32),
        compiler_params=_cparams(("parallel",)),
        name="ple",
    )(x, p, nw, w_gate, w_proj, final_w)


def _split_w_in(w):
    pool = w[:, 0:512]
    dn_qkv = w[:, 512:2048]
    dn_z = w[:, 2048:2560]
    dn_ba = w[:, 2560:2568]
    mb_qk = w[:, 2568:3592]
    mb_v = w[:, 3592:4104]
    gates = w[:, 4104:7176]
    main = jnp.concatenate([pool, dn_qkv, dn_z, mb_v, mb_qk, gates], axis=1).astype(BF16)
    small = jnp.pad(dn_ba, ((0, 0), (0, LANES - 2 * DN_HEADS))).astype(BF16)
    return main, small


def kernel(x, p, rel_bias, norm_mix, w_in, pool_w, pool_scale, dn_conv, dn_a_log, dn_dt_bias,
           dn_norm, w_branch, w_out, norm_ffn, ffn_up, ffn_conv, ffn_down, norm_ple, ple_gate,
           ple_proj, norm_final):
    bsz, seq, d = x.shape
    depth = w_in.shape[0]
    t = bsz * seq
    xt = x.reshape(t, d)
    dist_bias = _moba_dist_bias(rel_bias)
    for i in range(depth):
        w_main, w_small = _split_w_in(w_in[i])
        z, zs, zqk = _in_proj(xt, norm_mix[i][None, :], w_main, w_small)
        y_a = _pool(z, pool_w[i].astype(BF16), pool_scale[i][None, :], bsz, seq)
        head_params = jnp.zeros((DN_HEADS, SUBLANES, LANES), F32)
        head_params = head_params.at[:, 0, :].set(dn_a_log[i][:, None])
        head_params = head_params.at[:, 1, :].set(dn_dt_bias[i][:, None])
        y_b = _deltanet(z, zs, dn_conv[i], head_params, dn_norm[i][None, :], bsz, seq)
        y_c = _moba(z, zqk, dist_bias, bsz, seq)
        xt = _merge(y_a, y_b, y_c, z, xt, w_branch[i].astype(BF16), w_out[i].astype(BF16))
        xt = _ffn(xt, norm_ffn[i][None, :], ffn_up[i].astype(BF16), ffn_conv[i],
                  ffn_down[i].astype(BF16), bsz, seq)
        xt = _ple(xt, p[i].reshape(t, PLE_DIM), norm_ple[i][None, :], ple_gate[i].astype(BF16),
                  ple_proj[i].astype(BF16), norm_final[None, :], final=(i == depth - 1))
    return xt.reshape(bsz, seq, d)
```

```python
import functools
import math

import jax
import jax.numpy as jnp
from jax import lax
from jax.experimental import pallas as pl
from jax.experimental.pallas import tpu as pltpu

F32 = jnp.float32
BF16 = jnp.bfloat16
HIGHEST = lax.Precision.HIGHEST
ACT = jnp.bfloat16

D_MODEL = 1024
BRANCH_WIDTH = 512
POOL_WINDOWS = (2, 4, 8, 16)
POOL_GC = 128
POOL_HALO = 16
DN_HEADS = 4
DN_HEAD_DIM = 128
DN_CONV = 4
DN_CHUNK = 64
MOBA_HEADS = 8
MOBA_HEAD_DIM = 64
MOBA_BLOCK = 256
MOBA_TOPK = 3
REL_BUCKETS = 32
REL_MAX_DIST = 128
FFN_DIM = 2816
FFN_CONV = 3
PLE_DIM = 256
NORM_EPS = 1e-6
NEG_INF = -1e30
LANES = 128
SUBLANES = 8

COL_POOL = 0
COL_DNQ = 512
COL_DNK = 1024
COL_DNV = 1536
COL_DNZ = 2048
COL_MV = 2560
COL_MQ = 3072
COL_MK = 3584
COL_GATE = 4096
Z_COLS = 7168

VMEM_LIMIT = 56 * 1024 * 1024


def _cparams(sem):
    return pltpu.CompilerParams(dimension_semantics=sem, vmem_limit_bytes=VMEM_LIMIT)


def _dot(a, b, precision=None):
    return lax.dot_general(a, b, (((1,), (0,)), ((), ())), precision=precision,
                           preferred_element_type=F32)


def _dot_nt(a, b, precision=None):
    return lax.dot_general(a, b, (((1,), (1,)), ((), ())), precision=precision,
                           preferred_element_type=F32)


def _dot_tn(a, b, precision=None):
    return lax.dot_general(a, b, (((0,), (0,)), ((), ())), precision=precision,
                           preferred_element_type=F32)


def _split3(x):
    hi = x.astype(BF16)
    rest = x - hi.astype(F32)
    mid = rest.astype(BF16)
    lo = (rest - mid.astype(F32)).astype(BF16)
    return hi, mid, lo


def _mm_dims(a, b, dims, passes):
    dot = lambda x, y: lax.dot_general(x, y, dims, preferred_element_type=F32)
    a_hi = a.astype(BF16)
    b_hi = b.astype(BF16)
    if passes == 1:
        return dot(a_hi, b_hi)
    a_lo = (a - a_hi.astype(F32)).astype(BF16)
    b_lo = (b - b_hi.astype(F32)).astype(BF16)
    return dot(a_hi, b_hi) + (dot(a_hi, b_lo) + dot(a_lo, b_hi))


def _mm(a, b, passes):
    return _mm_dims(a, b, (((1,), (0,)), ((), ())), passes)


def _mm_nt(a, b, passes):
    return _mm_dims(a, b, (((1,), (1,)), ((), ())), passes)


def _mm_tn(a, b, passes):
    return _mm_dims(a, b, (((0,), (0,)), ((), ())), passes)


DN_PASSES_QK = 1
DN_PASSES_INV = 1
DN_PASSES_STATE = 1
DN_INV_BASE = 8
DN_GROUP = 4


def _rms(x, w):
    return x * lax.rsqrt(jnp.mean(x * x, axis=-1, keepdims=True) + NORM_EPS) * w


def _in_proj_kernel(x_ref, nw_ref, w_ref, ws_ref, z_ref, zs_ref, zqk_ref, *, tn):
    hb = _rms(x_ref[...], nw_ref[...]).astype(BF16)
    zs_ref[...] = _dot(hb, ws_ref[...])
    for jb in range(Z_COLS // tn):
        cols = slice(jb * tn, (jb + 1) * tn)
        z = _dot(hb, w_ref[:, cols])
        z_ref[:, cols] = z.astype(ACT)
        if jb == COL_MQ // tn:
            zqk_ref[...] = z


def _in_proj(x, nw, w_main, w_small, tm=512, tn=1024):
    t = x.shape[0]
    assert COL_MQ % tn == 0 and COL_MK == COL_MQ + tn // 2
    resident = dict(pipeline_mode=pl.Buffered(1))
    return pl.pallas_call(
        functools.partial(_in_proj_kernel, tn=tn),
        grid=(t // tm,),
        in_specs=[
            pl.BlockSpec((tm, D_MODEL), lambda i: (i, 0)),
            pl.BlockSpec((1, D_MODEL), lambda i: (0, 0)),
            pl.BlockSpec((D_MODEL, Z_COLS), lambda i: (0, 0), **resident),
            pl.BlockSpec((D_MODEL, LANES), lambda i: (0, 0), **resident),
        ],
        out_specs=[
            pl.BlockSpec((tm, Z_COLS), lambda i: (i, 0)),
            pl.BlockSpec((tm, LANES), lambda i: (i, 0)),
            pl.BlockSpec((tm, tn), lambda i: (i, 0)),
        ],
        out_shape=[jax.ShapeDtypeStruct((t, Z_COLS), ACT),
                   jax.ShapeDtypeStruct((t, LANES), F32),
                   jax.ShapeDtypeStruct((t, tn), F32)],
        compiler_params=_cparams(("parallel",)),
        name="in_proj",
    )(x, nw, w_main, w_small)


def _pool_kernel(u_ref, up_ref, w_ref, sc_ref, y_ref, pad_ref, *, tp):
    j = pl.program_id(1)
    pad_ref[0:POOL_HALO, :] = jnp.where(j == 0, 0.0, up_ref[...].astype(F32))
    pad_ref[POOL_HALO:POOL_HALO + tp, :] = u_ref[...].astype(F32)
    t = j * tp + lax.broadcasted_iota(jnp.int32, (tp, POOL_GC), 0)
    for g, win in enumerate(POOL_WINDOWS):
        cols = slice(g * POOL_GC, (g + 1) * POOL_GC)
        tot = pad_ref[POOL_HALO:POOL_HALO + tp, cols]
        for d in range(1, win):
            tot = tot + pad_ref[POOL_HALO - d:POOL_HALO - d + tp, cols]
        cnt = jnp.minimum(t + 1, win).astype(F32)
        mixed = tot / cnt - pad_ref[POOL_HALO:POOL_HALO + tp, cols]
        y = _dot(mixed.astype(BF16), w_ref[g])
        y_ref[:, cols] = (y * sc_ref[:, cols]).astype(ACT)


def _pool(z, pool_w, pool_scale, bsz, seq, tp=512):
    nt = seq // tp
    hb = tp // POOL_HALO
    return pl.pallas_call(
        functools.partial(_pool_kernel, tp=tp),
        grid=(bsz, nt),
        in_specs=[
            pl.BlockSpec((tp, BRANCH_WIDTH), lambda b, j: (b * nt + j, 0)),
            pl.BlockSpec((POOL_HALO, BRANCH_WIDTH),
                         lambda b, j: (jnp.maximum((b * nt + j) * hb - 1, 0), 0)),
            pl.BlockSpec((4, POOL_GC, POOL_GC), lambda b, j: (0, 0, 0)),
            pl.BlockSpec((1, BRANCH_WIDTH), lambda b, j: (0, 0)),
        ],
        out_specs=pl.BlockSpec((tp, BRANCH_WIDTH), lambda b, j: (b * nt + j, 0)),
        out_shape=jax.ShapeDtypeStruct((bsz * seq, BRANCH_WIDTH), ACT),
        scratch_shapes=[pltpu.VMEM((POOL_HALO + tp, BRANCH_WIDTH), F32)],
        compiler_params=_cparams(("parallel", "arbitrary")),
        name="pool",
    )(z, z, pool_w, pool_scale)


def _dn_kernel(q_ref, k_ref, v_ref, zg_ref, zs_ref, cwq_ref, cwk_ref, cwv_ref, hp_ref, nw_ref, tri_ref,
               y_ref, pad_ref, halo_ref, state_ref, qs_ref, ks_ref, vs_ref, kb_ref, kg_ref, qd_ref,
               kd_ref, gs_ref, os_ref, *, tc):
    c = pl.program_id(1)

    @pl.when(c == 0)
    def _():
        halo_ref[...] = jnp.zeros_like(halo_ref)
        state_ref[...] = jnp.zeros_like(state_ref)

    for part, (src, cw, dst) in enumerate(((q_ref, cwq_ref, qs_ref), (k_ref, cwk_ref, ks_ref),
                                          (v_ref, cwv_ref, vs_ref))):
        pad_ref[0:SUBLANES, :] = halo_ref[part]
        x = src[...].astype(F32)
        pad_ref[SUBLANES:SUBLANES + tc, :] = x
        halo_ref[part] = x[tc - SUBLANES:tc, :]
        conv = cw[DN_CONV - 1:DN_CONV, :] * x
        for tap in range(DN_CONV - 1):
            off = SUBLANES - (DN_CONV - 1) + tap
            conv = conv + cw[tap:tap + 1, :] * pad_ref[off:off + tc, :]
        dst[...] = conv * jax.nn.sigmoid(conv)

    n_chunks = tc // DN_CHUNK
    zs = zs_ref[...]
    beta_c = jax.nn.sigmoid(zs)
    g_c = -jnp.exp(hp_ref[0:1, :]) * jnp.logaddexp(zs + hp_ref[1:2, :], 0.0)
    gc_c = sum(_dot(tri_ref[...], part) for part in _split3(g_c))
    g_last_c = jnp.concatenate(
        [jnp.broadcast_to(gc_c[(n + 1) * DN_CHUNK - 1:(n + 1) * DN_CHUNK, :], (DN_CHUNK, LANES))
         for n in range(n_chunks)], axis=0)
    egc_c = jnp.exp(gc_c)
    kdf_c = jnp.exp(g_last_c - gc_c)

    for h in range(DN_HEADS):
        cols = slice(h * DN_HEAD_DIM, (h + 1) * DN_HEAD_DIM)
        lane_bcast = lambda x, lane: jnp.broadcast_to(x[:, lane:lane + 1], (tc, LANES))
        beta = lane_bcast(beta_c, h)
        egc = lane_bcast(egc_c, DN_HEADS + h)
        q = qs_ref[:, cols]
        k = ks_ref[:, cols]
        q = q * lax.rsqrt(jnp.sum(q * q, axis=-1, keepdims=True) + NORM_EPS) * (DN_HEAD_DIM ** -0.5)
        k = k * lax.rsqrt(jnp.sum(k * k, axis=-1, keepdims=True) + NORM_EPS)
        kb = k * beta
        qs_ref[:, cols] = q
        ks_ref[:, cols] = k
        kb_ref[:, cols] = kb
        vs_ref[:, cols] = vs_ref[:, cols] * beta
        kg_ref[:, cols] = kb * egc
        qd_ref[:, cols] = q * egc
        kd_ref[:, cols] = k * lane_bcast(kdf_c, DN_HEADS + h)
        gs_ref[h] = lane_bcast(gc_c, DN_HEADS + h)

    hc = DN_HEADS * DN_CHUNK
    ri = lax.broadcasted_iota(jnp.int32, (hc, hc), 0)
    ci = lax.broadcasted_iota(jnp.int32, (hc, hc), 1)
    same_head = (ri // DN_CHUNK) == (ci // DN_CHUNK)
    incl = same_head & (ri >= ci)
    level_masks = [((ri // DN_INV_BASE) == (ci // DN_INV_BASE)) & (ri > ci)]
    size = DN_INV_BASE
    while size < DN_CHUNK:
        level_masks.append(((ri // (2 * size)) == (ci // (2 * size)))
                           & ((ri % (2 * size)) >= size) & ((ci % (2 * size)) < size))
        size *= 2
    eye_f = (ri == ci).astype(F32)

    def stack(ref, rows):
        return jnp.concatenate([ref[rows, h * DN_HEAD_DIM:(h + 1) * DN_HEAD_DIM]
                                for h in range(DN_HEADS)], axis=0)

    heads = range(DN_HEADS)
    last = lambda x, h: x[(h + 1) * DN_CHUNK - 1:(h + 1) * DN_CHUNK, :]
    state = [state_ref[h] for h in heads]
    for first in range(0, n_chunks, DN_GROUP):
        grp = range(first, min(first + DN_GROUP, n_chunks))
        rows = [slice(n * DN_CHUNK, (n + 1) * DN_CHUNK) for n in grp]
        q = [stack(qs_ref, r) for r in rows]
        k = [stack(ks_ref, r) for r in rows]
        kb = [stack(kb_ref, r) for r in rows]
        vb = [stack(vs_ref, r) for r in rows]
        kg = [stack(kg_ref, r) for r in rows]
        q_dec = [stack(qd_ref, r) for r in rows]
        k_dec = [stack(kd_ref, r) for r in rows]
        gc = [jnp.concatenate([gs_ref[h, r, :] for h in heads], axis=0) for r in rows]
        gc_t = [x.T for x in gc]
        decay = [jnp.exp(jnp.where(incl, jnp.concatenate([x, x], axis=1)
                                   - jnp.concatenate([xt, xt], axis=0), NEG_INF))
                 for x, xt in zip(gc, gc_t)]
        a_kk =[_mm_nt(x, y, DN_PASSES_QK) * d for x, y, d in zip(kb, k, decay)]
        m = [-jnp.where(level_masks[0], a, 0.0) for a in a_kk]
        t_mat = [eye_f + x for x in m]
        for _ in range(2):
            m = [_mm(x, x, DN_PASSES_INV) for x in m]
            t_mat = [t + _mm(t, x, DN_PASSES_INV) for t, x in zip(t_mat, m)]
        for mask in level_masks[1:]:
            low = [_mm(t, jnp.where(mask, a, 0.0), DN_PASSES_INV) for t, a in zip(t_mat, a_kk)]
            t_mat = [t - _mm(x, t, DN_PASSES_INV) for t, x in zip(t_mat, low)]
        uw = [_mm(t, jnp.concatenate([x, y], axis=1), DN_PASSES_INV)
              for t, x, y in zip(t_mat, vb, kg)]
        a_qk = [_mm_nt(x, y, DN_PASSES_QK) * d for x, y, d in zip(q, k, decay)]
        for i, r in enumerate(rows):
            u = uw[i][:, :DN_HEAD_DIM]
            w = uw[i][:, DN_HEAD_DIM:]
            ws, qs_ = [], []
            for h in heads:
                hr = slice(h * DN_CHUNK, (h + 1) * DN_CHUNK)
                both = _mm(jnp.concatenate([w[hr], q_dec[i][hr]], axis=0), state[h], DN_PASSES_STATE)
                ws.append(both[:DN_CHUNK])
                qs_.append(both[DN_CHUNK:])
            v_new = u - jnp.concatenate(ws, axis=0)
            o = jnp.concatenate(qs_, axis=0) + _mm(a_qk[i], v_new, DN_PASSES_STATE)
            for h in heads:
                hr = slice(h * DN_CHUNK, (h + 1) * DN_CHUNK)
                os_ref[r, h * DN_HEAD_DIM:(h + 1) * DN_HEAD_DIM] = o[hr]
                state[h] = (state[h] * jnp.exp(last(gc[i], h))
                            + _mm_tn(k_dec[i][hr], v_new[hr], DN_PASSES_STATE))
    for h in heads:
        state_ref[h] = state[h]

    for h in range(DN_HEADS):
        cols = slice(h * DN_HEAD_DIM, (h + 1) * DN_HEAD_DIM)
        zg = zg_ref[:, cols].astype(F32)
        y_ref[:, cols] = (_rms(os_ref[:, cols], nw_ref[...]) * (zg * jax.nn.sigmoid(zg))).astype(ACT)


def _deltanet(z, zs, conv_w, head_params, norm_w, bsz, seq, tc=512):
    nt = seq // tc
    wide = DN_HEADS * DN_HEAD_DIM

    def col(cb):
        return pl.BlockSpec((tc, wide), lambda b, c: (b * nt + c, cb))

    def cw(cb):
        return pl.BlockSpec((DN_CONV, wide), lambda b, c: (0, cb))

    pos = jnp.arange(tc, dtype=jnp.int32)
    tri = ((pos[:, None] // DN_CHUNK == pos[None, :] // DN_CHUNK) & (pos[:, None] >= pos[None, :])).astype(BF16)
    tile = pltpu.VMEM((tc, wide), F32)
    return pl.pallas_call(
        functools.partial(_dn_kernel, tc=tc),
        grid=(bsz, nt),
        in_specs=[
            col(COL_DNQ // wide), col(COL_DNK // wide), col(COL_DNV // wide), col(COL_DNZ // wide),
            pl.BlockSpec((tc, LANES), lambda b, c: (b * nt + c, 0)),
            cw(0), cw(1), cw(2),
            pl.BlockSpec((SUBLANES, LANES), lambda b, c: (0, 0)),
            pl.BlockSpec((1, DN_HEAD_DIM), lambda b, c: (0, 0)),
            pl.BlockSpec((tc, tc), lambda b, c: (0, 0)),
        ],
        out_specs=pl.BlockSpec((tc, wide), lambda b, c: (b * nt + c, 0)),
        out_shape=jax.ShapeDtypeStruct((bsz * seq, wide), ACT),
        scratch_shapes=[
            pltpu.VMEM((SUBLANES + tc, wide), F32),
            pltpu.VMEM((3, SUBLANES, wide), F32),
            pltpu.VMEM((DN_HEADS, DN_HEAD_DIM, DN_HEAD_DIM), F32),
            tile, tile, tile, tile, tile, tile, tile,
            pltpu.VMEM((DN_HEADS, tc, LANES), F32),
            tile,
        ],
        compiler_params=_cparams(("parallel", "arbitrary")),
        name="deltanet",
    )(z, z, z, z, zs, conv_w, conv_w, conv_w, head_params, norm_w, tri)


def _moba_kernel(q_ref, k_ref, v_ref, d_ref, o_ref, kb_ref, vt_ref, bown_ref, bprev_ref, *, nb):
    blk = MOBA_BLOCK
    half = LANES // 2

    kmeans = []
    for j in range(nb):
        kj = k_ref[j * blk:(j + 1) * blk, :]
        kmeans.append(jnp.mean(kj.astype(F32), axis=0, keepdims=True))
        kb_ref[j * blk:(j + 1) * blk, :] = kj.astype(BF16)
        vt_ref[j] = v_ref[j * blk:(j + 1) * blk, :].astype(F32).T.astype(BF16)
    kmean = jnp.concatenate(kmeans, axis=0)

    r = lax.broadcasted_iota(jnp.int32, (LANES, LANES), 0)
    c = lax.broadcasted_iota(jnp.int32, (LANES, LANES), 1)
    for hh in range(2):
        far = jnp.broadcast_to(d_ref[hh, 1:2, :], (LANES, LANES))
        band = pltpu.roll(jnp.broadcast_to(d_ref[hh, 0:1, :], (LANES, LANES)), 0, 1,
                          stride=1, stride_axis=0)
        diag = jnp.where(c >= r, band, NEG_INF)
        off = jnp.where(c < r, band, far)
        lo = hh * blk
        bown_ref[0:LANES, lo:lo + LANES] = diag
        bown_ref[0:LANES, lo + LANES:lo + blk] = off
        bown_ref[LANES:blk, lo:lo + LANES] = jnp.full((LANES, LANES), NEG_INF, F32)
        bown_ref[LANES:blk, lo + LANES:lo + blk] = diag
        bprev_ref[0:LANES, lo:lo + LANES] = far
        bprev_ref[0:LANES, lo + LANES:lo + blk] = far
        bprev_ref[LANES:blk, lo:lo + LANES] = off
        bprev_ref[LANES:blk, lo + LANES:lo + blk] = far
    far_row = jnp.concatenate([jnp.broadcast_to(d_ref[0, 1:2, :], (1, LANES))] * 2
                              + [jnp.broadcast_to(d_ref[1, 1:2, :], (1, LANES))] * 2, axis=1)

    lane = lax.broadcasted_iota(jnp.int32, (blk, LANES), 1)
    blk_id = lax.broadcasted_iota(jnp.int32, (nb, 2 * blk), 0)
    row = lax.broadcasted_iota(jnp.int32, (LANES, blk), 0)
    def score_stage(i):
        qf = q_ref[i * blk:(i + 1) * blk, :].astype(F32) * (MOBA_HEAD_DIM ** -0.5)
        q2 = jnp.concatenate([jnp.where(lane < half, qf, 0.0), jnp.where(lane >= half, qf, 0.0)],
                             axis=0)
        q2b = q2.astype(BF16)

        sel = None
        if i > MOBA_TOPK:
            gate = jnp.where(blk_id < i, _dot_nt(kmean, q2, HIGHEST), NEG_INF)
            sel = []
            for j in range(i):
                gj = gate[j:j + 1, :]
                ahead = (gate > gj) | ((gate == gj) & (blk_id < j))
                rank = jnp.sum(ahead.astype(F32), axis=0, keepdims=True)
                sel.append(rank < MOBA_TOPK)

        scores = []
        for j in range(i + 1):
            s = _dot_nt(kb_ref[j * blk:(j + 1) * blk, :], q2b)
            if j == i:
                s = s + bown_ref[...]
            elif j == i - 1:
                s = s + bprev_ref[...]
                if sel is not None:
                    s = s + jnp.where(sel[j], 0.0, NEG_INF)
            elif sel is not None:
                s = s + jnp.where(sel[j], far_row, NEG_INF)
            else:
                s = s + far_row
            scores.append(s)
        return scores

    def exp_stage(scores):
        m = functools.reduce(jnp.maximum, [jnp.max(s, axis=0, keepdims=True) for s in scores])
        probs = []
        l = None
        for s in scores:
            p = jnp.exp(s - m)
            p_sum = jnp.sum(p, axis=0, keepdims=True)
            l = p_sum if l is None else l + p_sum
            probs.append(p.astype(BF16))
        return probs, l

    def pv_stage(i, probs, l):
        acc = None
        for j, p in enumerate(probs):
            pv = _dot(vt_ref[j], p)
            acc = pv if acc is None else acc + pv
        out_t = acc / l
        out_t = jnp.where(row < half, out_t[:, :blk], out_t[:, blk:])
        o_ref[i * blk:(i + 1) * blk, :] = out_t.T.astype(ACT)

    sc_q, ex_q = None, None
    for i in list(reversed(range(nb))) + [None, None]:
        new_sc = (i, score_stage(i)) if i is not None else None
        new_ex = (sc_q[0],) + exp_stage(sc_q[1]) if sc_q is not None else None
        if ex_q is not None:
            pv_stage(*ex_q)
        sc_q, ex_q = new_sc, new_ex


def _moba(z, zqk, dist_bias, bsz, seq):
    nb = seq // MOBA_BLOCK
    pairs = MOBA_HEADS // 2
    kcb = (COL_MK - COL_MQ) // LANES
    vcb = COL_MV // LANES
    return pl.pallas_call(
        functools.partial(_moba_kernel, nb=nb),
        grid=(bsz, pairs),
        in_specs=[
            pl.BlockSpec((seq, LANES), lambda b, p: (b, p)),
            pl.BlockSpec((seq, LANES), lambda b, p: (b, kcb + p)),
            pl.BlockSpec((seq, LANES), lambda b, p: (b, vcb + p)),
            pl.BlockSpec((2, SUBLANES, LANES), lambda b, p: (p, 0, 0)),
        ],
        out_specs=pl.BlockSpec((seq, LANES), lambda b, p: (b, p)),
        out_shape=jax.ShapeDtypeStruct((bsz * seq, MOBA_HEADS * MOBA_HEAD_DIM), ACT),
        scratch_shapes=[
            pltpu.VMEM((seq, LANES), BF16),
            pltpu.VMEM((nb, LANES, MOBA_BLOCK), BF16),
            pltpu.VMEM((MOBA_BLOCK, 2 * MOBA_BLOCK), F32),
            pltpu.VMEM((MOBA_BLOCK, 2 * MOBA_BLOCK), F32),
        ],
        compiler_params=_cparams(("parallel", "parallel")),
        name="moba",
    )(zqk, zqk, z, dist_bias)


def _t5_bucket(rel):
    n = jnp.maximum(-rel, 0)
    exact = REL_BUCKETS // 2
    nf = jnp.maximum(n, 1).astype(F32)
    large = exact + (jnp.log(nf / exact) / math.log(REL_MAX_DIST / exact)
                     * (REL_BUCKETS - exact)).astype(jnp.int32)
    large = jnp.minimum(large, REL_BUCKETS - 1)
    return jnp.where(n < exact, n, large)


def _moba_dist_bias(rel_bias):
    bias_ht = rel_bias.T.astype(F32)
    near = bias_ht[:, _t5_bucket(-jnp.arange(LANES, dtype=jnp.int32))]
    far = bias_ht[:, _t5_bucket(jnp.full((LANES,), -REL_MAX_DIST, jnp.int32))]
    rows = jnp.zeros((MOBA_HEADS, SUBLANES, LANES), F32)
    return rows.at[:, 0, :].set(near).at[:, 1, :].set(far)


def _merge_kernel(ya_ref, yb_ref, yc_ref, ga_ref, gb_ref, gc_ref, x_ref, wb_ref, wo_ref, o_ref):
    merged = None
    for n, (y_ref, g_ref) in enumerate(((ya_ref, ga_ref), (yb_ref, gb_ref), (yc_ref, gc_ref))):
        proj = _dot(y_ref[...], wb_ref[n])
        gate = jax.nn.sigmoid(g_ref[...].astype(F32))
        term = gate * proj
        merged = term if merged is None else merged + term
    o_ref[...] = x_ref[...] + _dot(merged.astype(BF16), wo_ref[...])


def _merge(ya, yb, yc, z, x, w_branch, w_out, tm=512):
    t = x.shape[0]
    gcb = COL_GATE // D_MODEL
    row = lambda i: (i, 0)
    return pl.pallas_call(
        _merge_kernel,
        grid=(t // tm,),
        in_specs=[
            pl.BlockSpec((tm, BRANCH_WIDTH), row),
            pl.BlockSpec((tm, BRANCH_WIDTH), row),
            pl.BlockSpec((tm, BRANCH_WIDTH), row),
            pl.BlockSpec((tm, D_MODEL), lambda i: (i, gcb)),
            pl.BlockSpec((tm, D_MODEL), lambda i: (i, gcb + 1)),
            pl.BlockSpec((tm, D_MODEL), lambda i: (i, gcb + 2)),
            pl.BlockSpec((tm, D_MODEL), row),
            pl.BlockSpec((3, BRANCH_WIDTH, D_MODEL), lambda i: (0, 0, 0)),
            pl.BlockSpec((D_MODEL, D_MODEL), lambda i: (0, 0)),
        ],
        out_specs=pl.BlockSpec((tm, D_MODEL), row),
        out_shape=jax.ShapeDtypeStruct((t, D_MODEL), F32),
        compiler_params=_cparams(("parallel",)),
        name="merge",
    )(ya, yb, yc, z, z, z, x, w_branch, w_out)


def _ffn_kernel(x_ref, nw_ref, wu_ref, cw_ref, wd_ref, o_ref, act_ref, carry_ref, pad_ref, *, tm, fc):
    j = pl.program_id(1)
    x = x_ref[...]
    h = _rms(x, nw_ref[...]).astype(BF16)
    for f in range(FFN_DIM // fc):
        conv = []
        for part in range(2):
            cols = slice(part * FFN_DIM + f * fc, part * FFN_DIM + (f + 1) * fc)
            buf = 2 * (f % 2) + part
            up = _dot(h, wu_ref[:, cols])
            pad_ref[buf, 0:SUBLANES, :] = jnp.where(j == 0, 0.0, carry_ref[:, cols])
            pad_ref[buf, SUBLANES:SUBLANES + tm, :] = up
            carry_ref[:, cols] = up[tm - SUBLANES:tm, :]
            y = cw_ref[FFN_CONV - 1:FFN_CONV, cols] * up
            for tap in range(FFN_CONV - 1):
                off = SUBLANES - (FFN_CONV - 1) + tap
                y = y + cw_ref[tap:tap + 1, cols] * pad_ref[buf, off:off + tm, :]
            conv.append(y)
        act_ref[:, f * fc:(f + 1) * fc] = (jax.nn.gelu(conv[0], approximate=True) * conv[1]).astype(BF16)
    o_ref[...] = x + _dot(act_ref[...], wd_ref[...])


def _ffn(x, nw, w_up, w_conv, w_down, bsz, seq, tm=512, fc=256):
    nt = seq // tm
    resident = dict(pipeline_mode=pl.Buffered(1))
    return pl.pallas_call(
        functools.partial(_ffn_kernel, tm=tm, fc=fc),
        grid=(bsz, nt),
        in_specs=[
            pl.BlockSpec((tm, D_MODEL), lambda b, j: (b * nt + j, 0)),
            pl.BlockSpec((1, D_MODEL), lambda b, j: (0, 0)),
            pl.BlockSpec((D_MODEL, 2 * FFN_DIM), lambda b, j: (0, 0), **resident),
            pl.BlockSpec((FFN_CONV, 2 * FFN_DIM), lambda b, j: (0, 0)),
            pl.BlockSpec((FFN_DIM, D_MODEL), lambda b, j: (0, 0), **resident),
        ],
        out_specs=pl.BlockSpec((tm, D_MODEL), lambda b, j: (b * nt + j, 0)),
        out_shape=jax.ShapeDtypeStruct((bsz * seq, D_MODEL), F32),
        scratch_shapes=[
            pltpu.VMEM((tm, FFN_DIM), BF16),
            pltpu.VMEM((SUBLANES, 2 * FFN_DIM), F32),
            pltpu.VMEM((4, SUBLANES + tm, fc), F32),
        ],
        compiler_params=_cparams(("parallel", "arbitrary")),
        name="ffn",
    )(x, nw, w_up, w_conv, w_down)


def _ple_kernel(x_ref, p_ref, nw_ref, wg_ref, wp_ref, fw_ref, o_ref, *, final):
    x = x_ref[...]
    gate = jax.nn.sigmoid(_dot(_rms(x, nw_ref[...]).astype(BF16), wg_ref[...]))
    x = x + _dot(p_ref[...].astype(BF16), wp_ref[...]) * gate
    if final:
        x = _rms(x, fw_ref[...])
    o_ref[...] = x


def _ple(x, p, nw, w_gate, w_proj, final_w, final, tm=1024):
    t = x.shape[0]
    row = lambda i: (i, 0)
    fixed = lambda i: (0, 0)
    return pl.pallas_call(
        functools.partial(_ple_kernel, final=final),
        grid=(t // tm,),
        in_specs=[
            pl.BlockSpec((tm, D_MODEL), row),
            pl.BlockSpec((tm, PLE_DIM), row),
            pl.BlockSpec((1, D_MODEL), fixed),
            pl.BlockSpec((D_MODEL, D_MODEL), fixed),
            pl.BlockSpec((PLE_DIM, D_MODEL), fixed),
            pl.BlockSpec((1, D_MODEL), fixed),
        ],
        out_specs=pl.BlockSpec((tm, D_MODEL), row),
        out_shape=jax.ShapeDtypeStruct((t, D_MODEL), F32),
        compiler_params=_cparams(("parallel",)),
        name="ple",
    )(x, p, nw, w_gate, w_proj, final_w)


def _split_w_in(w):
    pool = w[:, 0:512]
    dn_qkv = w[:, 512:2048]
    dn_z = w[:, 2048:2560]
    dn_ba = w[:, 2560:2568]
    mb_qk = w[:, 2568:3592]
    mb_v = w[:, 3592:4104]
    gates = w[:, 4104:7176]
    main = jnp.concatenate([pool, dn_qkv, dn_z, mb_v, mb_qk, gates], axis=1).astype(BF16)
    small = jnp.pad(dn_ba, ((0, 0), (0, LANES - 2 * DN_HEADS))).astype(BF16)
    return main, small


def kernel(x, p, rel_bias, norm_mix, w_in, pool_w, pool_scale, dn_conv, dn_a_log, dn_dt_bias,
           dn_norm, w_branch, w_out, norm_ffn, ffn_up, ffn_conv, ffn_down, norm_ple, ple_gate,
           ple_proj, norm_final):
    bsz, seq, d = x.shape
    depth = w_in.shape[0]
    t = bsz * seq
    xt = x.reshape(t, d)
    dist_bias = _moba_dist_bias(rel_bias)
    for i in range(depth):
        w_main, w_small = _split_w_in(w_in[i])
        z, zs, zqk = _in_proj(xt, norm_mix[i][None, :], w_main, w_small)
        y_a = _pool(z, pool_w[i].astype(BF16), pool_scale[i][None, :], bsz, seq)
        head_params = jnp.zeros((SUBLANES, LANES), F32)
        head_params = head_params.at[0, DN_HEADS:2 * DN_HEADS].set(dn_a_log[i])
        head_params = head_params.at[1, DN_HEADS:2 * DN_HEADS].set(dn_dt_bias[i])
        y_b = _deltanet(z, zs, dn_conv[i], head_params, dn_norm[i][None, :], bsz, seq)
        y_c = _moba(z, zqk, dist_bias, bsz, seq)
        xt = _merge(y_a, y_b, y_c, z, xt, w_branch[i].astype(BF16), w_out[i].astype(BF16))
        xt = _ffn(xt, norm_ffn[i][None, :], ffn_up[i].astype(BF16), ffn_conv[i],
                  ffn_down[i].astype(BF16), bsz, seq)
        xt = _ple(xt, p[i].reshape(t, PLE_DIM), norm_ple[i][None, :], ple_gate[i].astype(BF16),
                  ple_proj[i].astype(BF16), norm_final[None, :], final=(i == depth - 1))
    return xt.reshape(bsz, seq, d)
```

```python
import functools
import math

import jax
import jax.numpy as jnp
from jax import lax
from jax.experimental import pallas as pl
from jax.experimental.pallas import tpu as pltpu

F32 = jnp.float32
BF16 = jnp.bfloat16
HIGHEST = lax.Precision.HIGHEST
ACT = jnp.bfloat16

D_MODEL = 1024
BRANCH_WIDTH = 512
POOL_WINDOWS = (2, 4, 8, 16)
POOL_GC = 128
POOL_HALO = 16
DN_HEADS = 4
DN_HEAD_DIM = 128
DN_CONV = 4
DN_CHUNK = 64
MOBA_HEADS = 8
MOBA_HEAD_DIM = 64
MOBA_BLOCK = 256
MOBA_TOPK = 3
REL_BUCKETS = 32
REL_MAX_DIST = 128
FFN_DIM = 2816
FFN_CONV = 3
PLE_DIM = 256
NORM_EPS = 1e-6
NEG_INF = -1e30
LANES = 128
SUBLANES = 8

COL_POOL = 0
COL_DNQ = 512
COL_DNK = 1024
COL_DNV = 1536
COL_DNZ = 2048
COL_MV = 2560
COL_MQ = 3072
COL_MK = 3584
COL_GATE = 4096
Z_COLS = 7168

VMEM_LIMIT = 56 * 1024 * 1024


def _cparams(sem):
    return pltpu.CompilerParams(dimension_semantics=sem, vmem_limit_bytes=VMEM_LIMIT)


def _layer_spec(block, layer, index=None, **kwargs):
    def index_map(*grid):
        rest = index(*grid) if index is not None else (0,) * len(block)
        return (layer,) + tuple(rest)
    return pl.BlockSpec((None,) + tuple(block), index_map, **kwargs)


def _dot(a, b, precision=None):
    return lax.dot_general(a, b, (((1,), (0,)), ((), ())), precision=precision,
                           preferred_element_type=F32)


def _dot_nt(a, b, precision=None):
    return lax.dot_general(a, b, (((1,), (1,)), ((), ())), precision=precision,
                           preferred_element_type=F32)


def _dot_tn(a, b, precision=None):
    return lax.dot_general(a, b, (((0,), (0,)), ((), ())), precision=precision,
                           preferred_element_type=F32)


def _split3(x):
    hi = x.astype(BF16)
    rest = x - hi.astype(F32)
    mid = rest.astype(BF16)
    lo = (rest - mid.astype(F32)).astype(BF16)
    return hi, mid, lo


def _mm_dims(a, b, dims, passes):
    dot = lambda x, y: lax.dot_general(x, y, dims, preferred_element_type=F32)
    a_hi = a.astype(BF16)
    b_hi = b.astype(BF16)
    if passes == 1:
        return dot(a_hi, b_hi)
    a_lo = (a - a_hi.astype(F32)).astype(BF16)
    b_lo = (b - b_hi.astype(F32)).astype(BF16)
    return dot(a_hi, b_hi) + (dot(a_hi, b_lo) + dot(a_lo, b_hi))


def _mm(a, b, passes):
    return _mm_dims(a, b, (((1,), (0,)), ((), ())), passes)


def _mm_nt(a, b, passes):
    return _mm_dims(a, b, (((1,), (1,)), ((), ())), passes)


def _mm_tn(a, b, passes):
    return _mm_dims(a, b, (((0,), (0,)), ((), ())), passes)


DN_PASSES_QK = 1
DN_PASSES_INV = 1
DN_PASSES_STATE = 1
DN_INV_BASE = 8
DN_GROUP = 4


def _rms(x, w):
    return x * lax.rsqrt(jnp.mean(x * x, axis=-1, keepdims=True) + NORM_EPS) * w


def _in_proj_kernel(x_ref, nw_ref, w_ref, ws_ref, z_ref, zs_ref, zqk_ref, *, tn):
    hb = _rms(x_ref[...], nw_ref[...]).astype(BF16)
    zs_ref[...] = _dot(hb, ws_ref[...])
    for jb in range(Z_COLS // tn):
        cols = slice(jb * tn, (jb + 1) * tn)
        z = _dot(hb, w_ref[:, cols])
        z_ref[:, cols] = z.astype(ACT)
        if jb == COL_MQ // tn:
            zqk_ref[...] = z


def _in_proj(x, nw, w_main, w_small, layer, tm=512, tn=1024):
    t = x.shape[0]
    assert COL_MQ % tn == 0 and COL_MK == COL_MQ + tn // 2
    resident = dict(pipeline_mode=pl.Buffered(1))
    return pl.pallas_call(
        functools.partial(_in_proj_kernel, tn=tn),
        grid=(t // tm,),
        in_specs=[
            pl.BlockSpec((tm, D_MODEL), lambda i: (i, 0)),
            _layer_spec((1, D_MODEL), layer),
            _layer_spec((D_MODEL, Z_COLS), layer, **resident),
            _layer_spec((D_MODEL, LANES), layer, **resident),
        ],
        out_specs=[
            pl.BlockSpec((tm, Z_COLS), lambda i: (i, 0)),
            pl.BlockSpec((tm, LANES), lambda i: (i, 0)),
            pl.BlockSpec((tm, tn), lambda i: (i, 0)),
        ],
        out_shape=[jax.ShapeDtypeStruct((t, Z_COLS), ACT),
                   jax.ShapeDtypeStruct((t, LANES), F32),
                   jax.ShapeDtypeStruct((t, tn), F32)],
        compiler_params=_cparams(("parallel",)),
        name="in_proj",
    )(x, nw, w_main, w_small)


def _pool_kernel(u_ref, up_ref, w_ref, sc_ref, y_ref, *, tp):
    j = pl.program_id(1)
    halo = jnp.where(j == 0, 0.0, up_ref[...].astype(F32))
    t = j * tp + lax.broadcasted_iota(jnp.int32, (tp, POOL_GC), 0)
    for g, win in enumerate(POOL_WINDOWS):
        cols = slice(g * POOL_GC, (g + 1) * POOL_GC)
        u = u_ref[:, cols].astype(F32)
        tot = jnp.concatenate([halo[:, cols], u], axis=0)
        width = 1
        while width < win:
            tot = tot + pltpu.roll(tot, width, 0)
            width *= 2
        cnt = jnp.minimum(t + 1, win).astype(F32)
        mixed = tot[POOL_HALO:, :] / cnt - u
        y = _dot(mixed.astype(BF16), w_ref[g])
        y_ref[:, cols] = (y * sc_ref[:, cols]).astype(ACT)


def _pool(z, pool_w, pool_scale, layer, bsz, seq, tp=512):
    nt = seq // tp
    hb = tp // POOL_HALO
    return pl.pallas_call(
        functools.partial(_pool_kernel, tp=tp),
        grid=(bsz, nt),
        in_specs=[
            pl.BlockSpec((tp, BRANCH_WIDTH), lambda b, j: (b * nt + j, 0)),
            pl.BlockSpec((POOL_HALO, BRANCH_WIDTH),
                         lambda b, j: (jnp.maximum((b * nt + j) * hb - 1, 0), 0)),
            _layer_spec((4, POOL_GC, POOL_GC), layer),
            _layer_spec((1, BRANCH_WIDTH), layer),
        ],
        out_specs=pl.BlockSpec((tp, BRANCH_WIDTH), lambda b, j: (b * nt + j, 0)),
        out_shape=jax.ShapeDtypeStruct((bsz * seq, BRANCH_WIDTH), ACT),
        compiler_params=_cparams(("parallel", "arbitrary")),
        name="pool",
    )(z, z, pool_w, pool_scale)


def _dn_kernel(q_ref, k_ref, v_ref, zg_ref, zs_ref, cwq_ref, cwk_ref, cwv_ref, hp_ref, nw_ref, tri_ref,
               y_ref, halo_ref, state_ref, qs_ref, ks_ref, vs_ref, kb_ref, kg_ref, qd_ref,
               kd_ref, gs_ref, os_ref, *, tc):
    c = pl.program_id(1)

    @pl.when(c == 0)
    def _():
        halo_ref[...] = jnp.zeros_like(halo_ref)
        state_ref[...] = jnp.zeros_like(state_ref)

    for part, (src, cw, dst) in enumerate(((q_ref, cwq_ref, qs_ref), (k_ref, cwk_ref, ks_ref),
                                          (v_ref, cwv_ref, vs_ref))):
        x = src[...].astype(F32)
        ext = jnp.concatenate([halo_ref[part], x], axis=0)
        halo_ref[part] = x[tc - SUBLANES:tc, :]
        acc = cw[0:1, :] * ext
        for tap in range(1, DN_CONV):
            acc = cw[tap:tap + 1, :] * ext + pltpu.roll(acc, 1, 0)
        conv = acc[SUBLANES:, :]
        dst[...] = conv * jax.nn.sigmoid(conv)

    n_chunks = tc // DN_CHUNK
    zs = zs_ref[...]
    beta_c = jax.nn.sigmoid(zs)
    g_c = -jnp.exp(hp_ref[0:1, :]) * jnp.logaddexp(zs + hp_ref[1:2, :], 0.0)
    gc_c = sum(_dot(tri_ref[...], part) for part in _split3(g_c))
    g_last_c = jnp.concatenate(
        [jnp.broadcast_to(gc_c[(n + 1) * DN_CHUNK - 1:(n + 1) * DN_CHUNK, :], (DN_CHUNK, LANES))
         for n in range(n_chunks)], axis=0)
    egc_c = jnp.exp(gc_c)
    kdf_c = jnp.exp(g_last_c - gc_c)

    for h in range(DN_HEADS):
        cols = slice(h * DN_HEAD_DIM, (h + 1) * DN_HEAD_DIM)
        lane_bcast = lambda x, lane: jnp.broadcast_to(x[:, lane:lane + 1], (tc, LANES))
        beta = lane_bcast(beta_c, h)
        egc = lane_bcast(egc_c, DN_HEADS + h)
        q = qs_ref[:, cols]
        k = ks_ref[:, cols]
        q = q * lax.rsqrt(jnp.sum(q * q, axis=-1, keepdims=True) + NORM_EPS) * (DN_HEAD_DIM ** -0.5)
        k = k * lax.rsqrt(jnp.sum(k * k, axis=-1, keepdims=True) + NORM_EPS)
        kb = k * beta
        qs_ref[:, cols] = q
        ks_ref[:, cols] = k
        kb_ref[:, cols] = kb
        vs_ref[:, cols] = vs_ref[:, cols] * beta
        kg_ref[:, cols] = kb * egc
        qd_ref[:, cols] = q * egc
        kd_ref[:, cols] = k * lane_bcast(kdf_c, DN_HEADS + h)
        gs_ref[h] = lane_bcast(gc_c, DN_HEADS + h)

    hc = DN_HEADS * DN_CHUNK
    ri = lax.broadcasted_iota(jnp.int32, (hc, hc), 0)
    ci = lax.broadcasted_iota(jnp.int32, (hc, hc), 1)
    same_head = (ri // DN_CHUNK) == (ci // DN_CHUNK)
    incl = same_head & (ri >= ci)
    level_masks = [((ri // DN_INV_BASE) == (ci // DN_INV_BASE)) & (ri > ci)]
    size = DN_INV_BASE
    while size < DN_CHUNK:
        level_masks.append(((ri // (2 * size)) == (ci // (2 * size)))
                           & ((ri % (2 * size)) >= size) & ((ci % (2 * size)) < size))
        size *= 2
    eye_f = (ri == ci).astype(F32)

    def stack(ref, rows):
        return jnp.concatenate([ref[rows, h * DN_HEAD_DIM:(h + 1) * DN_HEAD_DIM]
                                for h in range(DN_HEADS)], axis=0)

    heads = range(DN_HEADS)
    last = lambda x, h: x[(h + 1) * DN_CHUNK - 1:(h + 1) * DN_CHUNK, :]
    state = [state_ref[h] for h in heads]
    for first in range(0, n_chunks, DN_GROUP):
        grp = range(first, min(first + DN_GROUP, n_chunks))
        rows = [slice(n * DN_CHUNK, (n + 1) * DN_CHUNK) for n in grp]
        q = [stack(qs_ref, r) for r in rows]
        k = [stack(ks_ref, r) for r in rows]
        kb = [stack(kb_ref, r) for r in rows]
        vb = [stack(vs_ref, r) for r in rows]
        kg = [stack(kg_ref, r) for r in rows]
        q_dec = [stack(qd_ref, r) for r in rows]
        k_dec = [stack(kd_ref, r) for r in rows]
        gc = [jnp.concatenate([gs_ref[h, r, :] for h in heads], axis=0) for r in rows]
        gc_t = [x.T for x in gc]
        decay = [jnp.exp(jnp.where(incl, jnp.concatenate([x, x], axis=1)
                                   - jnp.concatenate([xt, xt], axis=0), NEG_INF))
                 for x, xt in zip(gc, gc_t)]
        a_kk =[_mm_nt(x, y, DN_PASSES_QK) * d for x, y, d in zip(kb, k, decay)]
        m = [-jnp.where(level_masks[0], a, 0.0) for a in a_kk]
        t_mat = [eye_f + x for x in m]
        for _ in range(2):
            m = [_mm(x, x, DN_PASSES_INV) for x in m]
            t_mat = [t + _mm(t, x, DN_PASSES_INV) for t, x in zip(t_mat, m)]
        for mask in level_masks[1:]:
            low = [_mm(t, jnp.where(mask, a, 0.0), DN_PASSES_INV) for t, a in zip(t_mat, a_kk)]
            t_mat = [t - _mm(x, t, DN_PASSES_INV) for t, x in zip(t_mat, low)]
        uw = [_mm(t, jnp.concatenate([x, y], axis=1), DN_PASSES_INV)
              for t, x, y in zip(t_mat, vb, kg)]
        a_qk = [_mm_nt(x, y, DN_PASSES_QK) * d for x, y, d in zip(q, k, decay)]
        for i, r in enumerate(rows):
            u = uw[i][:, :DN_HEAD_DIM]
            w = uw[i][:, DN_HEAD_DIM:]
            ws, qs_ = [], []
            for h in heads:
                hr = slice(h * DN_CHUNK, (h + 1) * DN_CHUNK)
                both = _mm(jnp.concatenate([w[hr], q_dec[i][hr]], axis=0), state[h], DN_PASSES_STATE)
                ws.append(both[:DN_CHUNK])
                qs_.append(both[DN_CHUNK:])
            v_new = u - jnp.concatenate(ws, axis=0)
            o = jnp.concatenate(qs_, axis=0) + _mm(a_qk[i], v_new, DN_PASSES_STATE)
            for h in heads:
                hr = slice(h * DN_CHUNK, (h + 1) * DN_CHUNK)
                os_ref[r, h * DN_HEAD_DIM:(h + 1) * DN_HEAD_DIM] = o[hr]
                state[h] = (state[h] * jnp.exp(last(gc[i], h))
                            + _mm_tn(k_dec[i][hr], v_new[hr], DN_PASSES_STATE))
    for h in heads:
        state_ref[h] = state[h]

    for h in range(DN_HEADS):
        cols = slice(h * DN_HEAD_DIM, (h + 1) * DN_HEAD_DIM)
        zg = zg_ref[:, cols].astype(F32)
        y_ref[:, cols] = (_rms(os_ref[:, cols], nw_ref[...]) * (zg * jax.nn.sigmoid(zg))).astype(ACT)


def _deltanet(z, zs, conv_w, head_params, norm_w, layer, bsz, seq, tc=512):
    nt = seq // tc
    wide = DN_HEADS * DN_HEAD_DIM

    def col(cb):
        return pl.BlockSpec((tc, wide), lambda b, c: (b * nt + c, cb))

    def cw(cb):
        return _layer_spec((DN_CONV, wide), layer, lambda b, c: (0, cb))

    pos = jnp.arange(tc, dtype=jnp.int32)
    tri = ((pos[:, None] // DN_CHUNK == pos[None, :] // DN_CHUNK) & (pos[:, None] >= pos[None, :])).astype(BF16)
    tile = pltpu.VMEM((tc, wide), F32)
    return pl.pallas_call(
        functools.partial(_dn_kernel, tc=tc),
        grid=(bsz, nt),
        in_specs=[
            col(COL_DNQ // wide), col(COL_DNK // wide), col(COL_DNV // wide), col(COL_DNZ // wide),
            pl.BlockSpec((tc, LANES), lambda b, c: (b * nt + c, 0)),
            cw(0), cw(1), cw(2),
            _layer_spec((SUBLANES, LANES), layer),
            _layer_spec((1, DN_HEAD_DIM), layer),
            pl.BlockSpec((tc, tc), lambda b, c: (0, 0)),
        ],
        out_specs=pl.BlockSpec((tc, wide), lambda b, c: (b * nt + c, 0)),
        out_shape=jax.ShapeDtypeStruct((bsz * seq, wide), ACT),
        scratch_shapes=[
            pltpu.VMEM((3, SUBLANES, wide), F32),
            pltpu.VMEM((DN_HEADS, DN_HEAD_DIM, DN_HEAD_DIM), F32),
            tile, tile, tile, tile, tile, tile, tile,
            pltpu.VMEM((DN_HEADS, tc, LANES), F32),
            tile,
        ],
        compiler_params=_cparams(("parallel", "arbitrary")),
        name="deltanet",
    )(z, z, z, z, zs, conv_w, conv_w, conv_w, head_params, norm_w, tri)


def _moba_kernel(q_ref, k_ref, v_ref, d_ref, o_ref, kb_ref, vt_ref, bown_ref, bprev_ref, *, nb):
    blk = MOBA_BLOCK
    half = LANES // 2

    kmeans = []
    for j in range(nb):
        kj = k_ref[j * blk:(j + 1) * blk, :]
        kmeans.append(jnp.mean(kj.astype(F32), axis=0, keepdims=True))
        kb_ref[j * blk:(j + 1) * blk, :] = kj.astype(BF16)
        vt_ref[j] = v_ref[j * blk:(j + 1) * blk, :].astype(F32).T.astype(BF16)
    kmean = jnp.concatenate(kmeans, axis=0)

    r = lax.broadcasted_iota(jnp.int32, (LANES, LANES), 0)
    c = lax.broadcasted_iota(jnp.int32, (LANES, LANES), 1)
    for hh in range(2):
        far = jnp.broadcast_to(d_ref[hh, 1:2, :], (LANES, LANES))
        band = pltpu.roll(jnp.broadcast_to(d_ref[hh, 0:1, :], (LANES, LANES)), 0, 1,
                          stride=1, stride_axis=0)
        diag = jnp.where(c >= r, band, NEG_INF)
        off = jnp.where(c < r, band, far)
        lo = hh * blk
        bown_ref[0:LANES, lo:lo + LANES] = diag
        bown_ref[0:LANES, lo + LANES:lo + blk] = off
        bown_ref[LANES:blk, lo:lo + LANES] = jnp.full((LANES, LANES), NEG_INF, F32)
        bown_ref[LANES:blk, lo + LANES:lo + blk] = diag
        bprev_ref[0:LANES, lo:lo + LANES] = far
        bprev_ref[0:LANES, lo + LANES:lo + blk] = far
        bprev_ref[LANES:blk, lo:lo + LANES] = off
        bprev_ref[LANES:blk, lo + LANES:lo + blk] = far
    far_row = jnp.concatenate([jnp.broadcast_to(d_ref[0, 1:2, :], (1, LANES))] * 2
                              + [jnp.broadcast_to(d_ref[1, 1:2, :], (1, LANES))] * 2, axis=1)

    lane = lax.broadcasted_iota(jnp.int32, (blk, LANES), 1)
    blk_id = lax.broadcasted_iota(jnp.int32, (nb, 2 * blk), 0)
    row = lax.broadcasted_iota(jnp.int32, (LANES, blk), 0)
    def score_stage(i):
        qf = q_ref[i * blk:(i + 1) * blk, :].astype(F32) * (MOBA_HEAD_DIM ** -0.5)
        q2 = jnp.concatenate([jnp.where(lane < half, qf, 0.0), jnp.where(lane >= half, qf, 0.0)],
                             axis=0)
        q2b = q2.astype(BF16)

        sel = None
        if i > MOBA_TOPK:
            gate = jnp.where(blk_id < i, _dot_nt(kmean, q2, HIGHEST), NEG_INF)
            sel = []
            for j in range(i):
                gj = gate[j:j + 1, :]
                ahead = (gate > gj) | ((gate == gj) & (blk_id < j))
                rank = jnp.sum(ahead.astype(F32), axis=0, keepdims=True)
                sel.append(rank < MOBA_TOPK)

        scores = []
        for j in range(i + 1):
            s = _dot_nt(kb_ref[j * blk:(j + 1) * blk, :], q2b)
            if j == i:
                s = s + bown_ref[...]
            elif j == i - 1:
                s = s + bprev_ref[...]
                if sel is not None:
                    s = s + jnp.where(sel[j], 0.0, NEG_INF)
            elif sel is not None:
                s = s + jnp.where(sel[j], far_row, NEG_INF)
            else:
                s = s + far_row
            scores.append(s)
        return scores

    def exp_stage(scores):
        m = functools.reduce(jnp.maximum, [jnp.max(s, axis=0, keepdims=True) for s in scores])
        probs = []
        l = None
        for s in scores:
            p = jnp.exp(s - m)
            p_sum = jnp.sum(p, axis=0, keepdims=True)
            l = p_sum if l is None else l + p_sum
            probs.append(p.astype(BF16))
        return probs, l

    def pv_stage(i, probs, l):
        acc = None
        for j, p in enumerate(probs):
            pv = _dot(vt_ref[j], p)
            acc = pv if acc is None else acc + pv
        out_t = acc / l
        out_t = jnp.where(row < half, out_t[:, :blk], out_t[:, blk:])
        o_ref[i * blk:(i + 1) * blk, :] = out_t.T.astype(ACT)

    sc_q, ex_q = None, None
    for i in list(reversed(range(nb))) + [None, None]:
        new_sc = (i, score_stage(i)) if i is not None else None
        new_ex = (sc_q[0],) + exp_stage(sc_q[1]) if sc_q is not None else None
        if ex_q is not None:
            pv_stage(*ex_q)
        sc_q, ex_q = new_sc, new_ex


def _moba(z, zqk, dist_bias, bsz, seq):
    nb = seq // MOBA_BLOCK
    pairs = MOBA_HEADS // 2
    kcb = (COL_MK - COL_MQ) // LANES
    vcb = COL_MV // LANES
    return pl.pallas_call(
        functools.partial(_moba_kernel, nb=nb),
        grid=(bsz, pairs),
        in_specs=[
            pl.BlockSpec((seq, LANES), lambda b, p: (b, p)),
            pl.BlockSpec((seq, LANES), lambda b, p: (b, kcb + p)),
            pl.BlockSpec((seq, LANES), lambda b, p: (b, vcb + p)),
            pl.BlockSpec((2, SUBLANES, LANES), lambda b, p: (p, 0, 0)),
        ],
        out_specs=pl.BlockSpec((seq, LANES), lambda b, p: (b, p)),
        out_shape=jax.ShapeDtypeStruct((bsz * seq, MOBA_HEADS * MOBA_HEAD_DIM), ACT),
        scratch_shapes=[
            pltpu.VMEM((seq, LANES), BF16),
            pltpu.VMEM((nb, LANES, MOBA_BLOCK), BF16),
            pltpu.VMEM((MOBA_BLOCK, 2 * MOBA_BLOCK), F32),
            pltpu.VMEM((MOBA_BLOCK, 2 * MOBA_BLOCK), F32),
        ],
        compiler_params=_cparams(("parallel", "parallel")),
        name="moba",
    )(zqk, zqk, z, dist_bias)


def _t5_bucket(rel):
    n = jnp.maximum(-rel, 0)
    exact = REL_BUCKETS // 2
    nf = jnp.maximum(n, 1).astype(F32)
    large = exact + (jnp.log(nf / exact) / math.log(REL_MAX_DIST / exact)
                     * (REL_BUCKETS - exact)).astype(jnp.int32)
    large = jnp.minimum(large, REL_BUCKETS - 1)
    return jnp.where(n < exact, n, large)


def _moba_dist_bias(rel_bias):
    bias_ht = rel_bias.T.astype(F32)
    near = bias_ht[:, _t5_bucket(-jnp.arange(LANES, dtype=jnp.int32))]
    far = bias_ht[:, _t5_bucket(jnp.full((LANES,), -REL_MAX_DIST, jnp.int32))]
    rows = jnp.zeros((MOBA_HEADS, SUBLANES, LANES), F32)
    return rows.at[:, 0, :].set(near).at[:, 1, :].set(far)


def _merge_kernel(ya_ref, yb_ref, yc_ref, ga_ref, gb_ref, gc_ref, x_ref, wb_ref, wo_ref, o_ref):
    merged = None
    for n, (y_ref, g_ref) in enumerate(((ya_ref, ga_ref), (yb_ref, gb_ref), (yc_ref, gc_ref))):
        proj = _dot(y_ref[...], wb_ref[n])
        gate = jax.nn.sigmoid(g_ref[...].astype(F32))
        term = gate * proj
        merged = term if merged is None else merged + term
    o_ref[...] = x_ref[...] + _dot(merged.astype(BF16), wo_ref[...])


def _merge(ya, yb, yc, z, x, w_branch, w_out, layer, tm=512):
    t = x.shape[0]
    gcb = COL_GATE // D_MODEL
    row = lambda i: (i, 0)
    return pl.pallas_call(
        _merge_kernel,
        grid=(t // tm,),
        in_specs=[
            pl.BlockSpec((tm, BRANCH_WIDTH), row),
            pl.BlockSpec((tm, BRANCH_WIDTH), row),
            pl.BlockSpec((tm, BRANCH_WIDTH), row),
            pl.BlockSpec((tm, D_MODEL), lambda i: (i, gcb)),
            pl.BlockSpec((tm, D_MODEL), lambda i: (i, gcb + 1)),
            pl.BlockSpec((tm, D_MODEL), lambda i: (i, gcb + 2)),
            pl.BlockSpec((tm, D_MODEL), row),
            _layer_spec((3, BRANCH_WIDTH, D_MODEL), layer),
            _layer_spec((D_MODEL, D_MODEL), layer),
        ],
        out_specs=pl.BlockSpec((tm, D_MODEL), row),
        out_shape=jax.ShapeDtypeStruct((t, D_MODEL), F32),
        compiler_params=_cparams(("parallel",)),
        name="merge",
    )(ya, yb, yc, z, z, z, x, w_branch, w_out)


def _ffn_ple_kernel(x_ref, nw_ref, wu_ref, cw_ref, wd_ref, p_ref, pnw_ref, wg_ref, wp_ref, fw_ref, o_ref,
                    act_ref, carry_ref, pad_ref, *, tm, fc, final):
    j = pl.program_id(1)
    x = x_ref[...]
    h = _rms(x, nw_ref[...]).astype(BF16)
    for f in range(FFN_DIM // fc):
        conv = []
        for part in range(2):
            cols = slice(part * FFN_DIM + f * fc, part * FFN_DIM + (f + 1) * fc)
            buf = 2 * (f % 2) + part
            up = _dot(h, wu_ref[:, cols])
            pad_ref[buf, 0:SUBLANES, :] = jnp.where(j == 0, 0.0, carry_ref[:, cols])
            pad_ref[buf, SUBLANES:SUBLANES + tm, :] = up
            carry_ref[:, cols] = up[tm - SUBLANES:tm, :]
            y = cw_ref[FFN_CONV - 1:FFN_CONV, cols] * up
            for tap in range(FFN_CONV - 1):
                off = SUBLANES - (FFN_CONV - 1) + tap
                y = y + cw_ref[tap:tap + 1, cols] * pad_ref[buf, off:off + tm, :]
            conv.append(y)
        act_ref[:, f * fc:(f + 1) * fc] = (jax.nn.gelu(conv[0], approximate=True) * conv[1]).astype(BF16)
    x = x + _dot(act_ref[...], wd_ref[...])
    gate = jax.nn.sigmoid(_dot(_rms(x, pnw_ref[...]).astype(BF16), wg_ref[...]))
    x = x + _dot(p_ref[...].astype(BF16), wp_ref[...]) * gate
    if final:
        x = _rms(x, fw_ref[...])
    o_ref[...] = x


def _ffn_ple(x, nw, w_up, w_conv, w_down, p, ple_nw, w_gate, w_proj, final_w, layer, final, bsz, seq,
             tm=512, fc=256):
    nt = seq // tm
    resident = dict(pipeline_mode=pl.Buffered(1))
    row = lambda b, j: (b * nt + j, 0)
    return pl.pallas_call(
        functools.partial(_ffn_ple_kernel, tm=tm, fc=fc, final=final),
        grid=(bsz, nt),
        in_specs=[
            pl.BlockSpec((tm, D_MODEL), row),
            _layer_spec((1, D_MODEL), layer),
            _layer_spec((D_MODEL, 2 * FFN_DIM), layer, **resident),
            _layer_spec((FFN_CONV, 2 * FFN_DIM), layer),
            _layer_spec((FFN_DIM, D_MODEL), layer, **resident),
            _layer_spec((tm, PLE_DIM), layer, row),
            _layer_spec((1, D_MODEL), layer),
            _layer_spec((D_MODEL, D_MODEL), layer, **resident),
            _layer_spec((PLE_DIM, D_MODEL), layer, **resident),
            pl.BlockSpec((1, D_MODEL), lambda b, j: (0, 0)),
        ],
        out_specs=pl.BlockSpec((tm, D_MODEL), row),
        out_shape=jax.ShapeDtypeStruct((bsz * seq, D_MODEL), F32),
        scratch_shapes=[
            pltpu.VMEM((tm, FFN_DIM), BF16),
            pltpu.VMEM((SUBLANES, 2 * FFN_DIM), F32),
            pltpu.VMEM((4, SUBLANES + tm, fc), F32),
        ],
        compiler_params=_cparams(("parallel", "arbitrary")),
        name="ffn_ple",
    )(x, nw, w_up, w_conv, w_down, p, ple_nw, w_gate, w_proj, final_w)


def _split_w_in(w):
    pool = w[..., 0:512]
    dn_qkv = w[..., 512:2048]
    dn_z = w[..., 2048:2560]
    dn_ba = w[..., 2560:2568]
    mb_qk = w[..., 2568:3592]
    mb_v = w[..., 3592:4104]
    gates = w[..., 4104:7176]
    main = jnp.concatenate([pool, dn_qkv, dn_z, mb_v, mb_qk, gates], axis=-1).astype(BF16)
    small = jnp.pad(dn_ba, ((0, 0),) * (w.ndim - 1) + ((0, LANES - 2 * DN_HEADS),)).astype(BF16)
    return main, small


def kernel(x, p, rel_bias, norm_mix, w_in, pool_w, pool_scale, dn_conv, dn_a_log, dn_dt_bias,
           dn_norm, w_branch, w_out, norm_ffn, ffn_up, ffn_conv, ffn_down, norm_ple, ple_gate,
           ple_proj, norm_final):
    bsz, seq, d = x.shape
    depth = w_in.shape[0]
    t = bsz * seq
    xt = x.reshape(t, d)
    row = lambda a: a[:, None, :]
    w_main, w_small = _split_w_in(w_in)
    pool_w, w_branch, w_out = pool_w.astype(BF16), w_branch.astype(BF16), w_out.astype(BF16)
    ffn_up, ffn_down = ffn_up.astype(BF16), ffn_down.astype(BF16)
    ple_gate, ple_proj = ple_gate.astype(BF16), ple_proj.astype(BF16)
    p = p.reshape(depth, t, PLE_DIM)
    dist_bias = _moba_dist_bias(rel_bias)
    head_params = jnp.zeros((depth, SUBLANES, LANES), F32)
    head_params = head_params.at[:, 0, DN_HEADS:2 * DN_HEADS].set(dn_a_log)
    head_params = head_params.at[:, 1, DN_HEADS:2 * DN_HEADS].set(dn_dt_bias)
    for i in range(depth):
        z, zs, zqk = _in_proj(xt, row(norm_mix), w_main, w_small, i)
        y_a = _pool(z, pool_w, row(pool_scale), i, bsz, seq)
        y_b = _deltanet(z, zs, dn_conv, head_params, row(dn_norm), i, bsz, seq)
        y_c = _moba(z, zqk, dist_bias, bsz, seq)
        xt = _merge(y_a, y_b, y_c, z, xt, w_branch, w_out, i)
        xt = _ffn_ple(xt, row(norm_ffn), ffn_up, ffn_conv, ffn_down, p, row(norm_ple), ple_gate, ple_proj,
                      norm_final[None, :], i, i == depth - 1, bsz, seq)
    return xt.reshape(bsz, seq, d)
```

```python
import functools
import math

import jax
import jax.numpy as jnp
from jax import lax
from jax.experimental import pallas as pl
from jax.experimental.pallas import tpu as pltpu

F32 = jnp.float32
BF16 = jnp.bfloat16
HIGHEST = lax.Precision.HIGHEST
ACT = jnp.bfloat16

D_MODEL = 1024
BRANCH_WIDTH = 512
POOL_WINDOWS = (2, 4, 8, 16)
POOL_GC = 128
POOL_HALO = 16
DN_HEADS = 4
DN_HEAD_DIM = 128
DN_CONV = 4
DN_CHUNK = 64
MOBA_HEADS = 8
MOBA_HEAD_DIM = 64
MOBA_BLOCK = 256
MOBA_TOPK = 3
MOBA_ONES_ROWS = 16
REL_BUCKETS = 32
REL_MAX_DIST = 128
FFN_DIM = 2816
FFN_CONV = 3
PLE_DIM = 256
NORM_EPS = 1e-6
NEG_INF = -1e30
LANES = 128
SUBLANES = 8

COL_POOL = 0
COL_DNQ = 512
COL_DNK = 1024
COL_DNV = 1536
COL_DNZ = 2048
COL_MV = 2560
COL_MQ = 3072
COL_MK = 3584
COL_GATE = 4096
Z_COLS = 7168

VMEM_LIMIT = 56 * 1024 * 1024


def _cparams(sem):
    return pltpu.CompilerParams(dimension_semantics=sem, vmem_limit_bytes=VMEM_LIMIT)


def _layer_spec(block, layer, index=None, **kwargs):
    def index_map(*grid):
        rest = index(*grid) if index is not None else (0,) * len(block)
        return (layer,) + tuple(rest)
    return pl.BlockSpec((None,) + tuple(block), index_map, **kwargs)


def _dot(a, b, precision=None):
    return lax.dot_general(a, b, (((1,), (0,)), ((), ())), precision=precision,
                           preferred_element_type=F32)


def _dot_nt(a, b, precision=None):
    return lax.dot_general(a, b, (((1,), (1,)), ((), ())), precision=precision,
                           preferred_element_type=F32)


def _dot_tn(a, b, precision=None):
    return lax.dot_general(a, b, (((0,), (0,)), ((), ())), precision=precision,
                           preferred_element_type=F32)


def _split3(x):
    hi = x.astype(BF16)
    rest = x - hi.astype(F32)
    mid = rest.astype(BF16)
    lo = (rest - mid.astype(F32)).astype(BF16)
    return hi, mid, lo


def _mm_dims(a, b, dims, passes):
    dot = lambda x, y: lax.dot_general(x, y, dims, preferred_element_type=F32)
    a_hi = a.astype(BF16)
    b_hi = b.astype(BF16)
    if passes == 1:
        return dot(a_hi, b_hi)
    a_lo = (a - a_hi.astype(F32)).astype(BF16)
    b_lo = (b - b_hi.astype(F32)).astype(BF16)
    return dot(a_hi, b_hi) + (dot(a_hi, b_lo) + dot(a_lo, b_hi))


def _mm(a, b, passes):
    return _mm_dims(a, b, (((1,), (0,)), ((), ())), passes)


def _mm_nt(a, b, passes):
    return _mm_dims(a, b, (((1,), (1,)), ((), ())), passes)


def _mm_tn(a, b, passes):
    return _mm_dims(a, b, (((0,), (0,)), ((), ())), passes)


DN_PASSES_QK = 1
DN_PASSES_INV = 1
DN_PASSES_STATE = 1
DN_INV_BASE = 8
DN_GROUP = 4


def _rms(x, w):
    return x * lax.rsqrt(jnp.mean(x * x, axis=-1, keepdims=True) + NORM_EPS) * w


def _in_proj_kernel(x_ref, nw_ref, w_ref, ws_ref, z_ref, zs_ref, zqk_ref, *, tn):
    hb = _rms(x_ref[...], nw_ref[...]).astype(BF16)
    zs_ref[...] = _dot(hb, ws_ref[...])
    for jb in range(Z_COLS // tn):
        cols = slice(jb * tn, (jb + 1) * tn)
        z = _dot(hb, w_ref[:, cols])
        z_ref[:, cols] = z.astype(ACT)
        if jb == COL_MQ // tn:
            zqk_ref[...] = z


def _in_proj(x, nw, w_main, w_small, layer, tm=512, tn=1024):
    t = x.shape[0]
    assert COL_MQ % tn == 0 and COL_MK == COL_MQ + tn // 2
    resident = dict(pipeline_mode=pl.Buffered(1))
    return pl.pallas_call(
        functools.partial(_in_proj_kernel, tn=tn),
        grid=(t // tm,),
        in_specs=[
            pl.BlockSpec((tm, D_MODEL), lambda i: (i, 0)),
            _layer_spec((1, D_MODEL), layer),
            _layer_spec((D_MODEL, Z_COLS), layer, **resident),
            _layer_spec((D_MODEL, LANES), layer, **resident),
        ],
        out_specs=[
            pl.BlockSpec((tm, Z_COLS), lambda i: (i, 0)),
            pl.BlockSpec((tm, LANES), lambda i: (i, 0)),
            pl.BlockSpec((tm, tn), lambda i: (i, 0)),
        ],
        out_shape=[jax.ShapeDtypeStruct((t, Z_COLS), ACT),
                   jax.ShapeDtypeStruct((t, LANES), F32),
                   jax.ShapeDtypeStruct((t, tn), F32)],
        compiler_params=_cparams(("parallel",)),
        name="in_proj",
    )(x, nw, w_main, w_small)


def _pool_mixer(u_ref, up_ref, w_ref, sc_ref, j, tp):
    halo = jnp.where(j == 0, 0.0, up_ref[...].astype(F32))
    t = j * tp + lax.broadcasted_iota(jnp.int32, (tp, POOL_GC), 0)
    out = []
    for g, win in enumerate(POOL_WINDOWS):
        cols = slice(g * POOL_GC, (g + 1) * POOL_GC)
        u = u_ref[:, cols].astype(F32)
        tot = jnp.concatenate([halo[:, cols], u], axis=0)
        width = 1
        while width < win:
            tot = tot + pltpu.roll(tot, width, 0)
            width *= 2
        cnt = jnp.minimum(t + 1, win).astype(F32)
        mixed = tot[POOL_HALO:, :] / cnt - u
        y = _dot(mixed.astype(BF16), w_ref[g])
        out.append((y * sc_ref[:, cols]).astype(BF16))
    return jnp.concatenate(out, axis=1)


def _dn_kernel(q_ref, k_ref, v_ref, zg_ref, zs_ref, cwq_ref, cwk_ref, cwv_ref, hp_ref, nw_ref, tri_ref,
               y_ref, halo_ref, state_ref, qs_ref, ks_ref, vs_ref, kb_ref, kg_ref, qd_ref,
               kd_ref, gs_ref, os_ref, *, tc):
    c = pl.program_id(1)

    @pl.when(c == 0)
    def _():
        halo_ref[...] = jnp.zeros_like(halo_ref)
        state_ref[...] = jnp.zeros_like(state_ref)

    for part, (src, cw, dst) in enumerate(((q_ref, cwq_ref, qs_ref), (k_ref, cwk_ref, ks_ref),
                                          (v_ref, cwv_ref, vs_ref))):
        x = src[...].astype(F32)
        ext = jnp.concatenate([halo_ref[part], x], axis=0)
        halo_ref[part] = x[tc - SUBLANES:tc, :]
        acc = cw[0:1, :] * ext
        for tap in range(1, DN_CONV):
            acc = cw[tap:tap + 1, :] * ext + pltpu.roll(acc, 1, 0)
        conv = acc[SUBLANES:, :]
        dst[...] = conv * jax.nn.sigmoid(conv)

    n_chunks = tc // DN_CHUNK
    zs = zs_ref[...]
    beta_c = jax.nn.sigmoid(zs)
    g_c = -jnp.exp(hp_ref[0:1, :]) * jnp.logaddexp(zs + hp_ref[1:2, :], 0.0)
    gc_c = sum(_dot(tri_ref[...], part) for part in _split3(g_c))
    g_last_c = jnp.concatenate(
        [jnp.broadcast_to(gc_c[(n + 1) * DN_CHUNK - 1:(n + 1) * DN_CHUNK, :], (DN_CHUNK, LANES))
         for n in range(n_chunks)], axis=0)
    egc_c = jnp.exp(gc_c)
    kdf_c = jnp.exp(g_last_c - gc_c)

    for h in range(DN_HEADS):
        cols = slice(h * DN_HEAD_DIM, (h + 1) * DN_HEAD_DIM)
        lane_bcast = lambda x, lane: jnp.broadcast_to(x[:, lane:lane + 1], (tc, LANES))
        beta = lane_bcast(beta_c, h)
        egc = lane_bcast(egc_c, DN_HEADS + h)
        q = qs_ref[:, cols]
        k = ks_ref[:, cols]
        q = q * lax.rsqrt(jnp.sum(q * q, axis=-1, keepdims=True) + NORM_EPS) * (DN_HEAD_DIM ** -0.5)
        k = k * lax.rsqrt(jnp.sum(k * k, axis=-1, keepdims=True) + NORM_EPS)
        kb = k * beta
        qs_ref[:, cols] = q
        ks_ref[:, cols] = k
        kb_ref[:, cols] = kb
        vs_ref[:, cols] = vs_ref[:, cols] * beta
        kg_ref[:, cols] = kb * egc
        qd_ref[:, cols] = q * egc
        kd_ref[:, cols] = k * lane_bcast(kdf_c, DN_HEADS + h)
        gs_ref[h] = lane_bcast(gc_c, DN_HEADS + h)

    hc = DN_HEADS * DN_CHUNK
    ri = lax.broadcasted_iota(jnp.int32, (hc, hc), 0)
    ci = lax.broadcasted_iota(jnp.int32, (hc, hc), 1)
    same_head = (ri // DN_CHUNK) == (ci // DN_CHUNK)
    incl = same_head & (ri >= ci)
    level_masks = [((ri // DN_INV_BASE) == (ci // DN_INV_BASE)) & (ri > ci)]
    size = DN_INV_BASE
    while size < DN_CHUNK:
        level_masks.append(((ri // (2 * size)) == (ci // (2 * size)))
                           & ((ri % (2 * size)) >= size) & ((ci % (2 * size)) < size))
        size *= 2
    eye_f = (ri == ci).astype(F32)

    def stack(ref, rows):
        return jnp.concatenate([ref[rows, h * DN_HEAD_DIM:(h + 1) * DN_HEAD_DIM]
                                for h in range(DN_HEADS)], axis=0)

    heads = range(DN_HEADS)
    last = lambda x, h: x[(h + 1) * DN_CHUNK - 1:(h + 1) * DN_CHUNK, :]
    state = [state_ref[h] for h in heads]
    for first in range(0, n_chunks, DN_GROUP):
        grp = range(first, min(first + DN_GROUP, n_chunks))
        rows = [slice(n * DN_CHUNK, (n + 1) * DN_CHUNK) for n in grp]
        q = [stack(qs_ref, r) for r in rows]
        k = [stack(ks_ref, r) for r in rows]
        kb = [stack(kb_ref, r) for r in rows]
        vb = [stack(vs_ref, r) for r in rows]
        kg = [stack(kg_ref, r) for r in rows]
        q_dec = [stack(qd_ref, r) for r in rows]
        k_dec = [stack(kd_ref, r) for r in rows]
        gc = [jnp.concatenate([gs_ref[h, r, :] for h in heads], axis=0) for r in rows]
        gc_t = [x.T for x in gc]
        decay = [jnp.exp(jnp.where(incl, jnp.concatenate([x, x], axis=1)
                                   - jnp.concatenate([xt, xt], axis=0), NEG_INF))
                 for x, xt in zip(gc, gc_t)]
        a_kk =[_mm_nt(x, y, DN_PASSES_QK) * d for x, y, d in zip(kb, k, decay)]
        m = [-jnp.where(level_masks[0], a, 0.0) for a in a_kk]
        t_mat = [eye_f + x for x in m]
        for _ in range(2):
            m = [_mm(x, x, DN_PASSES_INV) for x in m]
            t_mat = [t + _mm(t, x, DN_PASSES_INV) for t, x in zip(t_mat, m)]
        for mask in level_masks[1:]:
            low = [_mm(t, jnp.where(mask, a, 0.0), DN_PASSES_INV) for t, a in zip(t_mat, a_kk)]
            t_mat = [t - _mm(x, t, DN_PASSES_INV) for t, x in zip(t_mat, low)]
        uw = [_mm(t, jnp.concatenate([x, y], axis=1), DN_PASSES_INV)
              for t, x, y in zip(t_mat, vb, kg)]
        a_qk = [_mm_nt(x, y, DN_PASSES_QK) * d for x, y, d in zip(q, k, decay)]
        for i, r in enumerate(rows):
            u = uw[i][:, :DN_HEAD_DIM]
            w = uw[i][:, DN_HEAD_DIM:]
            ws, qs_ = [], []
            for h in heads:
                hr = slice(h * DN_CHUNK, (h + 1) * DN_CHUNK)
                both = _mm(jnp.concatenate([w[hr], q_dec[i][hr]], axis=0), state[h], DN_PASSES_STATE)
                ws.append(both[:DN_CHUNK])
                qs_.append(both[DN_CHUNK:])
            v_new = u - jnp.concatenate(ws, axis=0)
            o = jnp.concatenate(qs_, axis=0) + _mm(a_qk[i], v_new, DN_PASSES_STATE)
            for h in heads:
                hr = slice(h * DN_CHUNK, (h + 1) * DN_CHUNK)
                os_ref[r, h * DN_HEAD_DIM:(h + 1) * DN_HEAD_DIM] = o[hr]
                state[h] = (state[h] * jnp.exp(last(gc[i], h))
                            + _mm_tn(k_dec[i][hr], v_new[hr], DN_PASSES_STATE))
    for h in heads:
        state_ref[h] = state[h]

    for h in range(DN_HEADS):
        cols = slice(h * DN_HEAD_DIM, (h + 1) * DN_HEAD_DIM)
        zg = zg_ref[:, cols].astype(F32)
        y_ref[:, cols] = (_rms(os_ref[:, cols], nw_ref[...]) * (zg * jax.nn.sigmoid(zg))).astype(ACT)


def _deltanet(z, zs, conv_w, head_params, norm_w, layer, bsz, seq, tc=512):
    nt = seq // tc
    wide = DN_HEADS * DN_HEAD_DIM

    def col(cb):
        return pl.BlockSpec((tc, wide), lambda b, c: (b * nt + c, cb))

    def cw(cb):
        return _layer_spec((DN_CONV, wide), layer, lambda b, c: (0, cb))

    pos = jnp.arange(tc, dtype=jnp.int32)
    tri = ((pos[:, None] // DN_CHUNK == pos[None, :] // DN_CHUNK) & (pos[:, None] >= pos[None, :])).astype(BF16)
    tile = pltpu.VMEM((tc, wide), F32)
    return pl.pallas_call(
        functools.partial(_dn_kernel, tc=tc),
        grid=(bsz, nt),
        in_specs=[
            col(COL_DNQ // wide), col(COL_DNK // wide), col(COL_DNV // wide), col(COL_DNZ // wide),
            pl.BlockSpec((tc, LANES), lambda b, c: (b * nt + c, 0)),
            cw(0), cw(1), cw(2),
            _layer_spec((SUBLANES, LANES), layer),
            _layer_spec((1, DN_HEAD_DIM), layer),
            pl.BlockSpec((tc, tc), lambda b, c: (0, 0)),
        ],
        out_specs=pl.BlockSpec((tc, wide), lambda b, c: (b * nt + c, 0)),
        out_shape=jax.ShapeDtypeStruct((bsz * seq, wide), ACT),
        scratch_shapes=[
            pltpu.VMEM((3, SUBLANES, wide), F32),
            pltpu.VMEM((DN_HEADS, DN_HEAD_DIM, DN_HEAD_DIM), F32),
            tile, tile, tile, tile, tile, tile, tile,
            pltpu.VMEM((DN_HEADS, tc, LANES), F32),
            tile,
        ],
        compiler_params=_cparams(("parallel", "arbitrary")),
        name="deltanet",
    )(z, z, z, z, zs, conv_w, conv_w, conv_w, head_params, norm_w, tri)


def _moba_kernel(q_ref, k_ref, v_ref, d_ref, o_ref, kb_ref, vt_ref, bown_ref, bprev_ref, *, nb):
    blk = MOBA_BLOCK
    half = LANES // 2

    kmeans = []
    for j in range(nb):
        kj = k_ref[j * blk:(j + 1) * blk, :]
        kmeans.append(jnp.mean(kj.astype(F32), axis=0, keepdims=True))
        kb_ref[j * blk:(j + 1) * blk, :] = kj.astype(BF16)
        vt_ref[j, 0:LANES, :] = v_ref[j * blk:(j + 1) * blk, :].astype(F32).T.astype(BF16)
        vt_ref[j, LANES:LANES + MOBA_ONES_ROWS, :] = jnp.ones((MOBA_ONES_ROWS, blk), BF16)
    kmean = jnp.concatenate(kmeans, axis=0)

    r = lax.broadcasted_iota(jnp.int32, (LANES, LANES), 0)
    c = lax.broadcasted_iota(jnp.int32, (LANES, LANES), 1)
    for hh in range(2):
        far = jnp.broadcast_to(d_ref[hh, 1:2, :], (LANES, LANES))
        band = pltpu.roll(jnp.broadcast_to(d_ref[hh, 0:1, :], (LANES, LANES)), 0, 1,
                          stride=1, stride_axis=0)
        diag = jnp.where(c >= r, band, NEG_INF)
        off = jnp.where(c < r, band, far)
        lo = hh * blk
        bown_ref[0:LANES, lo:lo + LANES] = diag
        bown_ref[0:LANES, lo + LANES:lo + blk] = off
        bown_ref[LANES:blk, lo:lo + LANES] = jnp.full((LANES, LANES), NEG_INF, F32)
        bown_ref[LANES:blk, lo + LANES:lo + blk] = diag
        bprev_ref[0:LANES, lo:lo + LANES] = far
        bprev_ref[0:LANES, lo + LANES:lo + blk] = far
        bprev_ref[LANES:blk, lo:lo + LANES] = off
        bprev_ref[LANES:blk, lo + LANES:lo + blk] = far
    far_row = jnp.concatenate([jnp.broadcast_to(d_ref[0, 1:2, :], (1, LANES))] * 2
                              + [jnp.broadcast_to(d_ref[1, 1:2, :], (1, LANES))] * 2, axis=1)

    lane = lax.broadcasted_iota(jnp.int32, (blk, LANES), 1)
    blk_id = lax.broadcasted_iota(jnp.int32, (nb, 2 * blk), 0)
    row = lax.broadcasted_iota(jnp.int32, (LANES, blk), 0)
    def score_stage(i):
        qf = q_ref[i * blk:(i + 1) * blk, :].astype(F32) * (MOBA_HEAD_DIM ** -0.5)
        q2 = jnp.concatenate([jnp.where(lane < half, qf, 0.0), jnp.where(lane >= half, qf, 0.0)],
                             axis=0)
        q2b = q2.astype(BF16)

        sel = None
        if i > MOBA_TOPK:
            gate = jnp.where(blk_id < i, _dot_nt(kmean, q2, HIGHEST), NEG_INF)
            sel = []
            for j in range(i):
                gj = gate[j:j + 1, :]
                ahead = (gate > gj) | ((gate == gj) & (blk_id < j))
                rank = jnp.sum(ahead.astype(F32), axis=0, keepdims=True)
                sel.append(rank < MOBA_TOPK)

        scores = []
        for j in range(i + 1):
            s = _dot_nt(kb_ref[j * blk:(j + 1) * blk, :], q2b)
            if j == i:
                s = s + bown_ref[...]
            elif j == i - 1:
                s = s + bprev_ref[...]
                if sel is not None:
                    s = s + jnp.where(sel[j], 0.0, NEG_INF)
            elif sel is not None:
                s = s + jnp.where(sel[j], far_row, NEG_INF)
            else:
                s = s + far_row
            scores.append(s)
        return scores

    def exp_stage(scores):
        m = functools.reduce(jnp.maximum, [jnp.max(s, axis=0, keepdims=True) for s in scores])
        return ([jnp.exp(s - m).astype(BF16) for s in scores],)

    def pv_stage(i, probs):
        acc = None
        for j, p in enumerate(probs):
            pv = _dot(vt_ref[j], p)
            acc = pv if acc is None else acc + pv
        out_t = acc[:LANES, :] / acc[LANES:LANES + 1, :]
        out_t = jnp.where(row < half, out_t[:, :blk], out_t[:, blk:])
        o_ref[i * blk:(i + 1) * blk, :] = out_t.T.astype(ACT)

    sc_q, ex_q = None, None
    for i in list(reversed(range(nb))) + [None, None]:
        new_sc = (i, score_stage(i)) if i is not None else None
        new_ex = (sc_q[0],) + exp_stage(sc_q[1]) if sc_q is not None else None
        if ex_q is not None:
            pv_stage(*ex_q)
        sc_q, ex_q = new_sc, new_ex


def _moba(z, zqk, dist_bias, bsz, seq):
    nb = seq // MOBA_BLOCK
    pairs = MOBA_HEADS // 2
    kcb = (COL_MK - COL_MQ) // LANES
    vcb = COL_MV // LANES
    return pl.pallas_call(
        functools.partial(_moba_kernel, nb=nb),
        grid=(bsz, pairs),
        in_specs=[
            pl.BlockSpec((seq, LANES), lambda b, p: (b, p)),
            pl.BlockSpec((seq, LANES), lambda b, p: (b, kcb + p)),
            pl.BlockSpec((seq, LANES), lambda b, p: (b, vcb + p)),
            pl.BlockSpec((2, SUBLANES, LANES), lambda b, p: (p, 0, 0)),
        ],
        out_specs=pl.BlockSpec((seq, LANES), lambda b, p: (b, p)),
        out_shape=jax.ShapeDtypeStruct((bsz * seq, MOBA_HEADS * MOBA_HEAD_DIM), ACT),
        scratch_shapes=[
            pltpu.VMEM((seq, LANES), BF16),
            pltpu.VMEM((nb, LANES + MOBA_ONES_ROWS, MOBA_BLOCK), BF16),
            pltpu.VMEM((MOBA_BLOCK, 2 * MOBA_BLOCK), F32),
            pltpu.VMEM((MOBA_BLOCK, 2 * MOBA_BLOCK), F32),
        ],
        compiler_params=_cparams(("parallel", "parallel")),
        name="moba",
    )(zqk, zqk, z, dist_bias)


def _t5_bucket(rel):
    n = jnp.maximum(-rel, 0)
    exact = REL_BUCKETS // 2
    nf = jnp.maximum(n, 1).astype(F32)
    large = exact + (jnp.log(nf / exact) / math.log(REL_MAX_DIST / exact)
                     * (REL_BUCKETS - exact)).astype(jnp.int32)
    large = jnp.minimum(large, REL_BUCKETS - 1)
    return jnp.where(n < exact, n, large)


def _moba_dist_bias(rel_bias):
    bias_ht = rel_bias.T.astype(F32)
    near = bias_ht[:, _t5_bucket(-jnp.arange(LANES, dtype=jnp.int32))]
    far = bias_ht[:, _t5_bucket(jnp.full((LANES,), -REL_MAX_DIST, jnp.int32))]
    rows = jnp.zeros((MOBA_HEADS, SUBLANES, LANES), F32)
    return rows.at[:, 0, :].set(near).at[:, 1, :].set(far)


def _merge_kernel(u_ref, up_ref, pw_ref, ps_ref, yb_ref, yc_ref, ga_ref, gb_ref, gc_ref, x_ref, wb_ref,
                  wo_ref, o_ref, *, tm):
    y_a = _pool_mixer(u_ref, up_ref, pw_ref, ps_ref, pl.program_id(1), tm)
    merged = None
    for n, (y, g_ref) in enumerate(((y_a, ga_ref), (yb_ref[...], gb_ref), (yc_ref[...], gc_ref))):
        proj = _dot(y, wb_ref[n])
        gate = jax.nn.sigmoid(g_ref[...].astype(F32))
        term = gate * proj
        merged = term if merged is None else merged + term
    o_ref[...] = x_ref[...] + _dot(merged.astype(BF16), wo_ref[...])


def _merge(yb, yc, z, x, pool_w, pool_scale, w_branch, w_out, layer, bsz, seq, tm=512):
    nt = seq // tm
    hb = tm // POOL_HALO
    gcb = COL_GATE // D_MODEL
    row = lambda b, j: (b * nt + j, 0)
    return pl.pallas_call(
        functools.partial(_merge_kernel, tm=tm),
        grid=(bsz, nt),
        in_specs=[
            pl.BlockSpec((tm, BRANCH_WIDTH), row),
            pl.BlockSpec((POOL_HALO, BRANCH_WIDTH),
                         lambda b, j: (jnp.maximum((b * nt + j) * hb - 1, 0), 0)),
            _layer_spec((4, POOL_GC, POOL_GC), layer),
            _layer_spec((1, BRANCH_WIDTH), layer),
            pl.BlockSpec((tm, BRANCH_WIDTH), row),
            pl.BlockSpec((tm, BRANCH_WIDTH), row),
            pl.BlockSpec((tm, D_MODEL), lambda b, j: (b * nt + j, gcb)),
            pl.BlockSpec((tm, D_MODEL), lambda b, j: (b * nt + j, gcb + 1)),
            pl.BlockSpec((tm, D_MODEL), lambda b, j: (b * nt + j, gcb + 2)),
            pl.BlockSpec((tm, D_MODEL), row),
            _layer_spec((3, BRANCH_WIDTH, D_MODEL), layer),
            _layer_spec((D_MODEL, D_MODEL), layer),
        ],
        out_specs=pl.BlockSpec((tm, D_MODEL), row),
        out_shape=jax.ShapeDtypeStruct((bsz * seq, D_MODEL), F32),
        compiler_params=_cparams(("parallel", "parallel")),
        name="merge",
    )(z, z, pool_w, pool_scale, yb, yc, z, z, z, x, w_branch, w_out)


def _ffn_ple_kernel(x_ref, nw_ref, wu_ref, cw_ref, wd_ref, p_ref, pnw_ref, wg_ref, wp_ref, fw_ref, o_ref,
                    act_ref, carry_ref, pad_ref, *, tm, fc, final):
    j = pl.program_id(1)
    x = x_ref[...]
    h = _rms(x, nw_ref[...]).astype(BF16)
    for f in range(FFN_DIM // fc):
        conv = []
        for part in range(2):
            cols = slice(part * FFN_DIM + f * fc, part * FFN_DIM + (f + 1) * fc)
            up = _dot(h, wu_ref[:, cols])
            ext = jnp.concatenate([jnp.where(j == 0, 0.0, carry_ref[:, cols]), up], axis=0)
            carry_ref[:, cols] = up[tm - SUBLANES:tm, :]
            acc = cw_ref[0:1, cols] * ext
            for tap in range(1, FFN_CONV):
                acc = cw_ref[tap:tap + 1, cols] * ext + pltpu.roll(acc, 1, 0)
            conv.append(acc[SUBLANES:, :])
        act_ref[:, f * fc:(f + 1) * fc] = (jax.nn.gelu(conv[0], approximate=True) * conv[1]).astype(BF16)
    x = x + _dot(act_ref[...], wd_ref[...])
    gate = jax.nn.sigmoid(_dot(_rms(x, pnw_ref[...]).astype(BF16), wg_ref[...]))
    x = x + _dot(p_ref[...].astype(BF16), wp_ref[...]) * gate
    if final:
        x = _rms(x, fw_ref[...])
    o_ref[...] = x


def _ffn_ple(x, nw, w_up, w_conv, w_down, p, ple_nw, w_gate, w_proj, final_w, layer, final, bsz, seq,
             tm=512, fc=256):
    nt = seq // tm
    resident = dict(pipeline_mode=pl.Buffered(1))
    row = lambda b, j: (b * nt + j, 0)
    return pl.pallas_call(
        functools.partial(_ffn_ple_kernel, tm=tm, fc=fc, final=final),
        grid=(bsz, nt),
        in_specs=[
            pl.BlockSpec((tm, D_MODEL), row),
            _layer_spec((1, D_MODEL), layer),
            _layer_spec((D_MODEL, 2 * FFN_DIM), layer, **resident),
            _layer_spec((FFN_CONV, 2 * FFN_DIM), layer),
            _layer_spec((FFN_DIM, D_MODEL), layer, **resident),
            _layer_spec((tm, PLE_DIM), layer, row),
            _layer_spec((1, D_MODEL), layer),
            _layer_spec((D_MODEL, D_MODEL), layer, **resident),
            _layer_spec((PLE_DIM, D_MODEL), layer, **resident),
            pl.BlockSpec((1, D_MODEL), lambda b, j: (0, 0)),
        ],
        out_specs=pl.BlockSpec((tm, D_MODEL), row),
        out_shape=jax.ShapeDtypeStruct((bsz * seq, D_MODEL), F32),
        scratch_shapes=[
            pltpu.VMEM((tm, FFN_DIM), BF16),
            pltpu.VMEM((SUBLANES, 2 * FFN_DIM), F32),
            pltpu.VMEM((4, SUBLANES + tm, fc), F32),
        ],
        compiler_params=_cparams(("parallel", "arbitrary")),
        name="ffn_ple",
    )(x, nw, w_up, w_conv, w_down, p, ple_nw, w_gate, w_proj, final_w)


def _split_w_in(w):
    pool = w[..., 0:512]
    dn_qkv = w[..., 512:2048]
    dn_z = w[..., 2048:2560]
    dn_ba = w[..., 2560:2568]
    mb_qk = w[..., 2568:3592]
    mb_v = w[..., 3592:4104]
    gates = w[..., 4104:7176]
    main = jnp.concatenate([pool, dn_qkv, dn_z, mb_v, mb_qk, gates], axis=-1).astype(BF16)
    small = jnp.pad(dn_ba, ((0, 0),) * (w.ndim - 1) + ((0, LANES - 2 * DN_HEADS),)).astype(BF16)
    return main, small


def kernel(x, p, rel_bias, norm_mix, w_in, pool_w, pool_scale, dn_conv, dn_a_log, dn_dt_bias,
           dn_norm, w_branch, w_out, norm_ffn, ffn_up, ffn_conv, ffn_down, norm_ple, ple_gate,
           ple_proj, norm_final):
    bsz, seq, d = x.shape
    depth = w_in.shape[0]
    t = bsz * seq
    xt = x.reshape(t, d)
    row = lambda a: a[:, None, :]
    w_main, w_small = _split_w_in(w_in)
    pool_w, w_branch, w_out = pool_w.astype(BF16), w_branch.astype(BF16), w_out.astype(BF16)
    ffn_up, ffn_down = ffn_up.astype(BF16), ffn_down.astype(BF16)
    ple_gate, ple_proj = ple_gate.astype(BF16), ple_proj.astype(BF16)
    p = p.reshape(depth, t, PLE_DIM)
    dist_bias = _moba_dist_bias(rel_bias)
    head_params = jnp.zeros((depth, SUBLANES, LANES), F32)
    head_params = head_params.at[:, 0, DN_HEADS:2 * DN_HEADS].set(dn_a_log)
    head_params = head_params.at[:, 1, DN_HEADS:2 * DN_HEADS].set(dn_dt_bias)
    for i in range(depth):
        z, zs, zqk = _in_proj(xt, row(norm_mix), w_main, w_small, i)
        y_b = _deltanet(z, zs, dn_conv, head_params, row(dn_norm), i, bsz, seq)
        y_c = _moba(z, zqk, dist_bias, bsz, seq)
        xt = _merge(y_b, y_c, z, xt, pool_w, row(pool_scale), w_branch, w_out, i, bsz, seq)
        xt = _ffn_ple(xt, row(norm_ffn), ffn_up, ffn_conv, ffn_down, p, row(norm_ple), ple_gate, ple_proj,
                      norm_final[None, :], i, i == depth - 1, bsz, seq)
    return xt.reshape(bsz, seq, d)
```

```python
import functools
import math

import jax
import jax.numpy as jnp
from jax import lax
from jax.experimental import pallas as pl
from jax.experimental.pallas import tpu as pltpu

F32 = jnp.float32
BF16 = jnp.bfloat16
HIGHEST = lax.Precision.HIGHEST
ACT = jnp.bfloat16

D_MODEL = 1024
BRANCH_WIDTH = 512
POOL_WINDOWS = (2, 4, 8, 16)
POOL_GC = 128
POOL_HALO = 16
DN_HEADS = 4
DN_HEAD_DIM = 128
DN_CONV = 4
DN_CHUNK = 64
MOBA_HEADS = 8
MOBA_HEAD_DIM = 64
MOBA_BLOCK = 256
MOBA_TOPK = 3
MOBA_ONES_ROWS = 16
MOBA_QTILE = 128
REL_BUCKETS = 32
REL_MAX_DIST = 128
FFN_DIM = 2816
FFN_CONV = 3
PLE_DIM = 256
NORM_EPS = 1e-6
NEG_INF = -1e30
LANES = 128
SUBLANES = 8

COL_POOL = 0
COL_DNQ = 512
COL_DNK = 1024
COL_DNV = 1536
COL_DNZ = 2048
COL_MV = 2560
COL_MQ = 3072
COL_MK = 3584
COL_GATE = 4096
Z_COLS = 7168

VMEM_LIMIT = 56 * 1024 * 1024


def _cparams(sem):
    return pltpu.CompilerParams(dimension_semantics=sem, vmem_limit_bytes=VMEM_LIMIT)


def _layer_spec(block, layer, index=None, **kwargs):
    def index_map(*grid):
        rest = index(*grid) if index is not None else (0,) * len(block)
        return (layer,) + tuple(rest)
    return pl.BlockSpec((None,) + tuple(block), index_map, **kwargs)


def _dot(a, b, precision=None):
    return lax.dot_general(a, b, (((1,), (0,)), ((), ())), precision=precision,
                           preferred_element_type=F32)


def _dot_nt(a, b, precision=None):
    return lax.dot_general(a, b, (((1,), (1,)), ((), ())), precision=precision,
                           preferred_element_type=F32)


def _dot_tn(a, b, precision=None):
    return lax.dot_general(a, b, (((0,), (0,)), ((), ())), precision=precision,
                           preferred_element_type=F32)


def _split3(x):
    hi = x.astype(BF16)
    rest = x - hi.astype(F32)
    mid = rest.astype(BF16)
    lo = (rest - mid.astype(F32)).astype(BF16)
    return hi, mid, lo


def _mm_dims(a, b, dims, passes):
    dot = lambda x, y: lax.dot_general(x, y, dims, preferred_element_type=F32)
    a_hi = a.astype(BF16)
    b_hi = b.astype(BF16)
    if passes == 1:
        return dot(a_hi, b_hi)
    a_lo = (a - a_hi.astype(F32)).astype(BF16)
    b_lo = (b - b_hi.astype(F32)).astype(BF16)
    return dot(a_hi, b_hi) + (dot(a_hi, b_lo) + dot(a_lo, b_hi))


def _mm(a, b, passes):
    return _mm_dims(a, b, (((1,), (0,)), ((), ())), passes)


def _mm_nt(a, b, passes):
    return _mm_dims(a, b, (((1,), (1,)), ((), ())), passes)


def _mm_tn(a, b, passes):
    return _mm_dims(a, b, (((0,), (0,)), ((), ())), passes)


DN_PASSES_QK = 1
DN_PASSES_INV = 1
DN_PASSES_STATE = 1
DN_INV_BASE = 8
DN_GROUP = 4


def _rms(x, w):
    return x * lax.rsqrt(jnp.mean(x * x, axis=-1, keepdims=True) + NORM_EPS) * w


def _in_proj_kernel(x_ref, nw_ref, w_ref, ws_ref, z_ref, zs_ref, zqk_ref, *, tn):
    hb = _rms(x_ref[...], nw_ref[...]).astype(BF16)
    zs_ref[...] = _dot(hb, ws_ref[...])
    for jb in range(Z_COLS // tn):
        cols = slice(jb * tn, (jb + 1) * tn)
        z = _dot(hb, w_ref[:, cols])
        z_ref[:, cols] = z.astype(ACT)
        if jb == COL_MQ // tn:
            zqk_ref[...] = z


def _in_proj(x, nw, w_main, w_small, layer, tm=512, tn=1024):
    t = x.shape[0]
    assert COL_MQ % tn == 0 and COL_MK == COL_MQ + tn // 2
    resident = dict(pipeline_mode=pl.Buffered(1))
    return pl.pallas_call(
        functools.partial(_in_proj_kernel, tn=tn),
        grid=(t // tm,),
        in_specs=[
            pl.BlockSpec((tm, D_MODEL), lambda i: (i, 0)),
            _layer_spec((1, D_MODEL), layer),
            _layer_spec((D_MODEL, Z_COLS), layer, **resident),
            _layer_spec((D_MODEL, LANES), layer, **resident),
        ],
        out_specs=[
            pl.BlockSpec((tm, Z_COLS), lambda i: (i, 0)),
            pl.BlockSpec((tm, LANES), lambda i: (i, 0)),
            pl.BlockSpec((tm, tn), lambda i: (i, 0)),
        ],
        out_shape=[jax.ShapeDtypeStruct((t, Z_COLS), ACT),
                   jax.ShapeDtypeStruct((t, LANES), F32),
                   jax.ShapeDtypeStruct((t, tn), F32)],
        compiler_params=_cparams(("parallel",)),
        name="in_proj",
    )(x, nw, w_main, w_small)


def _pool_mixer(u_ref, up_ref, w_ref, sc_ref, j, tp):
    halo = jnp.where(j == 0, 0.0, up_ref[...].astype(F32))
    t = j * tp + lax.broadcasted_iota(jnp.int32, (tp, POOL_GC), 0)
    out = []
    for g, win in enumerate(POOL_WINDOWS):
        cols = slice(g * POOL_GC, (g + 1) * POOL_GC)
        u = u_ref[:, cols].astype(F32)
        tot = jnp.concatenate([halo[:, cols], u], axis=0)
        width = 1
        while width < win:
            tot = tot + pltpu.roll(tot, width, 0)
            width *= 2
        cnt = jnp.minimum(t + 1, win).astype(F32)
        mixed = tot[POOL_HALO:, :] / cnt - u
        y = _dot(mixed.astype(BF16), w_ref[g])
        out.append((y * sc_ref[:, cols]).astype(BF16))
    return jnp.concatenate(out, axis=1)


def _dn_kernel(q_ref, k_ref, v_ref, zg_ref, zs_ref, cwq_ref, cwk_ref, cwv_ref, hp_ref, nw_ref, tri_ref,
               y_ref, halo_ref, state_ref, qs_ref, ks_ref, vs_ref, kb_ref, kg_ref, qd_ref,
               kd_ref, gs_ref, os_ref, *, tc):
    c = pl.program_id(1)

    @pl.when(c == 0)
    def _():
        halo_ref[...] = jnp.zeros_like(halo_ref)
        state_ref[...] = jnp.zeros_like(state_ref)

    for part, (src, cw, dst) in enumerate(((q_ref, cwq_ref, qs_ref), (k_ref, cwk_ref, ks_ref),
                                          (v_ref, cwv_ref, vs_ref))):
        x = src[...].astype(F32)
        ext = jnp.concatenate([halo_ref[part], x], axis=0)
        halo_ref[part] = x[tc - SUBLANES:tc, :]
        acc = cw[0:1, :] * ext
        for tap in range(1, DN_CONV):
            acc = cw[tap:tap + 1, :] * ext + pltpu.roll(acc, 1, 0)
        conv = acc[SUBLANES:, :]
        dst[...] = conv * jax.nn.sigmoid(conv)

    n_chunks = tc // DN_CHUNK
    zs = zs_ref[...]
    beta_c = jax.nn.sigmoid(zs)
    g_c = -jnp.exp(hp_ref[0:1, :]) * jnp.logaddexp(zs + hp_ref[1:2, :], 0.0)
    gc_c = sum(_dot(tri_ref[...], part) for part in _split3(g_c))
    g_last_c = jnp.concatenate(
        [jnp.broadcast_to(gc_c[(n + 1) * DN_CHUNK - 1:(n + 1) * DN_CHUNK, :], (DN_CHUNK, LANES))
         for n in range(n_chunks)], axis=0)
    egc_c = jnp.exp(gc_c)
    kdf_c = jnp.exp(g_last_c - gc_c)

    for h in range(DN_HEADS):
        cols = slice(h * DN_HEAD_DIM, (h + 1) * DN_HEAD_DIM)
        lane_bcast = lambda x, lane: jnp.broadcast_to(x[:, lane:lane + 1], (tc, LANES))
        beta = lane_bcast(beta_c, h)
        egc = lane_bcast(egc_c, DN_HEADS + h)
        q = qs_ref[:, cols]
        k = ks_ref[:, cols]
        q = q * lax.rsqrt(jnp.sum(q * q, axis=-1, keepdims=True) + NORM_EPS) * (DN_HEAD_DIM ** -0.5)
        k = k * lax.rsqrt(jnp.sum(k * k, axis=-1, keepdims=True) + NORM_EPS)
        kb = k * beta
        qs_ref[:, cols] = q
        ks_ref[:, cols] = k
        kb_ref[:, cols] = kb
        vs_ref[:, cols] = vs_ref[:, cols] * beta
        kg_ref[:, cols] = kb * egc
        qd_ref[:, cols] = q * egc
        kd_ref[:, cols] = k * lane_bcast(kdf_c, DN_HEADS + h)
        gs_ref[h] = lane_bcast(gc_c, DN_HEADS + h)

    hc = DN_HEADS * DN_CHUNK
    ri = lax.broadcasted_iota(jnp.int32, (hc, hc), 0)
    ci = lax.broadcasted_iota(jnp.int32, (hc, hc), 1)
    same_head = (ri // DN_CHUNK) == (ci // DN_CHUNK)
    incl = same_head & (ri >= ci)
    level_masks = [((ri // DN_INV_BASE) == (ci // DN_INV_BASE)) & (ri > ci)]
    size = DN_INV_BASE
    while size < DN_CHUNK:
        level_masks.append(((ri // (2 * size)) == (ci // (2 * size)))
                           & ((ri % (2 * size)) >= size) & ((ci % (2 * size)) < size))
        size *= 2
    eye_f = (ri == ci).astype(F32)

    def stack(ref, rows):
        return jnp.concatenate([ref[rows, h * DN_HEAD_DIM:(h + 1) * DN_HEAD_DIM]
                                for h in range(DN_HEADS)], axis=0)

    heads = range(DN_HEADS)
    last = lambda x, h: x[(h + 1) * DN_CHUNK - 1:(h + 1) * DN_CHUNK, :]
    state = [state_ref[h] for h in heads]
    for first in range(0, n_chunks, DN_GROUP):
        grp = range(first, min(first + DN_GROUP, n_chunks))
        rows = [slice(n * DN_CHUNK, (n + 1) * DN_CHUNK) for n in grp]
        q = [stack(qs_ref, r) for r in rows]
        k = [stack(ks_ref, r) for r in rows]
        kb = [stack(kb_ref, r) for r in rows]
        vb = [stack(vs_ref, r) for r in rows]
        kg = [stack(kg_ref, r) for r in rows]
        q_dec = [stack(qd_ref, r) for r in rows]
        k_dec = [stack(kd_ref, r) for r in rows]
        gc = [jnp.concatenate([gs_ref[h, r, :] for h in heads], axis=0) for r in rows]
        gc_t = [x.T for x in gc]
        decay = [jnp.exp(jnp.where(incl, jnp.concatenate([x, x], axis=1)
                                   - jnp.concatenate([xt, xt], axis=0), NEG_INF))
                 for x, xt in zip(gc, gc_t)]
        a_kk = [_mm_nt(x, y, DN_PASSES_QK) * d for x, y, d in zip(kb, k, decay)]
        m = [-jnp.where(level_masks[0], a, 0.0) for a in a_kk]
        t_mat = [eye_f + x for x in m]
        for _ in range(2):
            m = [_mm(x, x, DN_PASSES_INV) for x in m]
            t_mat = [t + _mm(t, x, DN_PASSES_INV) for t, x in zip(t_mat, m)]
        for mask in level_masks[1:]:
            low = [_mm(t, jnp.where(mask, a, 0.0), DN_PASSES_INV) for t, a in zip(t_mat, a_kk)]
            t_mat = [t - _mm(x, t, DN_PASSES_INV) for t, x in zip(t_mat, low)]
        uw = [_mm(t, jnp.concatenate([x, y], axis=1), DN_PASSES_INV)
              for t, x, y in zip(t_mat, vb, kg)]
        a_qk = [_mm_nt(x, y, DN_PASSES_QK) * d for x, y, d in zip(q, k, decay)]
        for i, r in enumerate(rows):
            u = uw[i][:, :DN_HEAD_DIM]
            w = uw[i][:, DN_HEAD_DIM:]
            ws, qs_ = [], []
            for h in heads:
                hr = slice(h * DN_CHUNK, (h + 1) * DN_CHUNK)
                both = _mm(jnp.concatenate([w[hr], q_dec[i][hr]], axis=0), state[h], DN_PASSES_STATE)
                ws.append(both[:DN_CHUNK])
                qs_.append(both[DN_CHUNK:])
            v_new = u - jnp.concatenate(ws, axis=0)
            o = jnp.concatenate(qs_, axis=0) + _mm(a_qk[i], v_new, DN_PASSES_STATE)
            for h in heads:
                hr = slice(h * DN_CHUNK, (h + 1) * DN_CHUNK)
                os_ref[r, h * DN_HEAD_DIM:(h + 1) * DN_HEAD_DIM] = o[hr]
                state[h] = (state[h] * jnp.exp(last(gc[i], h))
                            + _mm_tn(k_dec[i][hr], v_new[hr], DN_PASSES_STATE))
    for h in heads:
        state_ref[h] = state[h]

    for h in range(DN_HEADS):
        cols = slice(h * DN_HEAD_DIM, (h + 1) * DN_HEAD_DIM)
        zg = zg_ref[:, cols].astype(F32)
        y_ref[:, cols] = (_rms(os_ref[:, cols], nw_ref[...]) * (zg * jax.nn.sigmoid(zg))).astype(ACT)


def _deltanet(z, zs, conv_w, head_params, norm_w, layer, bsz, seq, tc=512):
    nt = seq // tc
    wide = DN_HEADS * DN_HEAD_DIM

    def col(cb):
        return pl.BlockSpec((tc, wide), lambda b, c: (b * nt + c, cb))

    def cw(cb):
        return _layer_spec((DN_CONV, wide), layer, lambda b, c: (0, cb))

    pos = jnp.arange(tc, dtype=jnp.int32)
    tri = ((pos[:, None] // DN_CHUNK == pos[None, :] // DN_CHUNK) & (pos[:, None] >= pos[None, :])).astype(BF16)
    tile = pltpu.VMEM((tc, wide), F32)
    return pl.pallas_call(
        functools.partial(_dn_kernel, tc=tc),
        grid=(bsz, nt),
        in_specs=[
            col(COL_DNQ // wide), col(COL_DNK // wide), col(COL_DNV // wide), col(COL_DNZ // wide),
            pl.BlockSpec((tc, LANES), lambda b, c: (b * nt + c, 0)),
            cw(0), cw(1), cw(2),
            _layer_spec((SUBLANES, LANES), layer),
            _layer_spec((1, DN_HEAD_DIM), layer),
            pl.BlockSpec((tc, tc), lambda b, c: (0, 0)),
        ],
        out_specs=pl.BlockSpec((tc, wide), lambda b, c: (b * nt + c, 0)),
        out_shape=jax.ShapeDtypeStruct((bsz * seq, wide), ACT),
        scratch_shapes=[
            pltpu.VMEM((3, SUBLANES, wide), F32),
            pltpu.VMEM((DN_HEADS, DN_HEAD_DIM, DN_HEAD_DIM), F32),
            tile, tile, tile, tile, tile, tile, tile,
            pltpu.VMEM((DN_HEADS, tc, LANES), F32),
            tile,
        ],
        compiler_params=_cparams(("parallel", "arbitrary")),
        name="deltanet",
    )(z, z, z, z, zs, conv_w, conv_w, conv_w, head_params, norm_w, tri)


def _moba_kernel(q_ref, k_ref, v_ref, d_ref, o_ref, kb_ref, vt_ref, bown_ref, bprev_ref, *, nb):
    blk = MOBA_BLOCK
    half = LANES // 2

    kmeans = []
    for j in range(nb):
        kj = k_ref[j * blk:(j + 1) * blk, :]
        kmeans.append(jnp.mean(kj.astype(F32), axis=0, keepdims=True))
        kb_ref[j * blk:(j + 1) * blk, :] = kj.astype(BF16)
        vt_ref[j, 0:LANES, :] = v_ref[j * blk:(j + 1) * blk, :].astype(F32).T.astype(BF16)
        vt_ref[j, LANES:LANES + MOBA_ONES_ROWS, :] = jnp.ones((MOBA_ONES_ROWS, blk), BF16)
    kmean = jnp.concatenate(kmeans, axis=0)

    r = lax.broadcasted_iota(jnp.int32, (LANES, LANES), 0)
    c = lax.broadcasted_iota(jnp.int32, (LANES, LANES), 1)
    for hh in range(2):
        far = jnp.broadcast_to(d_ref[hh, 1:2, :], (LANES, LANES))
        band = pltpu.roll(jnp.broadcast_to(d_ref[hh, 0:1, :], (LANES, LANES)), 0, 1,
                          stride=1, stride_axis=0)
        diag = jnp.where(c >= r, band, NEG_INF)
        off = jnp.where(c < r, band, far)
        lo = hh * blk
        bown_ref[0:LANES, lo:lo + LANES] = diag
        bown_ref[0:LANES, lo + LANES:lo + blk] = off
        bown_ref[LANES:blk, lo:lo + LANES] = jnp.full((LANES, LANES), NEG_INF, F32)
        bown_ref[LANES:blk, lo + LANES:lo + blk] = diag
        bprev_ref[0:LANES, lo:lo + LANES] = far
        bprev_ref[0:LANES, lo + LANES:lo + blk] = far
        bprev_ref[LANES:blk, lo:lo + LANES] = off
        bprev_ref[LANES:blk, lo + LANES:lo + blk] = far
    far_row = jnp.concatenate([jnp.broadcast_to(d_ref[0, 1:2, :], (1, LANES))] * 2
                              + [jnp.broadcast_to(d_ref[1, 1:2, :], (1, LANES))] * 2, axis=1)

    qt = MOBA_QTILE
    tiles_per_blk = blk // qt
    lane = lax.broadcasted_iota(jnp.int32, (qt, LANES), 1)
    blk_id = lax.broadcasted_iota(jnp.int32, (nb, 2 * qt), 0)
    row = lax.broadcasted_iota(jnp.int32, (LANES, qt), 0)
    far_tile = jnp.concatenate([far_row[:, :qt], far_row[:, blk:blk + qt]], axis=1)

    def tile_cols(ref, part, keys):
        return jnp.concatenate([ref[0:keys, part * qt:(part + 1) * qt],
                                ref[0:keys, blk + part * qt:blk + (part + 1) * qt]], axis=1)

    def score_stage(tile):
        i, part = divmod(tile, tiles_per_blk)
        rows = slice(tile * qt, (tile + 1) * qt)
        qf = q_ref[rows, :].astype(F32) * (MOBA_HEAD_DIM ** -0.5)
        q2 = jnp.concatenate([jnp.where(lane < half, qf, 0.0), jnp.where(lane >= half, qf, 0.0)],
                             axis=0)
        q2b = q2.astype(BF16)

        sel = None
        if i > MOBA_TOPK:
            gate = jnp.where(blk_id < i, _dot_nt(kmean, q2, HIGHEST), NEG_INF)
            sel = []
            for j in range(i):
                gj = gate[j:j + 1, :]
                ahead = (gate > gj) | ((gate == gj) & (blk_id < j))
                rank = jnp.sum(ahead.astype(F32), axis=0, keepdims=True)
                sel.append(rank < MOBA_TOPK)

        scores = []
        for j in range(i + 1):
            if j == i:
                keys = (part + 1) * qt
                s = _dot_nt(kb_ref[j * blk:j * blk + keys, :], q2b) + tile_cols(bown_ref, part, keys)
            else:
                s = _dot_nt(kb_ref[j * blk:(j + 1) * blk, :], q2b)
                if j == i - 1:
                    s = s + tile_cols(bprev_ref, part, blk)
                    if sel is not None:
                        s = s + jnp.where(sel[j], 0.0, NEG_INF)
                elif sel is not None:
                    s = s + jnp.where(sel[j], far_tile, NEG_INF)
                else:
                    s = s + far_tile
            scores.append(s)
        return scores

    def exp_stage(scores):
        m = functools.reduce(jnp.maximum, [jnp.max(s, axis=0, keepdims=True) for s in scores])
        return ([jnp.exp(s - m).astype(BF16) for s in scores],)

    def pv_stage(tile, probs):
        acc = None
        for j, p in enumerate(probs):
            pv = _dot(vt_ref[j, :, 0:p.shape[0]], p)
            acc = pv if acc is None else acc + pv
        out_t = acc[:LANES, :] / acc[LANES:LANES + 1, :]
        out_t = jnp.where(row < half, out_t[:, :qt], out_t[:, qt:])
        o_ref[tile * qt:(tile + 1) * qt, :] = out_t.T.astype(ACT)

    sc_q, ex_q = None, None
    for tile in list(reversed(range(nb * tiles_per_blk))) + [None, None]:
        new_sc = (tile, score_stage(tile)) if tile is not None else None
        new_ex = (sc_q[0],) + exp_stage(sc_q[1]) if sc_q is not None else None
        if ex_q is not None:
            pv_stage(*ex_q)
        sc_q, ex_q = new_sc, new_ex


def _moba(z, zqk, dist_bias, bsz, seq):
    nb = seq // MOBA_BLOCK
    pairs = MOBA_HEADS // 2
    kcb = (COL_MK - COL_MQ) // LANES
    vcb = COL_MV // LANES
    return pl.pallas_call(
        functools.partial(_moba_kernel, nb=nb),
        grid=(bsz, pairs),
        in_specs=[
            pl.BlockSpec((seq, LANES), lambda b, p: (b, p)),
            pl.BlockSpec((seq, LANES), lambda b, p: (b, kcb + p)),
            pl.BlockSpec((seq, LANES), lambda b, p: (b, vcb + p)),
            pl.BlockSpec((2, SUBLANES, LANES), lambda b, p: (p, 0, 0)),
        ],
        out_specs=pl.BlockSpec((seq, LANES), lambda b, p: (b, p)),
        out_shape=jax.ShapeDtypeStruct((bsz * seq, MOBA_HEADS * MOBA_HEAD_DIM), ACT),
        scratch_shapes=[
            pltpu.VMEM((seq, LANES), BF16),
            pltpu.VMEM((nb, LANES + MOBA_ONES_ROWS, MOBA_BLOCK), BF16),
            pltpu.VMEM((MOBA_BLOCK, 2 * MOBA_BLOCK), F32),
            pltpu.VMEM((MOBA_BLOCK, 2 * MOBA_BLOCK), F32),
        ],
        compiler_params=_cparams(("parallel", "parallel")),
        name="moba",
    )(zqk, zqk, z, dist_bias)


def _t5_bucket(rel):
    n = jnp.maximum(-rel, 0)
    exact = REL_BUCKETS // 2
    nf = jnp.maximum(n, 1).astype(F32)
    large = exact + (jnp.log(nf / exact) / math.log(REL_MAX_DIST / exact)
                     * (REL_BUCKETS - exact)).astype(jnp.int32)
    large = jnp.minimum(large, REL_BUCKETS - 1)
    return jnp.where(n < exact, n, large)


def _moba_dist_bias(rel_bias):
    bias_ht = rel_bias.T.astype(F32)
    near = bias_ht[:, _t5_bucket(-jnp.arange(LANES, dtype=jnp.int32))]
    far = bias_ht[:, _t5_bucket(jnp.full((LANES,), -REL_MAX_DIST, jnp.int32))]
    rows = jnp.zeros((MOBA_HEADS, SUBLANES, LANES), F32)
    return rows.at[:, 0, :].set(near).at[:, 1, :].set(far)


def _merge_kernel(u_ref, up_ref, pw_ref, ps_ref, yb_ref, yc_ref, ga_ref, gb_ref, gc_ref, x_ref, wb_ref,
                  wo_ref, o_ref, *, tm):
    y_a = _pool_mixer(u_ref, up_ref, pw_ref, ps_ref, pl.program_id(1), tm)
    merged = None
    for n, (y, g_ref) in enumerate(((y_a, ga_ref), (yb_ref[...], gb_ref), (yc_ref[...], gc_ref))):
        proj = _dot(y, wb_ref[n])
        gate = jax.nn.sigmoid(g_ref[...].astype(F32))
        term = gate * proj
        merged = term if merged is None else merged + term
    o_ref[...] = x_ref[...] + _dot(merged.astype(BF16), wo_ref[...])


def _merge(yb, yc, z, x, pool_w, pool_scale, w_branch, w_out, layer, bsz, seq, tm=512):
    nt = seq // tm
    hb = tm // POOL_HALO
    gcb = COL_GATE // D_MODEL
    row = lambda b, j: (b * nt + j, 0)
    return pl.pallas_call(
        functools.partial(_merge_kernel, tm=tm),
        grid=(bsz, nt),
        in_specs=[
            pl.BlockSpec((tm, BRANCH_WIDTH), row),
            pl.BlockSpec((POOL_HALO, BRANCH_WIDTH),
                         lambda b, j: (jnp.maximum((b * nt + j) * hb - 1, 0), 0)),
            _layer_spec((4, POOL_GC, POOL_GC), layer),
            _layer_spec((1, BRANCH_WIDTH), layer),
            pl.BlockSpec((tm, BRANCH_WIDTH), row),
            pl.BlockSpec((tm, BRANCH_WIDTH), row),
            pl.BlockSpec((tm, D_MODEL), lambda b, j: (b * nt + j, gcb)),
            pl.BlockSpec((tm, D_MODEL), lambda b, j: (b * nt + j, gcb + 1)),
            pl.BlockSpec((tm, D_MODEL), lambda b, j: (b * nt + j, gcb + 2)),
            pl.BlockSpec((tm, D_MODEL), row),
            _layer_spec((3, BRANCH_WIDTH, D_MODEL), layer),
            _layer_spec((D_MODEL, D_MODEL), layer),
        ],
        out_specs=pl.BlockSpec((tm, D_MODEL), row),
        out_shape=jax.ShapeDtypeStruct((bsz * seq, D_MODEL), F32),
        compiler_params=_cparams(("parallel", "parallel")),
        name="merge",
    )(z, z, pool_w, pool_scale, yb, yc, z, z, z, x, w_branch, w_out)


def _ffn_ple_kernel(x_ref, nw_ref, wu_ref, cw_ref, wd_ref, p_ref, pnw_ref, wg_ref, wp_ref, fw_ref, o_ref,
                    act_ref, carry_ref, pad_ref, *, tm, fc, final):
    j = pl.program_id(1)
    x = x_ref[...]
    h = _rms(x, nw_ref[...]).astype(BF16)
    for f in range(FFN_DIM // fc):
        conv = []
        for part in range(2):
            cols = slice(part * FFN_DIM + f * fc, part * FFN_DIM + (f + 1) * fc)
            buf = 2 * (f % 2) + part
            up = _dot(h, wu_ref[:, cols])
            pad_ref[buf, 0:SUBLANES, :] = jnp.where(j == 0, 0.0, carry_ref[:, cols])
            pad_ref[buf, SUBLANES:SUBLANES + tm, :] = up
            carry_ref[:, cols] = up[tm - SUBLANES:tm, :]
            y = cw_ref[FFN_CONV - 1:FFN_CONV, cols] * up
            for tap in range(FFN_CONV - 1):
                off = SUBLANES - (FFN_CONV - 1) + tap
                y = y + cw_ref[tap:tap + 1, cols] * pad_ref[buf, off:off + tm, :]
            conv.append(y)
        act_ref[:, f * fc:(f + 1) * fc] = (jax.nn.gelu(conv[0], approximate=True) * conv[1]).astype(BF16)
    x = x + _dot(act_ref[...], wd_ref[...])
    gate = jax.nn.sigmoid(_dot(_rms(x, pnw_ref[...]).astype(BF16), wg_ref[...]))
    x = x + _dot(p_ref[...].astype(BF16), wp_ref[...]) * gate
    if final:
        x = _rms(x, fw_ref[...])
    o_ref[...] = x


def _ffn_ple(x, nw, w_up, w_conv, w_down, p, ple_nw, w_gate, w_proj, final_w, layer, final, bsz, seq,
             tm=512, fc=256):
    nt = seq // tm
    resident = dict(pipeline_mode=pl.Buffered(1))
    row = lambda b, j: (b * nt + j, 0)
    return pl.pallas_call(
        functools.partial(_ffn_ple_kernel, tm=tm, fc=fc, final=final),
        grid=(bsz, nt),
        in_specs=[
            pl.BlockSpec((tm, D_MODEL), row),
            _layer_spec((1, D_MODEL), layer),
            _layer_spec((D_MODEL, 2 * FFN_DIM), layer, **resident),
            _layer_spec((FFN_CONV, 2 * FFN_DIM), layer),
            _layer_spec((FFN_DIM, D_MODEL), layer, **resident),
            _layer_spec((tm, PLE_DIM), layer, row),
            _layer_spec((1, D_MODEL), layer),
            _layer_spec((D_MODEL, D_MODEL), layer, **resident),
            _layer_spec((PLE_DIM, D_MODEL), layer, **resident),
            pl.BlockSpec((1, D_MODEL), lambda b, j: (0, 0)),
        ],
        out_specs=pl.BlockSpec((tm, D_MODEL), row),
        out_shape=jax.ShapeDtypeStruct((bsz * seq, D_MODEL), F32),
        scratch_shapes=[
            pltpu.VMEM((tm, FFN_DIM), BF16),
            pltpu.VMEM((SUBLANES, 2 * FFN_DIM), F32),
            pltpu.VMEM((4, SUBLANES + tm, fc), F32),
        ],
        compiler_params=_cparams(("parallel", "arbitrary")),
        name="ffn_ple",
    )(x, nw, w_up, w_conv, w_down, p, ple_nw, w_gate, w_proj, final_w)


def _split_w_in(w):
    pool = w[..., 0:512]
    dn_qkv = w[..., 512:2048]
    dn_z = w[..., 2048:2560]
    dn_ba = w[..., 2560:2568]
    mb_qk = w[..., 2568:3592]
    mb_v = w[..., 3592:4104]
    gates = w[..., 4104:7176]
    main = jnp.concatenate([pool, dn_qkv, dn_z, mb_v, mb_qk, gates], axis=-1).astype(BF16)
    small = jnp.pad(dn_ba, ((0, 0),) * (w.ndim - 1) + ((0, LANES - 2 * DN_HEADS),)).astype(BF16)
    return main, small


def kernel(x, p, rel_bias, norm_mix, w_in, pool_w, pool_scale, dn_conv, dn_a_log, dn_dt_bias,
           dn_norm, w_branch, w_out, norm_ffn, ffn_up, ffn_conv, ffn_down, norm_ple, ple_gate,
           ple_proj, norm_final):
    bsz, seq, d = x.shape
    depth = w_in.shape[0]
    t = bsz * seq
    xt = x.reshape(t, d)
    row = lambda a: a[:, None, :]
    w_main, w_small = _split_w_in(w_in)
    pool_w, w_branch, w_out = pool_w.astype(BF16), w_branch.astype(BF16), w_out.astype(BF16)
    ffn_up, ffn_down = ffn_up.astype(BF16), ffn_down.astype(BF16)
    ple_gate, ple_proj = ple_gate.astype(BF16), ple_proj.astype(BF16)
    p = p.reshape(depth, t, PLE_DIM)
    dist_bias = _moba_dist_bias(rel_bias)
    head_params = jnp.zeros((depth, SUBLANES, LANES), F32)
    head_params = head_params.at[:, 0, DN_HEADS:2 * DN_HEADS].set(dn_a_log)
    head_params = head_params.at[:, 1, DN_HEADS:2 * DN_HEADS].set(dn_dt_bias)
    for i in range(depth):
        z, zs, zqk = _in_proj(xt, row(norm_mix), w_main, w_small, i)
        y_b = _deltanet(z, zs, dn_conv, head_params, row(dn_norm), i, bsz, seq)
        y_c = _moba(z, zqk, dist_bias, bsz, seq)
        xt = _merge(y_b, y_c, z, xt, pool_w, row(pool_scale), w_branch, w_out, i, bsz, seq)
        xt = _ffn_ple(xt, row(norm_ffn), ffn_up, ffn_conv, ffn_down, p, row(norm_ple), ple_gate, ple_proj,
                      norm_final[None, :], i, i == depth - 1, bsz, seq)
    return xt.reshape(bsz, seq, d)
```

```python
import functools
import math

import jax
import jax.numpy as jnp
from jax import lax
from jax.experimental import pallas as pl
from jax.experimental.pallas import tpu as pltpu

F32 = jnp.float32
BF16 = jnp.bfloat16
HIGHEST = lax.Precision.HIGHEST
ACT = jnp.bfloat16

D_MODEL = 1024
BRANCH_WIDTH = 512
POOL_WINDOWS = (2, 4, 8, 16)
POOL_GC = 128
POOL_HALO = 16
DN_HEADS = 4
DN_HEAD_DIM = 128
DN_CONV = 4
DN_CHUNK = 64
MOBA_HEADS = 8
MOBA_HEAD_DIM = 64
MOBA_BLOCK = 256
MOBA_TOPK = 3
MOBA_ONES_ROWS = 16
MOBA_QTILE = 128
REL_BUCKETS = 32
REL_MAX_DIST = 128
FFN_DIM = 2816
FFN_CONV = 3
PLE_DIM = 256
NORM_EPS = 1e-6
NEG_INF = -1e30
LANES = 128
SUBLANES = 8

COL_POOL = 0
COL_DNQ = 512
COL_DNK = 1024
COL_DNV = 1536
COL_DNZ = 2048
COL_MV = 2560
COL_MQ = 3072
COL_MK = 3584
COL_GATE = 4096
Z_COLS = 7168

VMEM_LIMIT = 56 * 1024 * 1024


def _cparams(sem):
    return pltpu.CompilerParams(dimension_semantics=sem, vmem_limit_bytes=VMEM_LIMIT)


def _layer_spec(block, layer, index=None, **kwargs):
    def index_map(*grid):
        rest = index(*grid) if index is not None else (0,) * len(block)
        return (layer,) + tuple(rest)
    return pl.BlockSpec((None,) + tuple(block), index_map, **kwargs)


def _dot(a, b, precision=None):
    return lax.dot_general(a, b, (((1,), (0,)), ((), ())), precision=precision,
                           preferred_element_type=F32)


def _dot_nt(a, b, precision=None):
    return lax.dot_general(a, b, (((1,), (1,)), ((), ())), precision=precision,
                           preferred_element_type=F32)


def _dot_tn(a, b, precision=None):
    return lax.dot_general(a, b, (((0,), (0,)), ((), ())), precision=precision,
                           preferred_element_type=F32)


def _split3(x):
    hi = x.astype(BF16)
    rest = x - hi.astype(F32)
    mid = rest.astype(BF16)
    lo = (rest - mid.astype(F32)).astype(BF16)
    return hi, mid, lo


def _mm_dims(a, b, dims, passes):
    dot = lambda x, y: lax.dot_general(x, y, dims, preferred_element_type=F32)
    a_hi = a.astype(BF16)
    b_hi = b.astype(BF16)
    if passes == 1:
        return dot(a_hi, b_hi)
    a_lo = (a - a_hi.astype(F32)).astype(BF16)
    b_lo = (b - b_hi.astype(F32)).astype(BF16)
    return dot(a_hi, b_hi) + (dot(a_hi, b_lo) + dot(a_lo, b_hi))


def _mm(a, b, passes):
    return _mm_dims(a, b, (((1,), (0,)), ((), ())), passes)


def _mm_nt(a, b, passes):
    return _mm_dims(a, b, (((1,), (1,)), ((), ())), passes)


def _mm_tn(a, b, passes):
    return _mm_dims(a, b, (((0,), (0,)), ((), ())), passes)


DN_PASSES_QK = 1
DN_PASSES_INV = 1
DN_PASSES_STATE = 1
DN_INV_BASE = 8
DN_GROUP = 4


def _rms(x, w):
    return x * lax.rsqrt(jnp.mean(x * x, axis=-1, keepdims=True) + NORM_EPS) * w


def _in_proj_kernel(x_ref, nw_ref, w_ref, ws_ref, z_ref, zs_ref, zqk_ref, *, tn):
    hb = _rms(x_ref[...], nw_ref[...]).astype(BF16)
    zs_ref[...] = _dot(hb, ws_ref[...])
    for jb in range(Z_COLS // tn):
        cols = slice(jb * tn, (jb + 1) * tn)
        z = _dot(hb, w_ref[:, cols])
        z_ref[:, cols] = z.astype(ACT)
        if jb == COL_MQ // tn:
            zqk_ref[...] = z


def _in_proj(x, nw, w_main, w_small, layer, tm=512, tn=1024):
    t = x.shape[0]
    assert COL_MQ % tn == 0 and COL_MK == COL_MQ + tn // 2
    resident = dict(pipeline_mode=pl.Buffered(1))
    return pl.pallas_call(
        functools.partial(_in_proj_kernel, tn=tn),
        grid=(t // tm,),
        in_specs=[
            pl.BlockSpec((tm, D_MODEL), lambda i: (i, 0)),
            _layer_spec((1, D_MODEL), layer),
            _layer_spec((D_MODEL, Z_COLS), layer, **resident),
            _layer_spec((D_MODEL, LANES), layer, **resident),
        ],
        out_specs=[
            pl.BlockSpec((tm, Z_COLS), lambda i: (i, 0)),
            pl.BlockSpec((tm, LANES), lambda i: (i, 0)),
            pl.BlockSpec((tm, tn), lambda i: (i, 0)),
        ],
        out_shape=[jax.ShapeDtypeStruct((t, Z_COLS), ACT),
                   jax.ShapeDtypeStruct((t, LANES), F32),
                   jax.ShapeDtypeStruct((t, tn), F32)],
        compiler_params=_cparams(("parallel",)),
        name="in_proj",
    )(x, nw, w_main, w_small)


def _pool_mixer(u_ref, up_ref, w_ref, sc_ref, j, tp):
    halo = jnp.where(j == 0, 0.0, up_ref[...].astype(F32))
    t = j * tp + lax.broadcasted_iota(jnp.int32, (tp, POOL_GC), 0)
    out = []
    for g, win in enumerate(POOL_WINDOWS):
        cols = slice(g * POOL_GC, (g + 1) * POOL_GC)
        u = u_ref[:, cols].astype(F32)
        tot = jnp.concatenate([halo[:, cols], u], axis=0)
        width = 1
        while width < win:
            tot = tot + pltpu.roll(tot, width, 0)
            width *= 2
        cnt = jnp.minimum(t + 1, win).astype(F32)
        mixed = tot[POOL_HALO:, :] / cnt - u
        y = _dot(mixed.astype(BF16), w_ref[g])
        out.append((y * sc_ref[:, cols]).astype(BF16))
    return jnp.concatenate(out, axis=1)


def _dn_kernel(q_ref, k_ref, v_ref, zg_ref, zs_ref, cwq_ref, cwk_ref, cwv_ref, hp_ref, nw_ref, tri_ref,
               y_ref, halo_ref, state_ref, qs_ref, ks_ref, vs_ref, kb_ref, kg_ref, qd_ref,
               kd_ref, gs_ref, os_ref, *, tc):
    c = pl.program_id(1)

    @pl.when(c == 0)
    def _():
        halo_ref[...] = jnp.zeros_like(halo_ref)
        state_ref[...] = jnp.zeros_like(state_ref)

    for part, (src, cw, dst) in enumerate(((q_ref, cwq_ref, qs_ref), (k_ref, cwk_ref, ks_ref),
                                          (v_ref, cwv_ref, vs_ref))):
        x = src[...].astype(F32)
        ext = jnp.concatenate([halo_ref[part], x], axis=0)
        halo_ref[part] = x[tc - SUBLANES:tc, :]
        acc = cw[0:1, :] * ext
        for tap in range(1, DN_CONV):
            acc = cw[tap:tap + 1, :] * ext + pltpu.roll(acc, 1, 0)
        conv = acc[SUBLANES:, :]
        dst[...] = conv * jax.nn.sigmoid(conv)

    n_chunks = tc // DN_CHUNK
    zs = zs_ref[...]
    beta_c = jax.nn.sigmoid(zs)
    g_c = -jnp.exp(hp_ref[0:1, :]) * jnp.logaddexp(zs + hp_ref[1:2, :], 0.0)
    gc_c = sum(_dot(tri_ref[...], part) for part in _split3(g_c))
    g_last_c = jnp.concatenate(
        [jnp.broadcast_to(gc_c[(n + 1) * DN_CHUNK - 1:(n + 1) * DN_CHUNK, :], (DN_CHUNK, LANES))
         for n in range(n_chunks)], axis=0)
    egc_c = jnp.exp(gc_c)
    kdf_c = jnp.exp(g_last_c - gc_c)

    for h in range(DN_HEADS):
        cols = slice(h * DN_HEAD_DIM, (h + 1) * DN_HEAD_DIM)
        lane_bcast = lambda x, lane: jnp.broadcast_to(x[:, lane:lane + 1], (tc, LANES))
        beta = lane_bcast(beta_c, h)
        egc = lane_bcast(egc_c, DN_HEADS + h)
        q = qs_ref[:, cols]
        k = ks_ref[:, cols]
        q = q * lax.rsqrt(jnp.sum(q * q, axis=-1, keepdims=True) + NORM_EPS) * (DN_HEAD_DIM ** -0.5)
        k = k * lax.rsqrt(jnp.sum(k * k, axis=-1, keepdims=True) + NORM_EPS)
        kb = k * beta
        qs_ref[:, cols] = q
        ks_ref[:, cols] = k
        kb_ref[:, cols] = kb
        vs_ref[:, cols] = vs_ref[:, cols] * beta
        kg_ref[:, cols] = kb * egc
        qd_ref[:, cols] = q * egc
        kd_ref[:, cols] = k * lane_bcast(kdf_c, DN_HEADS + h)
        gs_ref[h] = lane_bcast(gc_c, DN_HEADS + h)

    hc = DN_HEADS * DN_CHUNK
    ri = lax.broadcasted_iota(jnp.int32, (hc, hc), 0)
    ci = lax.broadcasted_iota(jnp.int32, (hc, hc), 1)
    same_head = (ri // DN_CHUNK) == (ci // DN_CHUNK)
    incl = same_head & (ri >= ci)
    level_masks = [((ri // DN_INV_BASE) == (ci // DN_INV_BASE)) & (ri > ci)]
    size = DN_INV_BASE
    while size < DN_CHUNK:
        level_masks.append(((ri // (2 * size)) == (ci // (2 * size)))
                           & ((ri % (2 * size)) >= size) & ((ci % (2 * size)) < size))
        size *= 2
    eye_f = (ri == ci).astype(F32)

    def stack(ref, rows):
        return jnp.concatenate([ref[rows, h * DN_HEAD_DIM:(h + 1) * DN_HEAD_DIM]
                                for h in range(DN_HEADS)], axis=0)

    heads = range(DN_HEADS)
    last = lambda x, h: x[(h + 1) * DN_CHUNK - 1:(h + 1) * DN_CHUNK, :]
    state = [state_ref[h] for h in heads]
    for first in range(0, n_chunks, DN_GROUP):
        grp = range(first, min(first + DN_GROUP, n_chunks))
        rows = [slice(n * DN_CHUNK, (n + 1) * DN_CHUNK) for n in grp]
        q = [stack(qs_ref, r) for r in rows]
        k = [stack(ks_ref, r) for r in rows]
        kb = [stack(kb_ref, r) for r in rows]
        vb = [stack(vs_ref, r) for r in rows]
        kg = [stack(kg_ref, r) for r in rows]
        q_dec = [stack(qd_ref, r) for r in rows]
        k_dec = [stack(kd_ref, r) for r in rows]
        gc = [jnp.concatenate([gs_ref[h, r, :] for h in heads], axis=0) for r in rows]
        gc_t = [x.T for x in gc]
        decay = [jnp.exp(jnp.where(incl, jnp.concatenate([x, x], axis=1)
                                   - jnp.concatenate([xt, xt], axis=0), NEG_INF))
                 for x, xt in zip(gc, gc_t)]
        a_kk = [_mm_nt(x, y, DN_PASSES_QK) * d for x, y, d in zip(kb, k, decay)]
        m = [-jnp.where(level_masks[0], a, 0.0) for a in a_kk]
        t_mat = [eye_f + x for x in m]
        for _ in range(2):
            m = [_mm(x, x, DN_PASSES_INV) for x in m]
            t_mat = [t + _mm(t, x, DN_PASSES_INV) for t, x in zip(t_mat, m)]
        for mask in level_masks[1:]:
            low = [_mm(t, jnp.where(mask, a, 0.0), DN_PASSES_INV) for t, a in zip(t_mat, a_kk)]
            t_mat = [t - _mm(x, t, DN_PASSES_INV) for t, x in zip(t_mat, low)]
        uw = [_mm(t, jnp.concatenate([x, y], axis=1), DN_PASSES_INV)
              for t, x, y in zip(t_mat, vb, kg)]
        a_qk = [_mm_nt(x, y, DN_PASSES_QK) * d for x, y, d in zip(q, k, decay)]
        for i, r in enumerate(rows):
            u = uw[i][:, :DN_HEAD_DIM]
            w = uw[i][:, DN_HEAD_DIM:]
            ws, qs_ = [], []
            for h in heads:
                hr = slice(h * DN_CHUNK, (h + 1) * DN_CHUNK)
                both = _mm(jnp.concatenate([w[hr], q_dec[i][hr]], axis=0), state[h], DN_PASSES_STATE)
                ws.append(both[:DN_CHUNK])
                qs_.append(both[DN_CHUNK:])
            v_new = u - jnp.concatenate(ws, axis=0)
            o = jnp.concatenate(qs_, axis=0) + _mm(a_qk[i], v_new, DN_PASSES_STATE)
            for h in heads:
                hr = slice(h * DN_CHUNK, (h + 1) * DN_CHUNK)
                os_ref[r, h * DN_HEAD_DIM:(h + 1) * DN_HEAD_DIM] = o[hr]
                state[h] = (state[h] * jnp.exp(last(gc[i], h))
                            + _mm_tn(k_dec[i][hr], v_new[hr], DN_PASSES_STATE))
    for h in heads:
        state_ref[h] = state[h]

    for h in range(DN_HEADS):
        cols = slice(h * DN_HEAD_DIM, (h + 1) * DN_HEAD_DIM)
        zg = zg_ref[:, cols].astype(F32)
        y_ref[:, cols] = (_rms(os_ref[:, cols], nw_ref[...]) * (zg * jax.nn.sigmoid(zg))).astype(ACT)


def _deltanet(z, zs, conv_w, head_params, norm_w, layer, bsz, seq, tc=256):
    nt = seq // tc
    wide = DN_HEADS * DN_HEAD_DIM

    def col(cb):
        return pl.BlockSpec((tc, wide), lambda b, c: (b * nt + c, cb))

    def cw(cb):
        return _layer_spec((DN_CONV, wide), layer, lambda b, c: (0, cb))

    pos = jnp.arange(tc, dtype=jnp.int32)
    tri = ((pos[:, None] // DN_CHUNK == pos[None, :] // DN_CHUNK) & (pos[:, None] >= pos[None, :])).astype(BF16)
    tile = pltpu.VMEM((tc, wide), F32)
    return pl.pallas_call(
        functools.partial(_dn_kernel, tc=tc),
        grid=(bsz, nt),
        in_specs=[
            col(COL_DNQ // wide), col(COL_DNK // wide), col(COL_DNV // wide), col(COL_DNZ // wide),
            pl.BlockSpec((tc, LANES), lambda b, c: (b * nt + c, 0)),
            cw(0), cw(1), cw(2),
            _layer_spec((SUBLANES, LANES), layer),
            _layer_spec((1, DN_HEAD_DIM), layer),
            pl.BlockSpec((tc, tc), lambda b, c: (0, 0)),
        ],
        out_specs=pl.BlockSpec((tc, wide), lambda b, c: (b * nt + c, 0)),
        out_shape=jax.ShapeDtypeStruct((bsz * seq, wide), ACT),
        scratch_shapes=[
            pltpu.VMEM((3, SUBLANES, wide), F32),
            pltpu.VMEM((DN_HEADS, DN_HEAD_DIM, DN_HEAD_DIM), F32),
            tile, tile, tile, tile, tile, tile, tile,
            pltpu.VMEM((DN_HEADS, tc, LANES), F32),
            tile,
        ],
        compiler_params=_cparams(("parallel", "arbitrary")),
        name="deltanet",
    )(z, z, z, z, zs, conv_w, conv_w, conv_w, head_params, norm_w, tri)


def _moba_kernel(q_ref, k_ref, v_ref, d_ref, o_ref, kb_ref, vt_ref, bown_ref, bprev_ref, *, nb):
    blk = MOBA_BLOCK
    half = LANES // 2

    kmeans = []
    for j in range(nb):
        kj = k_ref[j * blk:(j + 1) * blk, :]
        kmeans.append(jnp.mean(kj.astype(F32), axis=0, keepdims=True))
        kb_ref[j * blk:(j + 1) * blk, :] = kj.astype(BF16)
        vt_ref[j, 0:LANES, :] = v_ref[j * blk:(j + 1) * blk, :].astype(F32).T.astype(BF16)
        vt_ref[j, LANES:LANES + MOBA_ONES_ROWS, :] = jnp.ones((MOBA_ONES_ROWS, blk), BF16)
    kmean = jnp.concatenate(kmeans, axis=0)

    r = lax.broadcasted_iota(jnp.int32, (LANES, LANES), 0)
    c = lax.broadcasted_iota(jnp.int32, (LANES, LANES), 1)
    for hh in range(2):
        far = jnp.broadcast_to(d_ref[hh, 1:2, :], (LANES, LANES))
        band = pltpu.roll(jnp.broadcast_to(d_ref[hh, 0:1, :], (LANES, LANES)), 0, 1,
                          stride=1, stride_axis=0)
        diag = jnp.where(c >= r, band, NEG_INF)
        off = jnp.where(c < r, band, far)
        lo = hh * blk
        bown_ref[0:LANES, lo:lo + LANES] = diag
        bown_ref[0:LANES, lo + LANES:lo + blk] = off
        bown_ref[LANES:blk, lo:lo + LANES] = jnp.full((LANES, LANES), NEG_INF, F32)
        bown_ref[LANES:blk, lo + LANES:lo + blk] = diag
        bprev_ref[0:LANES, lo:lo + LANES] = far
        bprev_ref[0:LANES, lo + LANES:lo + blk] = far
        bprev_ref[LANES:blk, lo:lo + LANES] = off
        bprev_ref[LANES:blk, lo + LANES:lo + blk] = far
    far_row = jnp.concatenate([jnp.broadcast_to(d_ref[0, 1:2, :], (1, LANES))] * 2
                              + [jnp.broadcast_to(d_ref[1, 1:2, :], (1, LANES))] * 2, axis=1)

    qt = MOBA_QTILE
    tiles_per_blk = blk // qt
    lane = lax.broadcasted_iota(jnp.int32, (qt, LANES), 1)
    blk_id = lax.broadcasted_iota(jnp.int32, (nb, 2 * qt), 0)
    row = lax.broadcasted_iota(jnp.int32, (LANES, qt), 0)
    far_tile = jnp.concatenate([far_row[:, :qt], far_row[:, blk:blk + qt]], axis=1)

    def tile_cols(ref, part, keys):
        return jnp.concatenate([ref[0:keys, part * qt:(part + 1) * qt],
                                ref[0:keys, blk + part * qt:blk + (part + 1) * qt]], axis=1)

    def score_stage(tile):
        i, part = divmod(tile, tiles_per_blk)
        rows = slice(tile * qt, (tile + 1) * qt)
        qf = q_ref[rows, :].astype(F32) * (MOBA_HEAD_DIM ** -0.5)
        q2 = jnp.concatenate([jnp.where(lane < half, qf, 0.0), jnp.where(lane >= half, qf, 0.0)],
                             axis=0)
        q2b = q2.astype(BF16)

        sel = None
        if i > MOBA_TOPK:
            gate = jnp.where(blk_id < i, _dot_nt(kmean, q2, HIGHEST), NEG_INF)
            sel = []
            for j in range(i):
                gj = gate[j:j + 1, :]
                ahead = (gate > gj) | ((gate == gj) & (blk_id < j))
                rank = jnp.sum(ahead.astype(F32), axis=0, keepdims=True)
                sel.append(rank < MOBA_TOPK)

        scores = []
        for j in range(i + 1):
            if j == i:
                keys = (part + 1) * qt
                s = _dot_nt(kb_ref[j * blk:j * blk + keys, :], q2b) + tile_cols(bown_ref, part, keys)
            else:
                s = _dot_nt(kb_ref[j * blk:(j + 1) * blk, :], q2b)
                if j == i - 1:
                    s = s + tile_cols(bprev_ref, part, blk)
                    if sel is not None:
                        s = s + jnp.where(sel[j], 0.0, NEG_INF)
                elif sel is not None:
                    s = s + jnp.where(sel[j], far_tile, NEG_INF)
                else:
                    s = s + far_tile
            scores.append(s)
        return scores

    def exp_stage(scores):
        m = functools.reduce(jnp.maximum, [jnp.max(s, axis=0, keepdims=True) for s in scores])
        return ([jnp.exp(s - m).astype(BF16) for s in scores],)

    def pv_stage(tile, probs):
        acc = None
        for j, p in enumerate(probs):
            pv = _dot(vt_ref[j, :, 0:p.shape[0]], p)
            acc = pv if acc is None else acc + pv
        out_t = acc[:LANES, :] / acc[LANES:LANES + 1, :]
        out_t = jnp.where(row < half, out_t[:, :qt], out_t[:, qt:])
        o_ref[tile * qt:(tile + 1) * qt, :] = out_t.T.astype(ACT)

    sc_q, ex_q = None, None
    for tile in list(reversed(range(nb * tiles_per_blk))) + [None, None]:
        new_sc = (tile, score_stage(tile)) if tile is not None else None
        new_ex = (sc_q[0],) + exp_stage(sc_q[1]) if sc_q is not None else None
        if ex_q is not None:
            pv_stage(*ex_q)
        sc_q, ex_q = new_sc, new_ex


def _moba(z, zqk, dist_bias, bsz, seq):
    nb = seq // MOBA_BLOCK
    pairs = MOBA_HEADS // 2
    kcb = (COL_MK - COL_MQ) // LANES
    vcb = COL_MV // LANES
    return pl.pallas_call(
        functools.partial(_moba_kernel, nb=nb),
        grid=(bsz, pairs),
        in_specs=[
            pl.BlockSpec((seq, LANES), lambda b, p: (b, p)),
            pl.BlockSpec((seq, LANES), lambda b, p: (b, kcb + p)),
            pl.BlockSpec((seq, LANES), lambda b, p: (b, vcb + p)),
            pl.BlockSpec((2, SUBLANES, LANES), lambda b, p: (p, 0, 0)),
        ],
        out_specs=pl.BlockSpec((seq, LANES), lambda b, p: (b, p)),
        out_shape=jax.ShapeDtypeStruct((bsz * seq, MOBA_HEADS * MOBA_HEAD_DIM), ACT),
        scratch_shapes=[
            pltpu.VMEM((seq, LANES), BF16),
            pltpu.VMEM((nb, LANES + MOBA_ONES_ROWS, MOBA_BLOCK), BF16),
            pltpu.VMEM((MOBA_BLOCK, 2 * MOBA_BLOCK), F32),
            pltpu.VMEM((MOBA_BLOCK, 2 * MOBA_BLOCK), F32),
        ],
        compiler_params=_cparams(("parallel", "parallel")),
        name="moba",
    )(zqk, zqk, z, dist_bias)


def _t5_bucket(rel):
    n = jnp.maximum(-rel, 0)
    exact = REL_BUCKETS // 2
    nf = jnp.maximum(n, 1).astype(F32)
    large = exact + (jnp.log(nf / exact) / math.log(REL_MAX_DIST / exact)
                     * (REL_BUCKETS - exact)).astype(jnp.int32)
    large = jnp.minimum(large, REL_BUCKETS - 1)
    return jnp.where(n < exact, n, large)


def _moba_dist_bias(rel_bias):
    bias_ht = rel_bias.T.astype(F32)
    near = bias_ht[:, _t5_bucket(-jnp.arange(LANES, dtype=jnp.int32))]
    far = bias_ht[:, _t5_bucket(jnp.full((LANES,), -REL_MAX_DIST, jnp.int32))]
    rows = jnp.zeros((MOBA_HEADS, SUBLANES, LANES), F32)
    return rows.at[:, 0, :].set(near).at[:, 1, :].set(far)


def _merge_kernel(u_ref, up_ref, pw_ref, ps_ref, yb_ref, yc_ref, ga_ref, gb_ref, gc_ref, x_ref, wb_ref,
                  wo_ref, o_ref, *, tm):
    y_a = _pool_mixer(u_ref, up_ref, pw_ref, ps_ref, pl.program_id(1), tm)
    merged = None
    for n, (y, g_ref) in enumerate(((y_a, ga_ref), (yb_ref[...], gb_ref), (yc_ref[...], gc_ref))):
        proj = _dot(y, wb_ref[n])
        gate = jax.nn.sigmoid(g_ref[...].astype(F32))
        term = gate * proj
        merged = term if merged is None else merged + term
    o_ref[...] = x_ref[...] + _dot(merged.astype(BF16), wo_ref[...])


def _merge(yb, yc, z, x, pool_w, pool_scale, w_branch, w_out, layer, bsz, seq, tm=512):
    nt = seq // tm
    hb = tm // POOL_HALO
    gcb = COL_GATE // D_MODEL
    row = lambda b, j: (b * nt + j, 0)
    return pl.pallas_call(
        functools.partial(_merge_kernel, tm=tm),
        grid=(bsz, nt),
        in_specs=[
            pl.BlockSpec((tm, BRANCH_WIDTH), row),
            pl.BlockSpec((POOL_HALO, BRANCH_WIDTH),
                         lambda b, j: (jnp.maximum((b * nt + j) * hb - 1, 0), 0)),
            _layer_spec((4, POOL_GC, POOL_GC), layer),
            _layer_spec((1, BRANCH_WIDTH), layer),
            pl.BlockSpec((tm, BRANCH_WIDTH), row),
            pl.BlockSpec((tm, BRANCH_WIDTH), row),
            pl.BlockSpec((tm, D_MODEL), lambda b, j: (b * nt + j, gcb)),
            pl.BlockSpec((tm, D_MODEL), lambda b, j: (b * nt + j, gcb + 1)),
            pl.BlockSpec((tm, D_MODEL), lambda b, j: (b * nt + j, gcb + 2)),
            pl.BlockSpec((tm, D_MODEL), row),
            _layer_spec((3, BRANCH_WIDTH, D_MODEL), layer),
            _layer_spec((D_MODEL, D_MODEL), layer),
        ],
        out_specs=pl.BlockSpec((tm, D_MODEL), row),
        out_shape=jax.ShapeDtypeStruct((bsz * seq, D_MODEL), F32),
        compiler_params=_cparams(("parallel", "parallel")),
        name="merge",
    )(z, z, pool_w, pool_scale, yb, yc, z, z, z, x, w_branch, w_out)


def _ffn_ple_kernel(x_ref, nw_ref, wu_ref, cw_ref, wd_ref, p_ref, pnw_ref, wg_ref, wp_ref, fw_ref, o_ref,
                    act_ref, carry_ref, pad_ref, *, tm, fc, final):
    j = pl.program_id(1)
    x = x_ref[...]
    h = _rms(x, nw_ref[...]).astype(BF16)
    for f in range(FFN_DIM // fc):
        conv = []
        for part in range(2):
            cols = slice(part * FFN_DIM + f * fc, part * FFN_DIM + (f + 1) * fc)
            buf = 2 * (f % 2) + part
            up = _dot(h, wu_ref[:, cols])
            pad_ref[buf, 0:SUBLANES, :] = jnp.where(j == 0, 0.0, carry_ref[:, cols])
            pad_ref[buf, SUBLANES:SUBLANES + tm, :] = up
            carry_ref[:, cols] = up[tm - SUBLANES:tm, :]
            y = cw_ref[FFN_CONV - 1:FFN_CONV, cols] * up
            for tap in range(FFN_CONV - 1):
                off = SUBLANES - (FFN_CONV - 1) + tap
                y = y + cw_ref[tap:tap + 1, cols] * pad_ref[buf, off:off + tm, :]
            conv.append(y)
        act_ref[:, f * fc:(f + 1) * fc] = (jax.nn.gelu(conv[0], approximate=True) * conv[1]).astype(BF16)
    x = x + _dot(act_ref[...], wd_ref[...])
    gate = jax.nn.sigmoid(_dot(_rms(x, pnw_ref[...]).astype(BF16), wg_ref[...]))
    x = x + _dot(p_ref[...].astype(BF16), wp_ref[...]) * gate
    if final:
        x = _rms(x, fw_ref[...])
    o_ref[...] = x


def _ffn_ple(x, nw, w_up, w_conv, w_down, p, ple_nw, w_gate, w_proj, final_w, layer, final, bsz, seq,
             tm=256, fc=256):
    nt = seq // tm
    resident = dict(pipeline_mode=pl.Buffered(1))
    row = lambda b, j: (b * nt + j, 0)
    return pl.pallas_call(
        functools.partial(_ffn_ple_kernel, tm=tm, fc=fc, final=final),
        grid=(bsz, nt),
        in_specs=[
            pl.BlockSpec((tm, D_MODEL), row),
            _layer_spec((1, D_MODEL), layer),
            _layer_spec((D_MODEL, 2 * FFN_DIM), layer, **resident),
            _layer_spec((FFN_CONV, 2 * FFN_DIM), layer),
            _layer_spec((FFN_DIM, D_MODEL), layer, **resident),
            _layer_spec((tm, PLE_DIM), layer, row),
            _layer_spec((1, D_MODEL), layer),
            _layer_spec((D_MODEL, D_MODEL), layer, **resident),
            _layer_spec((PLE_DIM, D_MODEL), layer, **resident),
            pl.BlockSpec((1, D_MODEL), lambda b, j: (0, 0)),
        ],
        out_specs=pl.BlockSpec((tm, D_MODEL), row),
        out_shape=jax.ShapeDtypeStruct((bsz * seq, D_MODEL), F32),
        scratch_shapes=[
            pltpu.VMEM((tm, FFN_DIM), BF16),
            pltpu.VMEM((SUBLANES, 2 * FFN_DIM), F32),
            pltpu.VMEM((4, SUBLANES + tm, fc), F32),
        ],
        compiler_params=_cparams(("parallel", "arbitrary")),
        name="ffn_ple",
    )(x, nw, w_up, w_conv, w_down, p, ple_nw, w_gate, w_proj, final_w)


def _split_w_in(w):
    pool = w[..., 0:512]
    dn_qkv = w[..., 512:2048]
    dn_z = w[..., 2048:2560]
    dn_ba = w[..., 2560:2568]
    mb_qk = w[..., 2568:3592]
    mb_v = w[..., 3592:4104]
    gates = w[..., 4104:7176]
    main = jnp.concatenate([pool, dn_qkv, dn_z, mb_v, mb_qk, gates], axis=-1).astype(BF16)
    small = jnp.pad(dn_ba, ((0, 0),) * (w.ndim - 1) + ((0, LANES - 2 * DN_HEADS),)).astype(BF16)
    return main, small


def kernel(x, p, rel_bias, norm_mix, w_in, pool_w, pool_scale, dn_conv, dn_a_log, dn_dt_bias,
           dn_norm, w_branch, w_out, norm_ffn, ffn_up, ffn_conv, ffn_down, norm_ple, ple_gate,
           ple_proj, norm_final):
    bsz, seq, d = x.shape
    depth = w_in.shape[0]
    t = bsz * seq
    xt = x.reshape(t, d)
    row = lambda a: a[:, None, :]
    w_main, w_small = _split_w_in(w_in)
    pool_w, w_branch, w_out = pool_w.astype(BF16), w_branch.astype(BF16), w_out.astype(BF16)
    ffn_up, ffn_down = ffn_up.astype(BF16), ffn_down.astype(BF16)
    ple_gate, ple_proj = ple_gate.astype(BF16), ple_proj.astype(BF16)
    p = p.reshape(depth, t, PLE_DIM)
    dist_bias = _moba_dist_bias(rel_bias)
    head_params = jnp.zeros((depth, SUBLANES, LANES), F32)
    head_params = head_params.at[:, 0, DN_HEADS:2 * DN_HEADS].set(dn_a_log)
    head_params = head_params.at[:, 1, DN_HEADS:2 * DN_HEADS].set(dn_dt_bias)
    for i in range(depth):
        z, zs, zqk = _in_proj(xt, row(norm_mix), w_main, w_small, i)
        y_b = _deltanet(z, zs, dn_conv, head_params, row(dn_norm), i, bsz, seq)
        y_c = _moba(z, zqk, dist_bias, bsz, seq)
        xt = _merge(y_b, y_c, z, xt, pool_w, row(pool_scale), w_branch, w_out, i, bsz, seq)
        xt = _ffn_ple(xt, row(norm_ffn), ffn_up, ffn_conv, ffn_down, p, row(norm_ple), ple_gate, ple_proj,
                      norm_final[None, :], i, i == depth - 1, bsz, seq)
    return xt.reshape(bsz, seq, d)
```

```python
import functools
import math

import jax
import jax.numpy as jnp
from jax import lax
from jax.experimental import pallas as pl
from jax.experimental.pallas import tpu as pltpu

F32 = jnp.float32
BF16 = jnp.bfloat16
HIGHEST = lax.Precision.HIGHEST
ACT = jnp.bfloat16

D_MODEL = 1024
BRANCH_WIDTH = 512
POOL_WINDOWS = (2, 4, 8, 16)
POOL_GC = 128
POOL_HALO = 16
DN_HEADS = 4
DN_HEAD_DIM = 128
DN_CONV = 4
DN_CHUNK = 64
MOBA_HEADS = 8
MOBA_HEAD_DIM = 64
MOBA_BLOCK = 256
MOBA_TOPK = 3
MOBA_ONES_ROWS = 16
MOBA_QTILE = 128
REL_BUCKETS = 32
REL_MAX_DIST = 128
FFN_DIM = 2816
FFN_CONV = 3
PLE_DIM = 256
NORM_EPS = 1e-6
NEG_INF = -1e30
LOG2E = math.log2(math.e)
LANES = 128
SUBLANES = 8

COL_POOL = 0
COL_DNQ = 512
COL_DNK = 1024
COL_DNV = 1536
COL_DNZ = 2048
COL_MV = 2560
COL_MQ = 3072
COL_MK = 3584
COL_GATE = 4096
Z_COLS = 7168

VMEM_LIMIT = 56 * 1024 * 1024


def _cparams(sem):
    return pltpu.CompilerParams(dimension_semantics=sem, vmem_limit_bytes=VMEM_LIMIT)


def _layer_spec(block, layer, index=None, **kwargs):
    def index_map(*grid):
        rest = index(*grid) if index is not None else (0,) * len(block)
        return (layer,) + tuple(rest)
    return pl.BlockSpec((None,) + tuple(block), index_map, **kwargs)


def _dot(a, b, precision=None):
    return lax.dot_general(a, b, (((1,), (0,)), ((), ())), precision=precision,
                           preferred_element_type=F32)


def _dot_nt(a, b, precision=None):
    return lax.dot_general(a, b, (((1,), (1,)), ((), ())), precision=precision,
                           preferred_element_type=F32)


def _dot_tn(a, b, precision=None):
    return lax.dot_general(a, b, (((0,), (0,)), ((), ())), precision=precision,
                           preferred_element_type=F32)


def _split3(x):
    hi = x.astype(BF16)
    rest = x - hi.astype(F32)
    mid = rest.astype(BF16)
    lo = (rest - mid.astype(F32)).astype(BF16)
    return hi, mid, lo


def _mm_dims(a, b, dims, passes):
    dot = lambda x, y: lax.dot_general(x, y, dims, preferred_element_type=F32)
    a_hi = a.astype(BF16)
    b_hi = b.astype(BF16)
    if passes == 1:
        return dot(a_hi, b_hi)
    a_lo = (a - a_hi.astype(F32)).astype(BF16)
    b_lo = (b - b_hi.astype(F32)).astype(BF16)
    return dot(a_hi, b_hi) + (dot(a_hi, b_lo) + dot(a_lo, b_hi))


def _mm(a, b, passes):
    return _mm_dims(a, b, (((1,), (0,)), ((), ())), passes)


def _mm_nt(a, b, passes):
    return _mm_dims(a, b, (((1,), (1,)), ((), ())), passes)


def _mm_tn(a, b, passes):
    return _mm_dims(a, b, (((0,), (0,)), ((), ())), passes)


DN_PASSES_QK = 1
DN_PASSES_INV = 1
DN_PASSES_STATE = 1
DN_INV_BASE = 8
DN_GROUP = 4


def _rms(x, w):
    return x * lax.rsqrt(jnp.mean(x * x, axis=-1, keepdims=True) + NORM_EPS) * w


def _in_proj_kernel(x_ref, nw_ref, w_ref, ws_ref, z_ref, zs_ref, zqk_ref, *, tn):
    hb = _rms(x_ref[...], nw_ref[...]).astype(BF16)
    zs_ref[...] = _dot(hb, ws_ref[...])
    for jb in range(Z_COLS // tn):
        cols = slice(jb * tn, (jb + 1) * tn)
        z = _dot(hb, w_ref[:, cols])
        z_ref[:, cols] = z.astype(ACT)
        if jb == COL_MQ // tn:
            zqk_ref[...] = z


def _in_proj(x, nw, w_main, w_small, layer, tm=512, tn=1024):
    t = x.shape[0]
    assert COL_MQ % tn == 0 and COL_MK == COL_MQ + tn // 2
    resident = dict(pipeline_mode=pl.Buffered(1))
    return pl.pallas_call(
        functools.partial(_in_proj_kernel, tn=tn),
        grid=(t // tm,),
        in_specs=[
            pl.BlockSpec((tm, D_MODEL), lambda i: (i, 0)),
            _layer_spec((1, D_MODEL), layer),
            _layer_spec((D_MODEL, Z_COLS), layer, **resident),
            _layer_spec((D_MODEL, LANES), layer, **resident),
        ],
        out_specs=[
            pl.BlockSpec((tm, Z_COLS), lambda i: (i, 0)),
            pl.BlockSpec((tm, LANES), lambda i: (i, 0)),
            pl.BlockSpec((tm, tn), lambda i: (i, 0)),
        ],
        out_shape=[jax.ShapeDtypeStruct((t, Z_COLS), ACT),
                   jax.ShapeDtypeStruct((t, LANES), F32),
                   jax.ShapeDtypeStruct((t, tn), F32)],
        compiler_params=_cparams(("parallel",)),
        name="in_proj",
    )(x, nw, w_main, w_small)


def _pool_mixer(u_ref, up_ref, w_ref, sc_ref, j, tp):
    halo = jnp.where(j == 0, 0.0, up_ref[...].astype(F32))
    t = j * tp + lax.broadcasted_iota(jnp.int32, (tp, POOL_GC), 0)
    out = []
    for g, win in enumerate(POOL_WINDOWS):
        cols = slice(g * POOL_GC, (g + 1) * POOL_GC)
        u = u_ref[:, cols].astype(F32)
        tot = jnp.concatenate([halo[:, cols], u], axis=0)
        width = 1
        while width < win:
            tot = tot + pltpu.roll(tot, width, 0)
            width *= 2
        cnt = jnp.minimum(t + 1, win).astype(F32)
        mixed = tot[POOL_HALO:, :] / cnt - u
        y = _dot(mixed.astype(BF16), w_ref[g])
        out.append((y * sc_ref[:, cols]).astype(BF16))
    return jnp.concatenate(out, axis=1)


def _dn_kernel(q_ref, k_ref, v_ref, zg_ref, zs_ref, cwq_ref, cwk_ref, cwv_ref, hp_ref, nw_ref, tri_ref,
               y_ref, halo_ref, state_ref, qs_ref, ks_ref, vs_ref, kb_ref, kg_ref, qd_ref,
               kd_ref, gs_ref, os_ref, *, tc):
    c = pl.program_id(1)

    @pl.when(c == 0)
    def _():
        halo_ref[...] = jnp.zeros_like(halo_ref)
        state_ref[...] = jnp.zeros_like(state_ref)

    for part, (src, cw, dst) in enumerate(((q_ref, cwq_ref, qs_ref), (k_ref, cwk_ref, ks_ref),
                                          (v_ref, cwv_ref, vs_ref))):
        x = src[...].astype(F32)
        ext = jnp.concatenate([halo_ref[part], x], axis=0)
        halo_ref[part] = x[tc - SUBLANES:tc, :]
        acc = cw[0:1, :] * ext
        for tap in range(1, DN_CONV):
            acc = cw[tap:tap + 1, :] * ext + pltpu.roll(acc, 1, 0)
        conv = acc[SUBLANES:, :]
        dst[...] = conv * jax.nn.sigmoid(conv)

    n_chunks = tc // DN_CHUNK
    zs = zs_ref[...]
    beta_c = jax.nn.sigmoid(zs)
    g_c = -jnp.exp(hp_ref[0:1, :]) * jnp.logaddexp(zs + hp_ref[1:2, :], 0.0)
    gc_c = sum(_dot(tri_ref[...], part) for part in _split3(g_c))
    g_last_c = jnp.concatenate(
        [jnp.broadcast_to(gc_c[(n + 1) * DN_CHUNK - 1:(n + 1) * DN_CHUNK, :], (DN_CHUNK, LANES))
         for n in range(n_chunks)], axis=0)
    egc_c = jnp.exp(gc_c)
    kdf_c = jnp.exp(g_last_c - gc_c)

    for h in range(DN_HEADS):
        cols = slice(h * DN_HEAD_DIM, (h + 1) * DN_HEAD_DIM)
        lane_bcast = lambda x, lane: jnp.broadcast_to(x[:, lane:lane + 1], (tc, LANES))
        beta = lane_bcast(beta_c, h)
        egc = lane_bcast(egc_c, DN_HEADS + h)
        q = qs_ref[:, cols]
        k = ks_ref[:, cols]
        q = q * lax.rsqrt(jnp.sum(q * q, axis=-1, keepdims=True) + NORM_EPS) * (DN_HEAD_DIM ** -0.5)
        k = k * lax.rsqrt(jnp.sum(k * k, axis=-1, keepdims=True) + NORM_EPS)
        kb = k * beta
        qs_ref[:, cols] = q
        ks_ref[:, cols] = k
        kb_ref[:, cols] = kb
        vs_ref[:, cols] = vs_ref[:, cols] * beta
        kg_ref[:, cols] = kb * egc
        qd_ref[:, cols] = q * egc
        kd_ref[:, cols] = k * lane_bcast(kdf_c, DN_HEADS + h)
        gs_ref[h] = lane_bcast(gc_c, DN_HEADS + h)

    hc = DN_HEADS * DN_CHUNK
    ri = lax.broadcasted_iota(jnp.int32, (hc, hc), 0)
    ci = lax.broadcasted_iota(jnp.int32, (hc, hc), 1)
    same_head = (ri // DN_CHUNK) == (ci // DN_CHUNK)
    incl = same_head & (ri >= ci)
    level_masks = [((ri // DN_INV_BASE) == (ci // DN_INV_BASE)) & (ri > ci)]
    size = DN_INV_BASE
    while size < DN_CHUNK:
        level_masks.append(((ri // (2 * size)) == (ci // (2 * size)))
                           & ((ri % (2 * size)) >= size) & ((ci % (2 * size)) < size))
        size *= 2
    eye_f = (ri == ci).astype(F32)

    def stack(ref, rows):
        return jnp.concatenate([ref[rows, h * DN_HEAD_DIM:(h + 1) * DN_HEAD_DIM]
                                for h in range(DN_HEADS)], axis=0)

    heads = range(DN_HEADS)
    last = lambda x, h: x[(h + 1) * DN_CHUNK - 1:(h + 1) * DN_CHUNK, :]
    state = [state_ref[h] for h in heads]
    for first in range(0, n_chunks, DN_GROUP):
        grp = range(first, min(first + DN_GROUP, n_chunks))
        rows = [slice(n * DN_CHUNK, (n + 1) * DN_CHUNK) for n in grp]
        q = [stack(qs_ref, r) for r in rows]
        k = [stack(ks_ref, r) for r in rows]
        kb = [stack(kb_ref, r) for r in rows]
        vb = [stack(vs_ref, r) for r in rows]
        kg = [stack(kg_ref, r) for r in rows]
        q_dec = [stack(qd_ref, r) for r in rows]
        k_dec = [stack(kd_ref, r) for r in rows]
        gc = [jnp.concatenate([gs_ref[h, r, :] for h in heads], axis=0) for r in rows]
        gc_t = [x.T for x in gc]
        decay = [jnp.exp(jnp.where(incl, jnp.concatenate([x, x], axis=1)
                                   - jnp.concatenate([xt, xt], axis=0), NEG_INF))
                 for x, xt in zip(gc, gc_t)]
        a_kk = [_mm_nt(x, y, DN_PASSES_QK) * d for x, y, d in zip(kb, k, decay)]
        m = [-jnp.where(level_masks[0], a, 0.0) for a in a_kk]
        t_mat = [eye_f + x for x in m]
        for _ in range(2):
            m = [_mm(x, x, DN_PASSES_INV) for x in m]
            t_mat = [t + _mm(t, x, DN_PASSES_INV) for t, x in zip(t_mat, m)]
        for mask in level_masks[1:]:
            low = [_mm(t, jnp.where(mask, a, 0.0), DN_PASSES_INV) for t, a in zip(t_mat, a_kk)]
            t_mat = [t - _mm(x, t, DN_PASSES_INV) for t, x in zip(t_mat, low)]
        uw = [_mm(t, jnp.concatenate([x, y], axis=1), DN_PASSES_INV)
              for t, x, y in zip(t_mat, vb, kg)]
        a_qk = [_mm_nt(x, y, DN_PASSES_QK) * d for x, y, d in zip(q, k, decay)]
        for i, r in enumerate(rows):
            u = uw[i][:, :DN_HEAD_DIM]
            w = uw[i][:, DN_HEAD_DIM:]
            ws, qs_ = [], []
            for h in heads:
                hr = slice(h * DN_CHUNK, (h + 1) * DN_CHUNK)
                both = _mm(jnp.concatenate([w[hr], q_dec[i][hr]], axis=0), state[h], DN_PASSES_STATE)
                ws.append(both[:DN_CHUNK])
                qs_.append(both[DN_CHUNK:])
            v_new = u - jnp.concatenate(ws, axis=0)
            o = jnp.concatenate(qs_, axis=0) + _mm(a_qk[i], v_new, DN_PASSES_STATE)
            for h in heads:
                hr = slice(h * DN_CHUNK, (h + 1) * DN_CHUNK)
                os_ref[r, h * DN_HEAD_DIM:(h + 1) * DN_HEAD_DIM] = o[hr]
                state[h] = (state[h] * jnp.exp(last(gc[i], h))
                            + _mm_tn(k_dec[i][hr], v_new[hr], DN_PASSES_STATE))
    for h in heads:
        state_ref[h] = state[h]

    for h in range(DN_HEADS):
        cols = slice(h * DN_HEAD_DIM, (h + 1) * DN_HEAD_DIM)
        zg = zg_ref[:, cols].astype(F32)
        y_ref[:, cols] = (_rms(os_ref[:, cols], nw_ref[...]) * (zg * jax.nn.sigmoid(zg))).astype(ACT)


def _deltanet(z, zs, conv_w, head_params, norm_w, layer, bsz, seq, tc=256):
    nt = seq // tc
    wide = DN_HEADS * DN_HEAD_DIM

    def col(cb):
        return pl.BlockSpec((tc, wide), lambda b, c: (b * nt + c, cb))

    def cw(cb):
        return _layer_spec((DN_CONV, wide), layer, lambda b, c: (0, cb))

    pos = jnp.arange(tc, dtype=jnp.int32)
    tri = ((pos[:, None] // DN_CHUNK == pos[None, :] // DN_CHUNK) & (pos[:, None] >= pos[None, :])).astype(BF16)
    tile = pltpu.VMEM((tc, wide), F32)
    return pl.pallas_call(
        functools.partial(_dn_kernel, tc=tc),
        grid=(bsz, nt),
        in_specs=[
            col(COL_DNQ // wide), col(COL_DNK // wide), col(COL_DNV // wide), col(COL_DNZ // wide),
            pl.BlockSpec((tc, LANES), lambda b, c: (b * nt + c, 0)),
            cw(0), cw(1), cw(2),
            _layer_spec((SUBLANES, LANES), layer),
            _layer_spec((1, DN_HEAD_DIM), layer),
            pl.BlockSpec((tc, tc), lambda b, c: (0, 0)),
        ],
        out_specs=pl.BlockSpec((tc, wide), lambda b, c: (b * nt + c, 0)),
        out_shape=jax.ShapeDtypeStruct((bsz * seq, wide), ACT),
        scratch_shapes=[
            pltpu.VMEM((3, SUBLANES, wide), F32),
            pltpu.VMEM((DN_HEADS, DN_HEAD_DIM, DN_HEAD_DIM), F32),
            tile, tile, tile, tile, tile, tile, tile,
            pltpu.VMEM((DN_HEADS, tc, LANES), F32),
            tile,
        ],
        compiler_params=_cparams(("parallel", "arbitrary")),
        name="deltanet",
    )(z, z, z, z, zs, conv_w, conv_w, conv_w, head_params, norm_w, tri)


def _moba_kernel(q_ref, k_ref, v_ref, d_ref, o_ref, kb_ref, vt_ref, bown_ref, bprev_ref, *, nb):
    blk = MOBA_BLOCK
    half = LANES // 2

    kmeans = []
    for j in range(nb):
        kj = k_ref[j * blk:(j + 1) * blk, :]
        kmeans.append(jnp.mean(kj.astype(F32), axis=0, keepdims=True))
        kb_ref[j * blk:(j + 1) * blk, :] = kj.astype(BF16)
        vt_ref[j, 0:LANES, :] = v_ref[j * blk:(j + 1) * blk, :].astype(F32).T.astype(BF16)
        vt_ref[j, LANES:LANES + MOBA_ONES_ROWS, :] = jnp.ones((MOBA_ONES_ROWS, blk), BF16)
    kmean = jnp.concatenate(kmeans, axis=0)

    r = lax.broadcasted_iota(jnp.int32, (LANES, LANES), 0)
    c = lax.broadcasted_iota(jnp.int32, (LANES, LANES), 1)
    for hh in range(2):
        far = jnp.broadcast_to(d_ref[hh, 1:2, :], (LANES, LANES)) * LOG2E
        band = pltpu.roll(jnp.broadcast_to(d_ref[hh, 0:1, :], (LANES, LANES)), 0, 1,
                          stride=1, stride_axis=0) * LOG2E
        diag = jnp.where(c >= r, band, NEG_INF)
        off = jnp.where(c < r, band, far)
        lo = hh * blk
        bown_ref[0:LANES, lo:lo + LANES] = diag
        bown_ref[0:LANES, lo + LANES:lo + blk] = off
        bown_ref[LANES:blk, lo:lo + LANES] = jnp.full((LANES, LANES), NEG_INF, F32)
        bown_ref[LANES:blk, lo + LANES:lo + blk] = diag
        bprev_ref[0:LANES, lo:lo + LANES] = far
        bprev_ref[0:LANES, lo + LANES:lo + blk] = far
        bprev_ref[LANES:blk, lo:lo + LANES] = off
        bprev_ref[LANES:blk, lo + LANES:lo + blk] = far
    far_row = jnp.concatenate([jnp.broadcast_to(d_ref[0, 1:2, :], (1, LANES))] * 2
                              + [jnp.broadcast_to(d_ref[1, 1:2, :], (1, LANES))] * 2, axis=1) * LOG2E

    qt = MOBA_QTILE
    tiles_per_blk = blk // qt
    lane = lax.broadcasted_iota(jnp.int32, (qt, LANES), 1)
    blk_id = lax.broadcasted_iota(jnp.int32, (nb, 2 * qt), 0)
    row = lax.broadcasted_iota(jnp.int32, (LANES, qt), 0)
    far_tile = jnp.concatenate([far_row[:, :qt], far_row[:, blk:blk + qt]], axis=1)

    def tile_cols(ref, part, keys):
        return jnp.concatenate([ref[0:keys, part * qt:(part + 1) * qt],
                                ref[0:keys, blk + part * qt:blk + (part + 1) * qt]], axis=1)

    def score_stage(tile):
        i, part = divmod(tile, tiles_per_blk)
        rows = slice(tile * qt, (tile + 1) * qt)
        qf = q_ref[rows, :].astype(F32) * (MOBA_HEAD_DIM ** -0.5)
        q2 = jnp.concatenate([jnp.where(lane < half, qf, 0.0), jnp.where(lane >= half, qf, 0.0)],
                             axis=0)
        q2b = (q2 * LOG2E).astype(BF16)

        sel = None
        if i > MOBA_TOPK:
            gate = jnp.where(blk_id < i, _dot_nt(kmean, q2, HIGHEST), NEG_INF)
            sel = []
            for j in range(i):
                gj = gate[j:j + 1, :]
                ahead = (gate > gj) | ((gate == gj) & (blk_id < j))
                rank = jnp.sum(ahead.astype(F32), axis=0, keepdims=True)
                sel.append(rank < MOBA_TOPK)

        scores = []
        for j in range(i + 1):
            if j == i:
                keys = (part + 1) * qt
                s = _dot_nt(kb_ref[j * blk:j * blk + keys, :], q2b) + tile_cols(bown_ref, part, keys)
            else:
                s = _dot_nt(kb_ref[j * blk:(j + 1) * blk, :], q2b)
                if j == i - 1:
                    s = s + tile_cols(bprev_ref, part, blk)
                    if sel is not None:
                        s = s + jnp.where(sel[j], 0.0, NEG_INF)
                elif sel is not None:
                    s = s + jnp.where(sel[j], far_tile, NEG_INF)
                else:
                    s = s + far_tile
            scores.append(s)
        return scores

    def exp_stage(scores):
        m = functools.reduce(jnp.maximum, [jnp.max(s, axis=0, keepdims=True) for s in scores])
        return ([jnp.exp2(s - m).astype(BF16) for s in scores],)

    def pv_stage(tile, probs):
        acc = None
        for j, p in enumerate(probs):
            pv = _dot(vt_ref[j, :, 0:p.shape[0]], p)
            acc = pv if acc is None else acc + pv
        out_t = acc[:LANES, :] / acc[LANES:LANES + 1, :]
        out_t = jnp.where(row < half, out_t[:, :qt], out_t[:, qt:])
        o_ref[tile * qt:(tile + 1) * qt, :] = out_t.T.astype(ACT)

    sc_q, ex_q = None, None
    for tile in list(reversed(range(nb * tiles_per_blk))) + [None, None]:
        new_sc = (tile, score_stage(tile)) if tile is not None else None
        new_ex = (sc_q[0],) + exp_stage(sc_q[1]) if sc_q is not None else None
        if ex_q is not None:
            pv_stage(*ex_q)
        sc_q, ex_q = new_sc, new_ex


def _moba(z, zqk, dist_bias, bsz, seq):
    nb = seq // MOBA_BLOCK
    pairs = MOBA_HEADS // 2
    kcb = (COL_MK - COL_MQ) // LANES
    vcb = COL_MV // LANES
    return pl.pallas_call(
        functools.partial(_moba_kernel, nb=nb),
        grid=(bsz, pairs),
        in_specs=[
            pl.BlockSpec((seq, LANES), lambda b, p: (b, p)),
            pl.BlockSpec((seq, LANES), lambda b, p: (b, kcb + p)),
            pl.BlockSpec((seq, LANES), lambda b, p: (b, vcb + p)),
            pl.BlockSpec((2, SUBLANES, LANES), lambda b, p: (p, 0, 0)),
        ],
        out_specs=pl.BlockSpec((seq, LANES), lambda b, p: (b, p)),
        out_shape=jax.ShapeDtypeStruct((bsz * seq, MOBA_HEADS * MOBA_HEAD_DIM), ACT),
        scratch_shapes=[
            pltpu.VMEM((seq, LANES), BF16),
            pltpu.VMEM((nb, LANES + MOBA_ONES_ROWS, MOBA_BLOCK), BF16),
            pltpu.VMEM((MOBA_BLOCK, 2 * MOBA_BLOCK), F32),
            pltpu.VMEM((MOBA_BLOCK, 2 * MOBA_BLOCK), F32),
        ],
        compiler_params=_cparams(("parallel", "parallel")),
        name="moba",
    )(zqk, zqk, z, dist_bias)


def _t5_bucket(rel):
    n = jnp.maximum(-rel, 0)
    exact = REL_BUCKETS // 2
    nf = jnp.maximum(n, 1).astype(F32)
    large = exact + (jnp.log(nf / exact) / math.log(REL_MAX_DIST / exact)
                     * (REL_BUCKETS - exact)).astype(jnp.int32)
    large = jnp.minimum(large, REL_BUCKETS - 1)
    return jnp.where(n < exact, n, large)


def _moba_dist_bias(rel_bias):
    bias_ht = rel_bias.T.astype(F32)
    near = bias_ht[:, _t5_bucket(-jnp.arange(LANES, dtype=jnp.int32))]
    far = bias_ht[:, _t5_bucket(jnp.full((LANES,), -REL_MAX_DIST, jnp.int32))]
    rows = jnp.zeros((MOBA_HEADS, SUBLANES, LANES), F32)
    return rows.at[:, 0, :].set(near).at[:, 1, :].set(far)


def _merge_kernel(u_ref, up_ref, pw_ref, ps_ref, yb_ref, yc_ref, ga_ref, gb_ref, gc_ref, x_ref, wb_ref,
                  wo_ref, o_ref, *, tm):
    y_a = _pool_mixer(u_ref, up_ref, pw_ref, ps_ref, pl.program_id(1), tm)
    merged = None
    for n, (y, g_ref) in enumerate(((y_a, ga_ref), (yb_ref[...], gb_ref), (yc_ref[...], gc_ref))):
        proj = _dot(y, wb_ref[n])
        gate = jax.nn.sigmoid(g_ref[...].astype(F32))
        term = gate * proj
        merged = term if merged is None else merged + term
    o_ref[...] = x_ref[...] + _dot(merged.astype(BF16), wo_ref[...])


def _merge(yb, yc, z, x, pool_w, pool_scale, w_branch, w_out, layer, bsz, seq, tm=512):
    nt = seq // tm
    hb = tm // POOL_HALO
    gcb = COL_GATE // D_MODEL
    row = lambda b, j: (b * nt + j, 0)
    return pl.pallas_call(
        functools.partial(_merge_kernel, tm=tm),
        grid=(bsz, nt),
        in_specs=[
            pl.BlockSpec((tm, BRANCH_WIDTH), row),
            pl.BlockSpec((POOL_HALO, BRANCH_WIDTH),
                         lambda b, j: (jnp.maximum((b * nt + j) * hb - 1, 0), 0)),
            _layer_spec((4, POOL_GC, POOL_GC), layer),
            _layer_spec((1, BRANCH_WIDTH), layer),
            pl.BlockSpec((tm, BRANCH_WIDTH), row),
            pl.BlockSpec((tm, BRANCH_WIDTH), row),
            pl.BlockSpec((tm, D_MODEL), lambda b, j: (b * nt + j, gcb)),
            pl.BlockSpec((tm, D_MODEL), lambda b, j: (b * nt + j, gcb + 1)),
            pl.BlockSpec((tm, D_MODEL), lambda b, j: (b * nt + j, gcb + 2)),
            pl.BlockSpec((tm, D_MODEL), row),
            _layer_spec((3, BRANCH_WIDTH, D_MODEL), layer),
            _layer_spec((D_MODEL, D_MODEL), layer),
        ],
        out_specs=pl.BlockSpec((tm, D_MODEL), row),
        out_shape=jax.ShapeDtypeStruct((bsz * seq, D_MODEL), F32),
        compiler_params=_cparams(("parallel", "parallel")),
        name="merge",
    )(z, z, pool_w, pool_scale, yb, yc, z, z, z, x, w_branch, w_out)


def _ffn_ple_kernel(x_ref, nw_ref, wu_ref, cw_ref, wd_ref, p_ref, pnw_ref, wg_ref, wp_ref, fw_ref, o_ref,
                    act_ref, carry_ref, pad_ref, *, tm, fc, final):
    j = pl.program_id(1)
    x = x_ref[...]
    h = _rms(x, nw_ref[...]).astype(BF16)
    for f in range(FFN_DIM // fc):
        conv = []
        for part in range(2):
            cols = slice(part * FFN_DIM + f * fc, part * FFN_DIM + (f + 1) * fc)
            buf = 2 * (f % 2) + part
            up = _dot(h, wu_ref[:, cols])
            pad_ref[buf, 0:SUBLANES, :] = jnp.where(j == 0, 0.0, carry_ref[:, cols])
            pad_ref[buf, SUBLANES:SUBLANES + tm, :] = up
            carry_ref[:, cols] = up[tm - SUBLANES:tm, :]
            y = cw_ref[FFN_CONV - 1:FFN_CONV, cols] * up
            for tap in range(FFN_CONV - 1):
                off = SUBLANES - (FFN_CONV - 1) + tap
                y = y + cw_ref[tap:tap + 1, cols] * pad_ref[buf, off:off + tm, :]
            conv.append(y)
        act_ref[:, f * fc:(f + 1) * fc] = (jax.nn.gelu(conv[0], approximate=True) * conv[1]).astype(BF16)
    x = x + _dot(act_ref[...], wd_ref[...])
    gate = jax.nn.sigmoid(_dot(_rms(x, pnw_ref[...]).astype(BF16), wg_ref[...]))
    x = x + _dot(p_ref[...].astype(BF16), wp_ref[...]) * gate
    if final:
        x = _rms(x, fw_ref[...])
    o_ref[...] = x


def _ffn_ple(x, nw, w_up, w_conv, w_down, p, ple_nw, w_gate, w_proj, final_w, layer, final, bsz, seq,
             tm=256, fc=256):
    nt = seq // tm
    resident = dict(pipeline_mode=pl.Buffered(1))
    row = lambda b, j: (b * nt + j, 0)
    return pl.pallas_call(
        functools.partial(_ffn_ple_kernel, tm=tm, fc=fc, final=final),
        grid=(bsz, nt),
        in_specs=[
            pl.BlockSpec((tm, D_MODEL), row),
            _layer_spec((1, D_MODEL), layer),
            _layer_spec((D_MODEL, 2 * FFN_DIM), layer, **resident),
            _layer_spec((FFN_CONV, 2 * FFN_DIM), layer),
            _layer_spec((FFN_DIM, D_MODEL), layer, **resident),
            _layer_spec((tm, PLE_DIM), layer, row),
            _layer_spec((1, D_MODEL), layer),
            _layer_spec((D_MODEL, D_MODEL), layer, **resident),
            _layer_spec((PLE_DIM, D_MODEL), layer, **resident),
            pl.BlockSpec((1, D_MODEL), lambda b, j: (0, 0)),
        ],
        out_specs=pl.BlockSpec((tm, D_MODEL), row),
        out_shape=jax.ShapeDtypeStruct((bsz * seq, D_MODEL), F32),
        scratch_shapes=[
            pltpu.VMEM((tm, FFN_DIM), BF16),
            pltpu.VMEM((SUBLANES, 2 * FFN_DIM), F32),
            pltpu.VMEM((4, SUBLANES + tm, fc), F32),
        ],
        compiler_params=_cparams(("parallel", "arbitrary")),
        name="ffn_ple",
    )(x, nw, w_up, w_conv, w_down, p, ple_nw, w_gate, w_proj, final_w)


def _split_w_in(w):
    pool = w[..., 0:512]
    dn_qkv = w[..., 512:2048]
    dn_z = w[..., 2048:2560]
    dn_ba = w[..., 2560:2568]
    mb_qk = w[..., 2568:3592]
    mb_v = w[..., 3592:4104]
    gates = w[..., 4104:7176]
    main = jnp.concatenate([pool, dn_qkv, dn_z, mb_v, mb_qk, gates], axis=-1).astype(BF16)
    small = jnp.pad(dn_ba, ((0, 0),) * (w.ndim - 1) + ((0, LANES - 2 * DN_HEADS),)).astype(BF16)
    return main, small


def kernel(x, p, rel_bias, norm_mix, w_in, pool_w, pool_scale, dn_conv, dn_a_log, dn_dt_bias,
           dn_norm, w_branch, w_out, norm_ffn, ffn_up, ffn_conv, ffn_down, norm_ple, ple_gate,
           ple_proj, norm_final):
    bsz, seq, d = x.shape
    depth = w_in.shape[0]
    t = bsz * seq
    xt = x.reshape(t, d)
    row = lambda a: a[:, None, :]
    w_main, w_small = _split_w_in(w_in)
    pool_w, w_branch, w_out = pool_w.astype(BF16), w_branch.astype(BF16), w_out.astype(BF16)
    ffn_up, ffn_down = ffn_up.astype(BF16), ffn_down.astype(BF16)
    ple_gate, ple_proj = ple_gate.astype(BF16), ple_proj.astype(BF16)
    p = p.reshape(depth, t, PLE_DIM)
    dist_bias = _moba_dist_bias(rel_bias)
    head_params = jnp.zeros((depth, SUBLANES, LANES), F32)
    head_params = head_params.at[:, 0, DN_HEADS:2 * DN_HEADS].set(dn_a_log)
    head_params = head_params.at[:, 1, DN_HEADS:2 * DN_HEADS].set(dn_dt_bias)
    for i in range(depth):
        z, zs, zqk = _in_proj(xt, row(norm_mix), w_main, w_small, i)
        y_b = _deltanet(z, zs, dn_conv, head_params, row(dn_norm), i, bsz, seq)
        y_c = _moba(z, zqk, dist_bias, bsz, seq)
        xt = _merge(y_b, y_c, z, xt, pool_w, row(pool_scale), w_branch, w_out, i, bsz, seq)
        xt = _ffn_ple(xt, row(norm_ffn), ffn_up, ffn_conv, ffn_down, p, row(norm_ple), ple_gate, ple_proj,
                      norm_final[None, :], i, i == depth - 1, bsz, seq)
    return xt.reshape(bsz, seq, d)
```

```python
import functools
import math

import jax
import jax.numpy as jnp
from jax import lax
from jax.experimental import pallas as pl
from jax.experimental.pallas import tpu as pltpu

F32 = jnp.float32
BF16 = jnp.bfloat16
HIGHEST = lax.Precision.HIGHEST
ACT = jnp.bfloat16

D_MODEL = 1024
BRANCH_WIDTH = 512
POOL_WINDOWS = (2, 4, 8, 16)
POOL_GC = 128
POOL_HALO = 16
DN_HEADS = 4
DN_HEAD_DIM = 128
DN_CONV = 4
DN_CHUNK = 64
MOBA_HEADS = 8
MOBA_HEAD_DIM = 64
MOBA_BLOCK = 256
MOBA_TOPK = 3
MOBA_ONES_ROWS = 16
MOBA_QTILE = 128
REL_BUCKETS = 32
REL_MAX_DIST = 128
FFN_DIM = 2816
FFN_CONV = 3
PLE_DIM = 256
NORM_EPS = 1e-6
NEG_INF = -1e30
LOG2E = math.log2(math.e)
LANES = 128
SUBLANES = 8

COL_POOL = 0
COL_DNQ = 512
COL_DNK = 1024
COL_DNV = 1536
COL_DNZ = 2048
COL_MV = 2560
COL_MQ = 3072
COL_MK = 3584
COL_GATE = 4096
Z_COLS = 7168

VMEM_LIMIT = 56 * 1024 * 1024


def _cparams(sem):
    return pltpu.CompilerParams(dimension_semantics=sem, vmem_limit_bytes=VMEM_LIMIT)


def _layer_spec(block, layer, index=None, **kwargs):
    def index_map(*grid):
        rest = index(*grid) if index is not None else (0,) * len(block)
        return (layer,) + tuple(rest)
    return pl.BlockSpec((None,) + tuple(block), index_map, **kwargs)


def _dot(a, b, precision=None):
    return lax.dot_general(a, b, (((1,), (0,)), ((), ())), precision=precision,
                           preferred_element_type=F32)


def _dot_nt(a, b, precision=None):
    return lax.dot_general(a, b, (((1,), (1,)), ((), ())), precision=precision,
                           preferred_element_type=F32)


def _split3(x):
    hi = x.astype(BF16)
    rest = x - hi.astype(F32)
    mid = rest.astype(BF16)
    lo = (rest - mid.astype(F32)).astype(BF16)
    return hi, mid, lo


def _mm(a, b):
    return _dot(a.astype(BF16), b.astype(BF16))


def _mm_nt(a, b):
    return _dot_nt(a.astype(BF16), b.astype(BF16))


def _mm_tn(a, b):
    return lax.dot_general(a.astype(BF16), b.astype(BF16), (((0,), (0,)), ((), ())),
                           preferred_element_type=F32)


DN_INV_BASE = 8
DN_GROUP = 4


def _rms(x, w):
    return x * lax.rsqrt(jnp.mean(x * x, axis=-1, keepdims=True) + NORM_EPS) * w


def _in_proj_kernel(x_ref, nw_ref, w_ref, ws_ref, z_ref, zs_ref, zqk_ref, *, tn):
    hb = _rms(x_ref[...], nw_ref[...]).astype(BF16)
    zs_ref[...] = _dot(hb, ws_ref[...])
    for jb in range(Z_COLS // tn):
        cols = slice(jb * tn, (jb + 1) * tn)
        z = _dot(hb, w_ref[:, cols])
        z_ref[:, cols] = z.astype(ACT)
        if jb == COL_MQ // tn:
            zqk_ref[...] = z


def _in_proj(x, nw, w_main, w_small, layer, tm=512, tn=1024):
    t = x.shape[0]
    assert COL_MQ % tn == 0 and COL_MK == COL_MQ + tn // 2
    resident = dict(pipeline_mode=pl.Buffered(1))
    return pl.pallas_call(
        functools.partial(_in_proj_kernel, tn=tn),
        grid=(t // tm,),
        in_specs=[
            pl.BlockSpec((tm, D_MODEL), lambda i: (i, 0)),
            _layer_spec((1, D_MODEL), layer),
            _layer_spec((D_MODEL, Z_COLS), layer, **resident),
            _layer_spec((D_MODEL, LANES), layer, **resident),
        ],
        out_specs=[
            pl.BlockSpec((tm, Z_COLS), lambda i: (i, 0)),
            pl.BlockSpec((tm, LANES), lambda i: (i, 0)),
            pl.BlockSpec((tm, tn), lambda i: (i, 0)),
        ],
        out_shape=[jax.ShapeDtypeStruct((t, Z_COLS), ACT),
                   jax.ShapeDtypeStruct((t, LANES), F32),
                   jax.ShapeDtypeStruct((t, tn), F32)],
        compiler_params=_cparams(("parallel",)),
        name="in_proj",
    )(x, nw, w_main, w_small)


def _pool_mixer(u_ref, up_ref, w_ref, sc_ref, j, tp):
    halo = jnp.where(j == 0, 0.0, up_ref[...].astype(F32))
    t = j * tp + lax.broadcasted_iota(jnp.int32, (tp, POOL_GC), 0)
    out = []
    for g, win in enumerate(POOL_WINDOWS):
        cols = slice(g * POOL_GC, (g + 1) * POOL_GC)
        u = u_ref[:, cols].astype(F32)
        tot = jnp.concatenate([halo[:, cols], u], axis=0)
        width = 1
        while width < win:
            tot = tot + pltpu.roll(tot, width, 0)
            width *= 2
        cnt = jnp.minimum(t + 1, win).astype(F32)
        mixed = tot[POOL_HALO:, :] / cnt - u
        y = _dot(mixed.astype(BF16), w_ref[g])
        out.append((y * sc_ref[:, cols]).astype(BF16))
    return jnp.concatenate(out, axis=1)


def _dn_kernel(q_ref, k_ref, v_ref, zg_ref, zs_ref, cwq_ref, cwk_ref, cwv_ref, hp_ref, nw_ref, tri_ref,
               y_ref, halo_ref, state_ref, qs_ref, ks_ref, vs_ref, kb_ref, kg_ref, qd_ref,
               kd_ref, gs_ref, os_ref, *, tc):
    c = pl.program_id(1)

    @pl.when(c == 0)
    def _():
        halo_ref[...] = jnp.zeros_like(halo_ref)
        state_ref[...] = jnp.zeros_like(state_ref)

    for part, (src, cw, dst) in enumerate(((q_ref, cwq_ref, qs_ref), (k_ref, cwk_ref, ks_ref),
                                          (v_ref, cwv_ref, vs_ref))):
        x = src[...].astype(F32)
        ext = jnp.concatenate([halo_ref[part], x], axis=0)
        halo_ref[part] = x[tc - SUBLANES:tc, :]
        acc = cw[0:1, :] * ext
        for tap in range(1, DN_CONV):
            acc = cw[tap:tap + 1, :] * ext + pltpu.roll(acc, 1, 0)
        conv = acc[SUBLANES:, :]
        dst[...] = conv * jax.nn.sigmoid(conv)

    n_chunks = tc // DN_CHUNK
    zs = zs_ref[...]
    beta_c = jax.nn.sigmoid(zs)
    g_c = -jnp.exp(hp_ref[0:1, :]) * jnp.logaddexp(zs + hp_ref[1:2, :], 0.0)
    gc_c = sum(_dot(tri_ref[...], part) for part in _split3(g_c))
    g_last_c = jnp.concatenate(
        [jnp.broadcast_to(gc_c[(n + 1) * DN_CHUNK - 1:(n + 1) * DN_CHUNK, :], (DN_CHUNK, LANES))
         for n in range(n_chunks)], axis=0)
    egc_c = jnp.exp(gc_c)
    kdf_c = jnp.exp(g_last_c - gc_c)

    for h in range(DN_HEADS):
        cols = slice(h * DN_HEAD_DIM, (h + 1) * DN_HEAD_DIM)
        lane_bcast = lambda x, lane: jnp.broadcast_to(x[:, lane:lane + 1], (tc, LANES))
        beta = lane_bcast(beta_c, h)
        egc = lane_bcast(egc_c, DN_HEADS + h)
        q = qs_ref[:, cols]
        k = ks_ref[:, cols]
        q = q * lax.rsqrt(jnp.sum(q * q, axis=-1, keepdims=True) + NORM_EPS) * (DN_HEAD_DIM ** -0.5)
        k = k * lax.rsqrt(jnp.sum(k * k, axis=-1, keepdims=True) + NORM_EPS)
        kb = k * beta
        qs_ref[:, cols] = q
        ks_ref[:, cols] = k
        kb_ref[:, cols] = kb
        vs_ref[:, cols] = vs_ref[:, cols] * beta
        kg_ref[:, cols] = kb * egc
        qd_ref[:, cols] = q * egc
        kd_ref[:, cols] = k * lane_bcast(kdf_c, DN_HEADS + h)
        gs_ref[h] = lane_bcast(gc_c, DN_HEADS + h)

    hc = DN_HEADS * DN_CHUNK
    ri = lax.broadcasted_iota(jnp.int32, (hc, hc), 0)
    ci = lax.broadcasted_iota(jnp.int32, (hc, hc), 1)
    same_head = (ri // DN_CHUNK) == (ci // DN_CHUNK)
    incl = same_head & (ri >= ci)
    level_masks = [((ri // DN_INV_BASE) == (ci // DN_INV_BASE)) & (ri > ci)]
    size = DN_INV_BASE
    while size < DN_CHUNK:
        level_masks.append(((ri // (2 * size)) == (ci // (2 * size)))
                           & ((ri % (2 * size)) >= size) & ((ci % (2 * size)) < size))
        size *= 2
    eye_f = (ri == ci).astype(F32)

    def stack(ref, rows):
        return jnp.concatenate([ref[rows, h * DN_HEAD_DIM:(h + 1) * DN_HEAD_DIM]
                                for h in range(DN_HEADS)], axis=0)

    heads = range(DN_HEADS)
    last = lambda x, h: x[(h + 1) * DN_CHUNK - 1:(h + 1) * DN_CHUNK, :]
    state = [state_ref[h] for h in heads]
    for first in range(0, n_chunks, DN_GROUP):
        grp = range(first, min(first + DN_GROUP, n_chunks))
        rows = [slice(n * DN_CHUNK, (n + 1) * DN_CHUNK) for n in grp]
        q = [stack(qs_ref, r) for r in rows]
        k = [stack(ks_ref, r) for r in rows]
        kb = [stack(kb_ref, r) for r in rows]
        vb = [stack(vs_ref, r) for r in rows]
        kg = [stack(kg_ref, r) for r in rows]
        q_dec = [stack(qd_ref, r) for r in rows]
        k_dec = [stack(kd_ref, r) for r in rows]
        gc = [jnp.concatenate([gs_ref[h, r, :] for h in heads], axis=0) for r in rows]
        gc_t = [x.T for x in gc]
        decay = [jnp.exp(jnp.where(incl, jnp.concatenate([x, x], axis=1)
                                   - jnp.concatenate([xt, xt], axis=0), NEG_INF))
                 for x, xt in zip(gc, gc_t)]
        a_kk = [_mm_nt(x, y) * d for x, y, d in zip(kb, k, decay)]
        m = [-jnp.where(level_masks[0], a, 0.0) for a in a_kk]
        t_mat = [eye_f + x for x in m]
        for _ in range(2):
            m = [_mm(x, x) for x in m]
            t_mat = [t + _mm(t, x) for t, x in zip(t_mat, m)]
        for mask in level_masks[1:]:
            low = [_mm(t, jnp.where(mask, a, 0.0)) for t, a in zip(t_mat, a_kk)]
            t_mat = [t - _mm(x, t) for t, x in zip(t_mat, low)]
        uw = [_mm(t, jnp.concatenate([x, y], axis=1))
              for t, x, y in zip(t_mat, vb, kg)]
        a_qk = [_mm_nt(x, y) * d for x, y, d in zip(q, k, decay)]
        for i, r in enumerate(rows):
            u = uw[i][:, :DN_HEAD_DIM]
            w = uw[i][:, DN_HEAD_DIM:]
            ws, qs_ = [], []
            for h in heads:
                hr = slice(h * DN_CHUNK, (h + 1) * DN_CHUNK)
                both = _mm(jnp.concatenate([w[hr], q_dec[i][hr]], axis=0), state[h])
                ws.append(both[:DN_CHUNK])
                qs_.append(both[DN_CHUNK:])
            v_new = u - jnp.concatenate(ws, axis=0)
            o = jnp.concatenate(qs_, axis=0) + _mm(a_qk[i], v_new)
            for h in heads:
                hr = slice(h * DN_CHUNK, (h + 1) * DN_CHUNK)
                os_ref[r, h * DN_HEAD_DIM:(h + 1) * DN_HEAD_DIM] = o[hr]
                state[h] = (state[h] * jnp.exp(last(gc[i], h))
                            + _mm_tn(k_dec[i][hr], v_new[hr]))
    for h in heads:
        state_ref[h] = state[h]

    for h in range(DN_HEADS):
        cols = slice(h * DN_HEAD_DIM, (h + 1) * DN_HEAD_DIM)
        zg = zg_ref[:, cols].astype(F32)
        y_ref[:, cols] = (_rms(os_ref[:, cols], nw_ref[...]) * (zg * jax.nn.sigmoid(zg))).astype(ACT)


def _deltanet(z, zs, conv_w, head_params, norm_w, layer, bsz, seq, tc=256):
    nt = seq // tc
    wide = DN_HEADS * DN_HEAD_DIM

    def col(cb):
        return pl.BlockSpec((tc, wide), lambda b, c: (b * nt + c, cb))

    def cw(cb):
        return _layer_spec((DN_CONV, wide), layer, lambda b, c: (0, cb))

    pos = jnp.arange(tc, dtype=jnp.int32)
    tri = ((pos[:, None] // DN_CHUNK == pos[None, :] // DN_CHUNK) & (pos[:, None] >= pos[None, :])).astype(BF16)
    tile = pltpu.VMEM((tc, wide), F32)
    return pl.pallas_call(
        functools.partial(_dn_kernel, tc=tc),
        grid=(bsz, nt),
        in_specs=[
            col(COL_DNQ // wide), col(COL_DNK // wide), col(COL_DNV // wide), col(COL_DNZ // wide),
            pl.BlockSpec((tc, LANES), lambda b, c: (b * nt + c, 0)),
            cw(0), cw(1), cw(2),
            _layer_spec((SUBLANES, LANES), layer),
            _layer_spec((1, DN_HEAD_DIM), layer),
            pl.BlockSpec((tc, tc), lambda b, c: (0, 0)),
        ],
        out_specs=pl.BlockSpec((tc, wide), lambda b, c: (b * nt + c, 0)),
        out_shape=jax.ShapeDtypeStruct((bsz * seq, wide), ACT),
        scratch_shapes=[
            pltpu.VMEM((3, SUBLANES, wide), F32),
            pltpu.VMEM((DN_HEADS, DN_HEAD_DIM, DN_HEAD_DIM), F32),
            tile, tile, tile, tile, tile, tile, tile,
            pltpu.VMEM((DN_HEADS, tc, LANES), F32),
            tile,
        ],
        compiler_params=_cparams(("parallel", "arbitrary")),
        name="deltanet",
    )(z, z, z, z, zs, conv_w, conv_w, conv_w, head_params, norm_w, tri)


def _moba_kernel(q_ref, k_ref, v_ref, d_ref, o_ref, kb_ref, vt_ref, bown_ref, bprev_ref, *, nb):
    blk = MOBA_BLOCK
    half = MOBA_HEAD_DIM
    assert 2 * half == LANES

    kmeans = []
    for j in range(nb):
        kj = k_ref[j * blk:(j + 1) * blk, :]
        kmeans.append(jnp.mean(kj.astype(F32), axis=0, keepdims=True))
        kb_ref[j * blk:(j + 1) * blk, :] = kj.astype(BF16)
        vt_ref[j, 0:LANES, :] = v_ref[j * blk:(j + 1) * blk, :].astype(F32).T.astype(BF16)
        vt_ref[j, LANES:LANES + MOBA_ONES_ROWS, :] = jnp.ones((MOBA_ONES_ROWS, blk), BF16)
    kmean = jnp.concatenate(kmeans, axis=0)

    r = lax.broadcasted_iota(jnp.int32, (LANES, LANES), 0)
    c = lax.broadcasted_iota(jnp.int32, (LANES, LANES), 1)
    for hh in range(2):
        far = jnp.broadcast_to(d_ref[hh, 1:2, :], (LANES, LANES)) * LOG2E
        band = pltpu.roll(jnp.broadcast_to(d_ref[hh, 0:1, :], (LANES, LANES)), 0, 1,
                          stride=1, stride_axis=0) * LOG2E
        diag = jnp.where(c >= r, band, NEG_INF)
        off = jnp.where(c < r, band, far)
        lo = hh * blk
        bown_ref[0:LANES, lo:lo + LANES] = diag
        bown_ref[0:LANES, lo + LANES:lo + blk] = off
        bown_ref[LANES:blk, lo:lo + LANES] = jnp.full((LANES, LANES), NEG_INF, F32)
        bown_ref[LANES:blk, lo + LANES:lo + blk] = diag
        bprev_ref[0:LANES, lo:lo + LANES] = far
        bprev_ref[0:LANES, lo + LANES:lo + blk] = far
        bprev_ref[LANES:blk, lo:lo + LANES] = off
        bprev_ref[LANES:blk, lo + LANES:lo + blk] = far
    far_row = jnp.concatenate([jnp.broadcast_to(d_ref[0, 1:2, :], (1, LANES))] * 2
                              + [jnp.broadcast_to(d_ref[1, 1:2, :], (1, LANES))] * 2, axis=1) * LOG2E

    qt = MOBA_QTILE
    tiles_per_blk = blk // qt
    lane = lax.broadcasted_iota(jnp.int32, (qt, LANES), 1)
    blk_id = lax.broadcasted_iota(jnp.int32, (nb, 2 * qt), 0)
    row = lax.broadcasted_iota(jnp.int32, (LANES, qt), 0)
    far_tile = jnp.concatenate([far_row[:, :qt], far_row[:, blk:blk + qt]], axis=1)

    def tile_cols(ref, part, keys):
        return jnp.concatenate([ref[0:keys, part * qt:(part + 1) * qt],
                                ref[0:keys, blk + part * qt:blk + (part + 1) * qt]], axis=1)

    def score_stage(tile):
        i, part = divmod(tile, tiles_per_blk)
        rows = slice(tile * qt, (tile + 1) * qt)
        qf = q_ref[rows, :].astype(F32) * (MOBA_HEAD_DIM ** -0.5)
        q2 = jnp.concatenate([jnp.where(lane < half, qf, 0.0), jnp.where(lane >= half, qf, 0.0)],
                             axis=0)
        q2b = (q2 * LOG2E).astype(BF16)

        sel = None
        if i > MOBA_TOPK:
            gate = jnp.where(blk_id < i, _dot_nt(kmean, q2, HIGHEST), NEG_INF)
            sel = []
            for j in range(i):
                gj = gate[j:j + 1, :]
                ahead = (gate > gj) | ((gate == gj) & (blk_id < j))
                rank = jnp.sum(ahead.astype(F32), axis=0, keepdims=True)
                sel.append(rank < MOBA_TOPK)

        scores = []
        for j in range(i + 1):
            if j == i:
                keys = (part + 1) * qt
                s = _dot_nt(kb_ref[j * blk:j * blk + keys, :], q2b) + tile_cols(bown_ref, part, keys)
            else:
                s = _dot_nt(kb_ref[j * blk:(j + 1) * blk, :], q2b)
                if j == i - 1:
                    s = s + tile_cols(bprev_ref, part, blk)
                    if sel is not None:
                        s = s + jnp.where(sel[j], 0.0, NEG_INF)
                elif sel is not None:
                    s = s + jnp.where(sel[j], far_tile, NEG_INF)
                else:
                    s = s + far_tile
            scores.append(s)
        return scores

    def exp_stage(scores):
        m = functools.reduce(jnp.maximum, [jnp.max(s, axis=0, keepdims=True) for s in scores])
        return ([jnp.exp2(s - m).astype(BF16) for s in scores],)

    def pv_stage(tile, probs):
        acc = None
        for j, p in enumerate(probs):
            pv = _dot(vt_ref[j, :, 0:p.shape[0]], p)
            acc = pv if acc is None else acc + pv
        out_t = acc[:LANES, :] / acc[LANES:LANES + 1, :]
        out_t = jnp.where(row < half, out_t[:, :qt], out_t[:, qt:])
        o_ref[tile * qt:(tile + 1) * qt, :] = out_t.T.astype(ACT)

    sc_q, ex_q = None, None
    for tile in list(reversed(range(nb * tiles_per_blk))) + [None, None]:
        new_sc = (tile, score_stage(tile)) if tile is not None else None
        new_ex = (sc_q[0],) + exp_stage(sc_q[1]) if sc_q is not None else None
        if ex_q is not None:
            pv_stage(*ex_q)
        sc_q, ex_q = new_sc, new_ex


def _moba(z, zqk, dist_bias, bsz, seq):
    nb = seq // MOBA_BLOCK
    pairs = MOBA_HEADS // 2
    kcb = (COL_MK - COL_MQ) // LANES
    vcb = COL_MV // LANES
    return pl.pallas_call(
        functools.partial(_moba_kernel, nb=nb),
        grid=(bsz, pairs),
        in_specs=[
            pl.BlockSpec((seq, LANES), lambda b, p: (b, p)),
            pl.BlockSpec((seq, LANES), lambda b, p: (b, kcb + p)),
            pl.BlockSpec((seq, LANES), lambda b, p: (b, vcb + p)),
            pl.BlockSpec((2, SUBLANES, LANES), lambda b, p: (p, 0, 0)),
        ],
        out_specs=pl.BlockSpec((seq, LANES), lambda b, p: (b, p)),
        out_shape=jax.ShapeDtypeStruct((bsz * seq, MOBA_HEADS * MOBA_HEAD_DIM), ACT),
        scratch_shapes=[
            pltpu.VMEM((seq, LANES), BF16),
            pltpu.VMEM((nb, LANES + MOBA_ONES_ROWS, MOBA_BLOCK), BF16),
            pltpu.VMEM((MOBA_BLOCK, 2 * MOBA_BLOCK), F32),
            pltpu.VMEM((MOBA_BLOCK, 2 * MOBA_BLOCK), F32),
        ],
        compiler_params=_cparams(("parallel", "parallel")),
        name="moba",
    )(zqk, zqk, z, dist_bias)


def _t5_bucket(rel):
    n = jnp.maximum(-rel, 0)
    exact = REL_BUCKETS // 2
    nf = jnp.maximum(n, 1).astype(F32)
    large = exact + (jnp.log(nf / exact) / math.log(REL_MAX_DIST / exact)
                     * (REL_BUCKETS - exact)).astype(jnp.int32)
    large = jnp.minimum(large, REL_BUCKETS - 1)
    return jnp.where(n < exact, n, large)


def _moba_dist_bias(rel_bias):
    bias_ht = rel_bias.T.astype(F32)
    near = bias_ht[:, _t5_bucket(-jnp.arange(LANES, dtype=jnp.int32))]
    far = bias_ht[:, _t5_bucket(jnp.full((LANES,), -REL_MAX_DIST, jnp.int32))]
    rows = jnp.zeros((MOBA_HEADS, SUBLANES, LANES), F32)
    return rows.at[:, 0, :].set(near).at[:, 1, :].set(far)


def _merge_kernel(u_ref, up_ref, pw_ref, ps_ref, yb_ref, yc_ref, ga_ref, gb_ref, gc_ref, x_ref, wb_ref,
                  wo_ref, o_ref, *, tm):
    y_a = _pool_mixer(u_ref, up_ref, pw_ref, ps_ref, pl.program_id(1), tm)
    merged = None
    for n, (y, g_ref) in enumerate(((y_a, ga_ref), (yb_ref[...], gb_ref), (yc_ref[...], gc_ref))):
        proj = _dot(y, wb_ref[n])
        gate = jax.nn.sigmoid(g_ref[...].astype(F32))
        term = gate * proj
        merged = term if merged is None else merged + term
    o_ref[...] = x_ref[...] + _dot(merged.astype(BF16), wo_ref[...])


def _merge(yb, yc, z, x, pool_w, pool_scale, w_branch, w_out, layer, bsz, seq, tm=512):
    nt = seq // tm
    hb = tm // POOL_HALO
    gcb = COL_GATE // D_MODEL
    row = lambda b, j: (b * nt + j, 0)
    return pl.pallas_call(
        functools.partial(_merge_kernel, tm=tm),
        grid=(bsz, nt),
        in_specs=[
            pl.BlockSpec((tm, BRANCH_WIDTH), row),
            pl.BlockSpec((POOL_HALO, BRANCH_WIDTH),
                         lambda b, j: (jnp.maximum((b * nt + j) * hb - 1, 0), 0)),
            _layer_spec((4, POOL_GC, POOL_GC), layer),
            _layer_spec((1, BRANCH_WIDTH), layer),
            pl.BlockSpec((tm, BRANCH_WIDTH), row),
            pl.BlockSpec((tm, BRANCH_WIDTH), row),
            pl.BlockSpec((tm, D_MODEL), lambda b, j: (b * nt + j, gcb)),
            pl.BlockSpec((tm, D_MODEL), lambda b, j: (b * nt + j, gcb + 1)),
            pl.BlockSpec((tm, D_MODEL), lambda b, j: (b * nt + j, gcb + 2)),
            pl.BlockSpec((tm, D_MODEL), row),
            _layer_spec((3, BRANCH_WIDTH, D_MODEL), layer),
            _layer_spec((D_MODEL, D_MODEL), layer),
        ],
        out_specs=pl.BlockSpec((tm, D_MODEL), row),
        out_shape=jax.ShapeDtypeStruct((bsz * seq, D_MODEL), F32),
        compiler_params=_cparams(("parallel", "parallel")),
        name="merge",
    )(z, z, pool_w, pool_scale, yb, yc, z, z, z, x, w_branch, w_out)


def _ffn_ple_kernel(x_ref, nw_ref, wu_ref, cw_ref, wd_ref, p_ref, pnw_ref, wg_ref, wp_ref, fw_ref, o_ref,
                    act_ref, carry_ref, pad_ref, *, tm, fc, final):
    j = pl.program_id(1)
    x = x_ref[...]
    h = _rms(x, nw_ref[...]).astype(BF16)
    for f in range(FFN_DIM // fc):
        conv = []
        for part in range(2):
            cols = slice(part * FFN_DIM + f * fc, part * FFN_DIM + (f + 1) * fc)
            buf = 2 * (f % 2) + part
            up = _dot(h, wu_ref[:, cols])
            pad_ref[buf, 0:SUBLANES, :] = jnp.where(j == 0, 0.0, carry_ref[:, cols])
            pad_ref[buf, SUBLANES:SUBLANES + tm, :] = up
            carry_ref[:, cols] = up[tm - SUBLANES:tm, :]
            y = cw_ref[FFN_CONV - 1:FFN_CONV, cols] * up
            for tap in range(FFN_CONV - 1):
                off = SUBLANES - (FFN_CONV - 1) + tap
                y = y + cw_ref[tap:tap + 1, cols] * pad_ref[buf, off:off + tm, :]
            conv.append(y)
        act_ref[:, f * fc:(f + 1) * fc] = (jax.nn.gelu(conv[0], approximate=True) * conv[1]).astype(BF16)
    x = x + _dot(act_ref[...], wd_ref[...])
    gate = jax.nn.sigmoid(_dot(_rms(x, pnw_ref[...]).astype(BF16), wg_ref[...]))
    x = x + _dot(p_ref[...].astype(BF16), wp_ref[...]) * gate
    if final:
        x = _rms(x, fw_ref[...])
    o_ref[...] = x


def _ffn_ple(x, nw, w_up, w_conv, w_down, p, ple_nw, w_gate, w_proj, final_w, layer, final, bsz, seq,
             tm=256, fc=256):
    nt = seq // tm
    resident = dict(pipeline_mode=pl.Buffered(1))
    row = lambda b, j: (b * nt + j, 0)
    return pl.pallas_call(
        functools.partial(_ffn_ple_kernel, tm=tm, fc=fc, final=final),
        grid=(bsz, nt),
        in_specs=[
            pl.BlockSpec((tm, D_MODEL), row),
            _layer_spec((1, D_MODEL), layer),
            _layer_spec((D_MODEL, 2 * FFN_DIM), layer, **resident),
            _layer_spec((FFN_CONV, 2 * FFN_DIM), layer),
            _layer_spec((FFN_DIM, D_MODEL), layer, **resident),
            _layer_spec((tm, PLE_DIM), layer, row),
            _layer_spec((1, D_MODEL), layer),
            _layer_spec((D_MODEL, D_MODEL), layer, **resident),
            _layer_spec((PLE_DIM, D_MODEL), layer, **resident),
            pl.BlockSpec((1, D_MODEL), lambda b, j: (0, 0)),
        ],
        out_specs=pl.BlockSpec((tm, D_MODEL), row),
        out_shape=jax.ShapeDtypeStruct((bsz * seq, D_MODEL), F32),
        scratch_shapes=[
            pltpu.VMEM((tm, FFN_DIM), BF16),
            pltpu.VMEM((SUBLANES, 2 * FFN_DIM), F32),
            pltpu.VMEM((4, SUBLANES + tm, fc), F32),
        ],
        compiler_params=_cparams(("parallel", "arbitrary")),
        name="ffn_ple",
    )(x, nw, w_up, w_conv, w_down, p, ple_nw, w_gate, w_proj, final_w)


def _split_w_in(w):
    pool = w[..., 0:512]
    dn_qkv = w[..., 512:2048]
    dn_z = w[..., 2048:2560]
    dn_ba = w[..., 2560:2568]
    mb_qk = w[..., 2568:3592]
    mb_v = w[..., 3592:4104]
    gates = w[..., 4104:7176]
    main = jnp.concatenate([pool, dn_qkv, dn_z, mb_v, mb_qk, gates], axis=-1).astype(BF16)
    small = jnp.pad(dn_ba, ((0, 0),) * (w.ndim - 1) + ((0, LANES - 2 * DN_HEADS),)).astype(BF16)
    return main, small


def kernel(x, p, rel_bias, norm_mix, w_in, pool_w, pool_scale, dn_conv, dn_a_log, dn_dt_bias,
           dn_norm, w_branch, w_out, norm_ffn, ffn_up, ffn_conv, ffn_down, norm_ple, ple_gate,
           ple_proj, norm_final):
    bsz, seq, d = x.shape
    depth = w_in.shape[0]
    t = bsz * seq
    xt = x.reshape(t, d)
    row = lambda a: a[:, None, :]
    w_main, w_small = _split_w_in(w_in)
    pool_w, w_branch, w_out = pool_w.astype(BF16), w_branch.astype(BF16), w_out.astype(BF16)
    ffn_up, ffn_down = ffn_up.astype(BF16), ffn_down.astype(BF16)
    ple_gate, ple_proj = ple_gate.astype(BF16), ple_proj.astype(BF16)
    p = p.reshape(depth, t, PLE_DIM)
    dist_bias = _moba_dist_bias(rel_bias)
    head_params = jnp.zeros((depth, SUBLANES, LANES), F32)
    head_params = head_params.at[:, 0, DN_HEADS:2 * DN_HEADS].set(dn_a_log)
    head_params = head_params.at[:, 1, DN_HEADS:2 * DN_HEADS].set(dn_dt_bias)
    for i in range(depth):
        z, zs, zqk = _in_proj(xt, row(norm_mix), w_main, w_small, i)
        y_b = _deltanet(z, zs, dn_conv, head_params, row(dn_norm), i, bsz, seq)
        y_c = _moba(z, zqk, dist_bias, bsz, seq)
        xt = _merge(y_b, y_c, z, xt, pool_w, row(pool_scale), w_branch, w_out, i, bsz, seq)
        xt = _ffn_ple(xt, row(norm_ffn), ffn_up, ffn_conv, ffn_down, p, row(norm_ple), ple_gate, ple_proj,
                      norm_final[None, :], i, i == depth - 1, bsz, seq)
    return xt.reshape(bsz, seq, d)
```

```python
import functools
import math

import jax
import jax.numpy as jnp
from jax import lax
from jax.experimental import pallas as pl
from jax.experimental.pallas import tpu as pltpu

F32 = jnp.float32
BF16 = jnp.bfloat16
HIGHEST = lax.Precision.HIGHEST
ACT = jnp.bfloat16

D_MODEL = 1024
BRANCH_WIDTH = 512
POOL_WINDOWS = (2, 4, 8, 16)
POOL_GC = 128
POOL_HALO = 16
DN_HEADS = 4
DN_HEAD_DIM = 128
DN_CONV = 4
DN_CHUNK = 64
MOBA_HEADS = 8
MOBA_HEAD_DIM = 64
MOBA_BLOCK = 256
MOBA_TOPK = 3
MOBA_ONES_ROWS = 16
MOBA_QTILE = 128
REL_BUCKETS = 32
REL_MAX_DIST = 128
FFN_DIM = 2816
FFN_CONV = 3
PLE_DIM = 256
NORM_EPS = 1e-6
NEG_INF = -1e30
LOG2E = math.log2(math.e)
LANES = 128
SUBLANES = 8

COL_POOL = 0
COL_DNQ = 512
COL_DNK = 1024
COL_DNV = 1536
COL_DNZ = 2048
COL_MV = 2560
COL_MQ = 3072
COL_MK = 3584
COL_GATE = 4096
Z_COLS = 7168

VMEM_LIMIT = 56 * 1024 * 1024


def _cparams(sem):
    return pltpu.CompilerParams(dimension_semantics=sem, vmem_limit_bytes=VMEM_LIMIT)


def _layer_spec(block, layer, index=None, **kwargs):
    def index_map(*grid):
        rest = index(*grid) if index is not None else (0,) * len(block)
        return (layer,) + tuple(rest)
    return pl.BlockSpec((None,) + tuple(block), index_map, **kwargs)


def _dot(a, b, precision=None):
    return lax.dot_general(a, b, (((1,), (0,)), ((), ())), precision=precision,
                           preferred_element_type=F32)


def _dot_nt(a, b, precision=None):
    return lax.dot_general(a, b, (((1,), (1,)), ((), ())), precision=precision,
                           preferred_element_type=F32)


def _split3(x):
    hi = x.astype(BF16)
    rest = x - hi.astype(F32)
    mid = rest.astype(BF16)
    lo = (rest - mid.astype(F32)).astype(BF16)
    return hi, mid, lo


def _mm(a, b):
    return _dot(a.astype(BF16), b.astype(BF16))


def _mm_nt(a, b):
    return _dot_nt(a.astype(BF16), b.astype(BF16))


def _mm_tn(a, b):
    return lax.dot_general(a.astype(BF16), b.astype(BF16), (((0,), (0,)), ((), ())),
                           preferred_element_type=F32)


DN_INV_BASE = 8
DN_GROUP = 4


def _rms(x, w):
    return x * lax.rsqrt(jnp.mean(x * x, axis=-1, keepdims=True) + NORM_EPS) * w


def _in_proj_kernel(x_ref, nw_ref, w_ref, ws_ref, z_ref, zs_ref, zqk_ref, *, tn):
    hb = _rms(x_ref[...], nw_ref[...]).astype(BF16)
    zs_ref[...] = _dot(hb, ws_ref[...])
    for jb in range(Z_COLS // tn):
        cols = slice(jb * tn, (jb + 1) * tn)
        z = _dot(hb, w_ref[:, cols])
        z_ref[:, cols] = z.astype(ACT)
        if jb == COL_MQ // tn:
            zqk_ref[...] = z


def _in_proj(x, nw, w_main, w_small, layer, tm=512, tn=1024):
    t = x.shape[0]
    assert COL_MQ % tn == 0 and COL_MK == COL_MQ + tn // 2
    resident = dict(pipeline_mode=pl.Buffered(1))
    return pl.pallas_call(
        functools.partial(_in_proj_kernel, tn=tn),
        grid=(t // tm,),
        in_specs=[
            pl.BlockSpec((tm, D_MODEL), lambda i: (i, 0)),
            _layer_spec((1, D_MODEL), layer),
            _layer_spec((D_MODEL, Z_COLS), layer, **resident),
            _layer_spec((D_MODEL, LANES), layer, **resident),
        ],
        out_specs=[
            pl.BlockSpec((tm, Z_COLS), lambda i: (i, 0)),
            pl.BlockSpec((tm, LANES), lambda i: (i, 0)),
            pl.BlockSpec((tm, tn), lambda i: (i, 0)),
        ],
        out_shape=[jax.ShapeDtypeStruct((t, Z_COLS), ACT),
                   jax.ShapeDtypeStruct((t, LANES), F32),
                   jax.ShapeDtypeStruct((t, tn), F32)],
        compiler_params=_cparams(("parallel",)),
        name="in_proj",
    )(x, nw, w_main, w_small)


def _pool_mixer(u_ref, up_ref, w_ref, sc_ref, j, tp):
    halo = jnp.where(j == 0, 0.0, up_ref[...].astype(F32))
    t = j * tp + lax.broadcasted_iota(jnp.int32, (tp, POOL_GC), 0)
    out = []
    for g, win in enumerate(POOL_WINDOWS):
        cols = slice(g * POOL_GC, (g + 1) * POOL_GC)
        u = u_ref[:, cols].astype(F32)
        tot = jnp.concatenate([halo[:, cols], u], axis=0)
        width = 1
        while width < win:
            tot = tot + pltpu.roll(tot, width, 0)
            width *= 2
        cnt = jnp.minimum(t + 1, win).astype(F32)
        mixed = tot[POOL_HALO:, :] / cnt - u
        y = _dot(mixed.astype(BF16), w_ref[g])
        out.append((y * sc_ref[:, cols]).astype(BF16))
    return jnp.concatenate(out, axis=1)


def _dn_kernel(q_ref, k_ref, v_ref, zg_ref, zs_ref, cwq_ref, cwk_ref, cwv_ref, hp_ref, nw_ref, tri_ref,
               y_ref, halo_ref, state_ref, qs_ref, ks_ref, vs_ref, kb_ref, kg_ref, qd_ref,
               kd_ref, gs_ref, os_ref, *, tc):
    c = pl.program_id(1)

    @pl.when(c == 0)
    def _():
        halo_ref[...] = jnp.zeros_like(halo_ref)
        state_ref[...] = jnp.zeros_like(state_ref)

    for part, (src, cw, dst) in enumerate(((q_ref, cwq_ref, qs_ref), (k_ref, cwk_ref, ks_ref),
                                          (v_ref, cwv_ref, vs_ref))):
        x = src[...].astype(F32)
        ext = jnp.concatenate([halo_ref[part], x], axis=0)
        halo_ref[part] = x[tc - SUBLANES:tc, :]
        acc = cw[0:1, :] * ext
        for tap in range(1, DN_CONV):
            acc = cw[tap:tap + 1, :] * ext + pltpu.roll(acc, 1, 0)
        conv = acc[SUBLANES:, :]
        dst[...] = conv * jax.nn.sigmoid(conv)

    n_chunks = tc // DN_CHUNK
    zs = zs_ref[...]
    beta_c = jax.nn.sigmoid(zs)
    g_c = -jnp.exp(hp_ref[0:1, :]) * jnp.logaddexp(zs + hp_ref[1:2, :], 0.0)
    gc_c = sum(_dot(tri_ref[...], part) for part in _split3(g_c))
    g_last_c = jnp.concatenate(
        [jnp.broadcast_to(gc_c[(n + 1) * DN_CHUNK - 1:(n + 1) * DN_CHUNK, :], (DN_CHUNK, LANES))
         for n in range(n_chunks)], axis=0)
    egc_c = jnp.exp(gc_c)
    kdf_c = jnp.exp(g_last_c - gc_c)

    for h in range(DN_HEADS):
        cols = slice(h * DN_HEAD_DIM, (h + 1) * DN_HEAD_DIM)
        lane_bcast = lambda x, lane: jnp.broadcast_to(x[:, lane:lane + 1], (tc, LANES))
        beta = lane_bcast(beta_c, h)
        egc = lane_bcast(egc_c, DN_HEADS + h)
        q = qs_ref[:, cols]
        k = ks_ref[:, cols]
        q = q * lax.rsqrt(jnp.sum(q * q, axis=-1, keepdims=True) + NORM_EPS) * (DN_HEAD_DIM ** -0.5)
        k = k * lax.rsqrt(jnp.sum(k * k, axis=-1, keepdims=True) + NORM_EPS)
        kb = k * beta
        qs_ref[:, cols] = q
        ks_ref[:, cols] = k
        kb_ref[:, cols] = kb
        vs_ref[:, cols] = vs_ref[:, cols] * beta
        kg_ref[:, cols] = kb * egc
        qd_ref[:, cols] = q * egc
        kd_ref[:, cols] = k * lane_bcast(kdf_c, DN_HEADS + h)
        gs_ref[h] = lane_bcast(gc_c, DN_HEADS + h)

    hc = DN_HEADS * DN_CHUNK
    ri = lax.broadcasted_iota(jnp.int32, (hc, hc), 0)
    ci = lax.broadcasted_iota(jnp.int32, (hc, hc), 1)
    same_head = (ri // DN_CHUNK) == (ci // DN_CHUNK)
    incl = same_head & (ri >= ci)
    level_masks = [((ri // DN_INV_BASE) == (ci // DN_INV_BASE)) & (ri > ci)]
    size = DN_INV_BASE
    while size < DN_CHUNK:
        level_masks.append(((ri // (2 * size)) == (ci // (2 * size)))
                           & ((ri % (2 * size)) >= size) & ((ci % (2 * size)) < size))
        size *= 2
    eye_f = (ri == ci).astype(F32)

    def stack(ref, rows):
        return jnp.concatenate([ref[rows, h * DN_HEAD_DIM:(h + 1) * DN_HEAD_DIM]
                                for h in range(DN_HEADS)], axis=0)

    heads = range(DN_HEADS)
    last = lambda x, h: x[(h + 1) * DN_CHUNK - 1:(h + 1) * DN_CHUNK, :]
    state = [state_ref[h] for h in heads]
    for first in range(0, n_chunks, DN_GROUP):
        grp = range(first, min(first + DN_GROUP, n_chunks))
        rows = [slice(n * DN_CHUNK, (n + 1) * DN_CHUNK) for n in grp]
        q = [stack(qs_ref, r) for r in rows]
        k = [stack(ks_ref, r) for r in rows]
        kb = [stack(kb_ref, r) for r in rows]
        vb = [stack(vs_ref, r) for r in rows]
        kg = [stack(kg_ref, r) for r in rows]
        q_dec = [stack(qd_ref, r) for r in rows]
        k_dec = [stack(kd_ref, r) for r in rows]
        gc = [jnp.concatenate([gs_ref[h, r, :] for h in heads], axis=0) for r in rows]
        gc_t = [x.T for x in gc]
        decay = [jnp.exp(jnp.where(incl, jnp.concatenate([x, x], axis=1)
                                   - jnp.concatenate([xt, xt], axis=0), NEG_INF))
                 for x, xt in zip(gc, gc_t)]
        a_kk = [_mm_nt(x, y) * d for x, y, d in zip(kb, k, decay)]
        m = [-jnp.where(level_masks[0], a, 0.0) for a in a_kk]
        t_mat = [eye_f + x for x in m]
        for _ in range(2):
            m = [_mm(x, x) for x in m]
            t_mat = [t + _mm(t, x) for t, x in zip(t_mat, m)]
        for mask in level_masks[1:]:
            low = [_mm(t, jnp.where(mask, a, 0.0)) for t, a in zip(t_mat, a_kk)]
            t_mat = [t - _mm(x, t) for t, x in zip(t_mat, low)]
        uw = [_mm(t, jnp.concatenate([x, y], axis=1))
              for t, x, y in zip(t_mat, vb, kg)]
        a_qk = [_mm_nt(x, y) * d for x, y, d in zip(q, k, decay)]
        for i, r in enumerate(rows):
            u = uw[i][:, :DN_HEAD_DIM]
            w = uw[i][:, DN_HEAD_DIM:]
            ws, qs_ = [], []
            for h in heads:
                hr = slice(h * DN_CHUNK, (h + 1) * DN_CHUNK)
                both = _mm(jnp.concatenate([w[hr], q_dec[i][hr]], axis=0), state[h])
                ws.append(both[:DN_CHUNK])
                qs_.append(both[DN_CHUNK:])
            v_new = u - jnp.concatenate(ws, axis=0)
            o = jnp.concatenate(qs_, axis=0) + _mm(a_qk[i], v_new)
            for h in heads:
                hr = slice(h * DN_CHUNK, (h + 1) * DN_CHUNK)
                os_ref[r, h * DN_HEAD_DIM:(h + 1) * DN_HEAD_DIM] = o[hr]
                state[h] = (state[h] * jnp.exp(last(gc[i], h))
                            + _mm_tn(k_dec[i][hr], v_new[hr]))
    for h in heads:
        state_ref[h] = state[h]

    for h in range(DN_HEADS):
        cols = slice(h * DN_HEAD_DIM, (h + 1) * DN_HEAD_DIM)
        zg = zg_ref[:, cols].astype(F32)
        y_ref[:, cols] = (_rms(os_ref[:, cols], nw_ref[...]) * (zg * jax.nn.sigmoid(zg))).astype(ACT)


def _deltanet(z, zs, conv_w, head_params, norm_w, layer, bsz, seq, tc=256):
    nt = seq // tc
    wide = DN_HEADS * DN_HEAD_DIM

    def col(cb):
        return pl.BlockSpec((tc, wide), lambda b, c: (b * nt + c, cb))

    def cw(cb):
        return _layer_spec((DN_CONV, wide), layer, lambda b, c: (0, cb))

    pos = jnp.arange(tc, dtype=jnp.int32)
    tri = ((pos[:, None] // DN_CHUNK == pos[None, :] // DN_CHUNK) & (pos[:, None] >= pos[None, :])).astype(BF16)
    tile = pltpu.VMEM((tc, wide), F32)
    return pl.pallas_call(
        functools.partial(_dn_kernel, tc=tc),
        grid=(bsz, nt),
        in_specs=[
            col(COL_DNQ // wide), col(COL_DNK // wide), col(COL_DNV // wide), col(COL_DNZ // wide),
            pl.BlockSpec((tc, LANES), lambda b, c: (b * nt + c, 0)),
            cw(0), cw(1), cw(2),
            _layer_spec((SUBLANES, LANES), layer),
            _layer_spec((1, DN_HEAD_DIM), layer),
            pl.BlockSpec((tc, tc), lambda b, c: (0, 0)),
        ],
        out_specs=pl.BlockSpec((tc, wide), lambda b, c: (b * nt + c, 0)),
        out_shape=jax.ShapeDtypeStruct((bsz * seq, wide), ACT),
        scratch_shapes=[
            pltpu.VMEM((3, SUBLANES, wide), F32),
            pltpu.VMEM((DN_HEADS, DN_HEAD_DIM, DN_HEAD_DIM), F32),
            tile, tile, tile, tile, tile, tile, tile,
            pltpu.VMEM((DN_HEADS, tc, LANES), F32),
            tile,
        ],
        compiler_params=_cparams(("parallel", "arbitrary")),
        name="deltanet",
    )(z, z, z, z, zs, conv_w, conv_w, conv_w, head_params, norm_w, tri)


def _moba_kernel(q_ref, k_ref, v_ref, d_ref, o_ref, kb_ref, vt_ref, bown_ref, bprev_ref, *, nb):
    blk = MOBA_BLOCK
    half = MOBA_HEAD_DIM
    assert 2 * half == LANES

    kmeans = []
    for j in range(nb):
        kj = k_ref[j * blk:(j + 1) * blk, :]
        kmeans.append(jnp.mean(kj.astype(F32), axis=0, keepdims=True))
        kb_ref[j * blk:(j + 1) * blk, :] = kj.astype(BF16)
        vt_ref[j, 0:LANES, :] = v_ref[j * blk:(j + 1) * blk, :].astype(F32).T.astype(BF16)
        vt_ref[j, LANES:LANES + MOBA_ONES_ROWS, :] = jnp.ones((MOBA_ONES_ROWS, blk), BF16)
    kmean = jnp.concatenate(kmeans, axis=0)

    r = lax.broadcasted_iota(jnp.int32, (LANES, LANES), 0)
    c = lax.broadcasted_iota(jnp.int32, (LANES, LANES), 1)
    for hh in range(2):
        far = jnp.broadcast_to(d_ref[hh, 1:2, :], (LANES, LANES)) * LOG2E
        band = pltpu.roll(jnp.broadcast_to(d_ref[hh, 0:1, :], (LANES, LANES)), 0, 1,
                          stride=1, stride_axis=0) * LOG2E
        diag = jnp.where(c >= r, band, NEG_INF)
        off = jnp.where(c < r, band, far)
        lo = hh * blk
        bown_ref[0:LANES, lo:lo + LANES] = diag
        bown_ref[0:LANES, lo + LANES:lo + blk] = off
        bown_ref[LANES:blk, lo:lo + LANES] = jnp.full((LANES, LANES), NEG_INF, F32)
        bown_ref[LANES:blk, lo + LANES:lo + blk] = diag
        bprev_ref[0:LANES, lo:lo + LANES] = far
        bprev_ref[0:LANES, lo + LANES:lo + blk] = far
        bprev_ref[LANES:blk, lo:lo + LANES] = off
        bprev_ref[LANES:blk, lo + LANES:lo + blk] = far
    far_row = jnp.concatenate([jnp.broadcast_to(d_ref[0, 1:2, :], (1, LANES))] * 2
                              + [jnp.broadcast_to(d_ref[1, 1:2, :], (1, LANES))] * 2, axis=1) * LOG2E

    qt = MOBA_QTILE
    tiles_per_blk = blk // qt
    lane = lax.broadcasted_iota(jnp.int32, (qt, LANES), 1)
    blk_id = lax.broadcasted_iota(jnp.int32, (nb, 2 * qt), 0)
    row = lax.broadcasted_iota(jnp.int32, (LANES, qt), 0)
    far_tile = jnp.concatenate([far_row[:, :qt], far_row[:, blk:blk + qt]], axis=1)

    def tile_cols(ref, part, keys):
        return jnp.concatenate([ref[0:keys, part * qt:(part + 1) * qt],
                                ref[0:keys, blk + part * qt:blk + (part + 1) * qt]], axis=1)

    def score_stage(tile):
        i, part = divmod(tile, tiles_per_blk)
        rows = slice(tile * qt, (tile + 1) * qt)
        qf = q_ref[rows, :].astype(F32) * (MOBA_HEAD_DIM ** -0.5)
        q2 = jnp.concatenate([jnp.where(lane < half, qf, 0.0), jnp.where(lane >= half, qf, 0.0)],
                             axis=0)
        q2b = (q2 * LOG2E).astype(BF16)

        sel = None
        if i > MOBA_TOPK:
            gate = jnp.where(blk_id < i, _dot_nt(kmean, q2, HIGHEST), NEG_INF)
            sel = []
            for j in range(i):
                gj = gate[j:j + 1, :]
                ahead = (gate > gj) | ((gate == gj) & (blk_id < j))
                rank = jnp.sum(ahead.astype(F32), axis=0, keepdims=True)
                sel.append(rank < MOBA_TOPK)

        scores = []
        for j in range(i + 1):
            if j == i:
                keys = (part + 1) * qt
                s = _dot_nt(kb_ref[j * blk:j * blk + keys, :], q2b) + tile_cols(bown_ref, part, keys)
            else:
                s = _dot_nt(kb_ref[j * blk:(j + 1) * blk, :], q2b)
                if j == i - 1:
                    s = s + tile_cols(bprev_ref, part, blk)
                    if sel is not None:
                        s = s + jnp.where(sel[j], 0.0, NEG_INF)
                elif sel is not None:
                    s = s + jnp.where(sel[j], far_tile, NEG_INF)
                else:
                    s = s + far_tile
            scores.append(s)
        return scores

    def exp_stage(scores):
        m = functools.reduce(jnp.maximum, [jnp.max(s, axis=0, keepdims=True) for s in scores])
        return ([jnp.exp2(s - m).astype(BF16) for s in scores],)

    def pv_stage(tile, probs):
        acc = None
        for j, p in enumerate(probs):
            pv = _dot(vt_ref[j, :, 0:p.shape[0]], p)
            acc = pv if acc is None else acc + pv
        out_t = acc[:LANES, :] / acc[LANES:LANES + 1, :]
        out_t = jnp.where(row < half, out_t[:, :qt], out_t[:, qt:])
        o_ref[tile * qt:(tile + 1) * qt, :] = out_t.T.astype(ACT)

    sc_q, ex_q = None, None
    for tile in list(reversed(range(nb * tiles_per_blk))) + [None, None]:
        new_sc = (tile, score_stage(tile)) if tile is not None else None
        new_ex = (sc_q[0],) + exp_stage(sc_q[1]) if sc_q is not None else None
        if ex_q is not None:
            pv_stage(*ex_q)
        sc_q, ex_q = new_sc, new_ex


def _moba(z, zqk, dist_bias, bsz, seq):
    nb = seq // MOBA_BLOCK
    pairs = MOBA_HEADS // 2
    kcb = (COL_MK - COL_MQ) // LANES
    vcb = COL_MV // LANES
    return pl.pallas_call(
        functools.partial(_moba_kernel, nb=nb),
        grid=(bsz, pairs),
        in_specs=[
            pl.BlockSpec((seq, LANES), lambda b, p: (b, p)),
            pl.BlockSpec((seq, LANES), lambda b, p: (b, kcb + p)),
            pl.BlockSpec((seq, LANES), lambda b, p: (b, vcb + p)),
            pl.BlockSpec((2, SUBLANES, LANES), lambda b, p: (p, 0, 0)),
        ],
        out_specs=pl.BlockSpec((seq, LANES), lambda b, p: (b, p)),
        out_shape=jax.ShapeDtypeStruct((bsz * seq, MOBA_HEADS * MOBA_HEAD_DIM), ACT),
        scratch_shapes=[
            pltpu.VMEM((seq, LANES), BF16),
            pltpu.VMEM((nb, LANES + MOBA_ONES_ROWS, MOBA_BLOCK), BF16),
            pltpu.VMEM((MOBA_BLOCK, 2 * MOBA_BLOCK), F32),
            pltpu.VMEM((MOBA_BLOCK, 2 * MOBA_BLOCK), F32),
        ],
        compiler_params=_cparams(("parallel", "parallel")),
        name="moba",
    )(zqk, zqk, z, dist_bias)


def _t5_bucket(rel):
    n = jnp.maximum(-rel, 0)
    exact = REL_BUCKETS // 2
    nf = jnp.maximum(n, 1).astype(F32)
    large = exact + (jnp.log(nf / exact) / math.log(REL_MAX_DIST / exact)
                     * (REL_BUCKETS - exact)).astype(jnp.int32)
    large = jnp.minimum(large, REL_BUCKETS - 1)
    return jnp.where(n < exact, n, large)


def _moba_dist_bias(rel_bias):
    bias_ht = rel_bias.T.astype(F32)
    near = bias_ht[:, _t5_bucket(-jnp.arange(LANES, dtype=jnp.int32))]
    far = bias_ht[:, _t5_bucket(jnp.full((LANES,), -REL_MAX_DIST, jnp.int32))]
    rows = jnp.zeros((MOBA_HEADS, SUBLANES, LANES), F32)
    return rows.at[:, 0, :].set(near).at[:, 1, :].set(far)


def _merge_kernel(u_ref, up_ref, pw_ref, ps_ref, yb_ref, yc_ref, ga_ref, gb_ref, gc_ref, x_ref, wb_ref,
                  wo_ref, o_ref, *, tm):
    y_a = _pool_mixer(u_ref, up_ref, pw_ref, ps_ref, pl.program_id(1), tm)
    merged = None
    for n, (y, g_ref) in enumerate(((y_a, ga_ref), (yb_ref[...], gb_ref), (yc_ref[...], gc_ref))):
        proj = _dot(y, wb_ref[n])
        gate = jax.nn.sigmoid(g_ref[...].astype(F32))
        term = gate * proj
        merged = term if merged is None else merged + term
    o_ref[...] = x_ref[...] + _dot(merged.astype(BF16), wo_ref[...])


def _merge(yb, yc, z, x, pool_w, pool_scale, w_branch, w_out, layer, bsz, seq, tm=1024):
    nt = seq // tm
    hb = tm // POOL_HALO
    gcb = COL_GATE // D_MODEL
    row = lambda b, j: (b * nt + j, 0)
    return pl.pallas_call(
        functools.partial(_merge_kernel, tm=tm),
        grid=(bsz, nt),
        in_specs=[
            pl.BlockSpec((tm, BRANCH_WIDTH), row),
            pl.BlockSpec((POOL_HALO, BRANCH_WIDTH),
                         lambda b, j: (jnp.maximum((b * nt + j) * hb - 1, 0), 0)),
            _layer_spec((4, POOL_GC, POOL_GC), layer),
            _layer_spec((1, BRANCH_WIDTH), layer),
            pl.BlockSpec((tm, BRANCH_WIDTH), row),
            pl.BlockSpec((tm, BRANCH_WIDTH), row),
            pl.BlockSpec((tm, D_MODEL), lambda b, j: (b * nt + j, gcb)),
            pl.BlockSpec((tm, D_MODEL), lambda b, j: (b * nt + j, gcb + 1)),
            pl.BlockSpec((tm, D_MODEL), lambda b, j: (b * nt + j, gcb + 2)),
            pl.BlockSpec((tm, D_MODEL), row),
            _layer_spec((3, BRANCH_WIDTH, D_MODEL), layer),
            _layer_spec((D_MODEL, D_MODEL), layer),
        ],
        out_specs=pl.BlockSpec((tm, D_MODEL), row),
        out_shape=jax.ShapeDtypeStruct((bsz * seq, D_MODEL), F32),
        compiler_params=_cparams(("parallel", "parallel")),
        name="merge",
    )(z, z, pool_w, pool_scale, yb, yc, z, z, z, x, w_branch, w_out)


def _ffn_ple_kernel(x_ref, nw_ref, wu_ref, cw_ref, wd_ref, p_ref, pnw_ref, wg_ref, wp_ref, fw_ref, o_ref,
                    act_ref, carry_ref, pad_ref, *, tm, fc, final):
    j = pl.program_id(1)
    x = x_ref[...]
    h = _rms(x, nw_ref[...]).astype(BF16)
    for f in range(FFN_DIM // fc):
        conv = []
        for part in range(2):
            cols = slice(part * FFN_DIM + f * fc, part * FFN_DIM + (f + 1) * fc)
            buf = 2 * (f % 2) + part
            up = _dot(h, wu_ref[:, cols])
            pad_ref[buf, 0:SUBLANES, :] = jnp.where(j == 0, 0.0, carry_ref[:, cols])
            pad_ref[buf, SUBLANES:SUBLANES + tm, :] = up
            carry_ref[:, cols] = up[tm - SUBLANES:tm, :]
            y = cw_ref[FFN_CONV - 1:FFN_CONV, cols] * up
            for tap in range(FFN_CONV - 1):
                off = SUBLANES - (FFN_CONV - 1) + tap
                y = y + cw_ref[tap:tap + 1, cols] * pad_ref[buf, off:off + tm, :]
            conv.append(y)
        act_ref[:, f * fc:(f + 1) * fc] = (jax.nn.gelu(conv[0], approximate=True) * conv[1]).astype(BF16)
    x = x + _dot(act_ref[...], wd_ref[...])
    gate = jax.nn.sigmoid(_dot(_rms(x, pnw_ref[...]).astype(BF16), wg_ref[...]))
    x = x + _dot(p_ref[...].astype(BF16), wp_ref[...]) * gate
    if final:
        x = _rms(x, fw_ref[...])
    o_ref[...] = x


def _ffn_ple(x, nw, w_up, w_conv, w_down, p, ple_nw, w_gate, w_proj, final_w, layer, final, bsz, seq,
             tm=256, fc=256):
    nt = seq // tm
    resident = dict(pipeline_mode=pl.Buffered(1))
    row = lambda b, j: (b * nt + j, 0)
    return pl.pallas_call(
        functools.partial(_ffn_ple_kernel, tm=tm, fc=fc, final=final),
        grid=(bsz, nt),
        in_specs=[
            pl.BlockSpec((tm, D_MODEL), row),
            _layer_spec((1, D_MODEL), layer),
            _layer_spec((D_MODEL, 2 * FFN_DIM), layer, **resident),
            _layer_spec((FFN_CONV, 2 * FFN_DIM), layer),
            _layer_spec((FFN_DIM, D_MODEL), layer, **resident),
            _layer_spec((tm, PLE_DIM), layer, row),
            _layer_spec((1, D_MODEL), layer),
            _layer_spec((D_MODEL, D_MODEL), layer, **resident),
            _layer_spec((PLE_DIM, D_MODEL), layer, **resident),
            pl.BlockSpec((1, D_MODEL), lambda b, j: (0, 0)),
        ],
        out_specs=pl.BlockSpec((tm, D_MODEL), row),
        out_shape=jax.ShapeDtypeStruct((bsz * seq, D_MODEL), F32),
        scratch_shapes=[
            pltpu.VMEM((tm, FFN_DIM), BF16),
            pltpu.VMEM((SUBLANES, 2 * FFN_DIM), F32),
            pltpu.VMEM((4, SUBLANES + tm, fc), F32),
        ],
        compiler_params=_cparams(("parallel", "arbitrary")),
        name="ffn_ple",
    )(x, nw, w_up, w_conv, w_down, p, ple_nw, w_gate, w_proj, final_w)


def _split_w_in(w):
    w = w.astype(BF16)
    aligned = w[..., 0:2560]
    dn_ba = w[..., 2560:2568]
    mb_qk = w[..., 2568:3592]
    mb_v = w[..., 3592:4104]
    gates = w[..., 4104:7176]
    main = jnp.concatenate([aligned, mb_v, mb_qk, gates], axis=-1)
    small = jnp.pad(dn_ba, ((0, 0),) * (w.ndim - 1) + ((0, LANES - 2 * DN_HEADS),))
    return main, small


def kernel(x, p, rel_bias, norm_mix, w_in, pool_w, pool_scale, dn_conv, dn_a_log, dn_dt_bias,
           dn_norm, w_branch, w_out, norm_ffn, ffn_up, ffn_conv, ffn_down, norm_ple, ple_gate,
           ple_proj, norm_final):
    bsz, seq, d = x.shape
    depth = w_in.shape[0]
    t = bsz * seq
    xt = x.reshape(t, d)
    row = lambda a: a[:, None, :]
    w_main, w_small = _split_w_in(w_in)
    pool_w, w_branch, w_out = pool_w.astype(BF16), w_branch.astype(BF16), w_out.astype(BF16)
    ffn_up, ffn_down = ffn_up.astype(BF16), ffn_down.astype(BF16)
    ple_gate, ple_proj = ple_gate.astype(BF16), ple_proj.astype(BF16)
    p = p.reshape(depth, t, PLE_DIM)
    dist_bias = _moba_dist_bias(rel_bias)
    head_params = jnp.zeros((depth, SUBLANES, LANES), F32)
    head_params = head_params.at[:, 0, DN_HEADS:2 * DN_HEADS].set(dn_a_log)
    head_params = head_params.at[:, 1, DN_HEADS:2 * DN_HEADS].set(dn_dt_bias)
    for i in range(depth):
        z, zs, zqk = _in_proj(xt, row(norm_mix), w_main, w_small, i)
        y_b = _deltanet(z, zs, dn_conv, head_params, row(dn_norm), i, bsz, seq)
        y_c = _moba(z, zqk, dist_bias, bsz, seq)
        xt = _merge(y_b, y_c, z, xt, pool_w, row(pool_scale), w_branch, w_out, i, bsz, seq)
        xt = _ffn_ple(xt, row(norm_ffn), ffn_up, ffn_conv, ffn_down, p, row(norm_ple), ple_gate, ple_proj,
                      norm_final[None, :], i, i == depth - 1, bsz, seq)
    return xt.reshape(bsz, seq, d)
```

```python
import functools
import math

import jax
import jax.numpy as jnp
from jax import lax
from jax.experimental import pallas as pl
from jax.experimental.pallas import tpu as pltpu

F32 = jnp.float32
BF16 = jnp.bfloat16
HIGHEST = lax.Precision.HIGHEST
ACT = jnp.bfloat16

D_MODEL = 1024
BRANCH_WIDTH = 512
POOL_WINDOWS = (2, 4, 8, 16)
POOL_GC = 128
POOL_HALO = 16
DN_HEADS = 4
DN_HEAD_DIM = 128
DN_CONV = 4
DN_CHUNK = 64
MOBA_HEADS = 8
MOBA_HEAD_DIM = 64
MOBA_BLOCK = 256
MOBA_TOPK = 3
MOBA_ONES_ROWS = 16
MOBA_QTILE = 128
REL_BUCKETS = 32
REL_MAX_DIST = 128
FFN_DIM = 2816
FFN_CONV = 3
PLE_DIM = 256
NORM_EPS = 1e-6
NEG_INF = -1e30
LOG2E = math.log2(math.e)
LANES = 128
SUBLANES = 8

COL_POOL = 0
COL_DNQ = 512
COL_DNK = 1024
COL_DNV = 1536
COL_DNZ = 2048
COL_MV = 2560
COL_MQ = 3072
COL_MK = 3584
COL_GATE = 4096
Z_COLS = 7168

VMEM_LIMIT = 56 * 1024 * 1024


def _cparams(sem):
    return pltpu.CompilerParams(dimension_semantics=sem, vmem_limit_bytes=VMEM_LIMIT)


def _layer_spec(block, layer, index=None, **kwargs):
    def index_map(*grid):
        rest = index(*grid) if index is not None else (0,) * len(block)
        return (layer,) + tuple(rest)
    return pl.BlockSpec((None,) + tuple(block), index_map, **kwargs)


def _dot(a, b, precision=None):
    return lax.dot_general(a, b, (((1,), (0,)), ((), ())), precision=precision,
                           preferred_element_type=F32)


def _dot_nt(a, b, precision=None):
    return lax.dot_general(a, b, (((1,), (1,)), ((), ())), precision=precision,
                           preferred_element_type=F32)


def _split3(x):
    hi = x.astype(BF16)
    rest = x - hi.astype(F32)
    mid = rest.astype(BF16)
    lo = (rest - mid.astype(F32)).astype(BF16)
    return hi, mid, lo


def _mm(a, b):
    return _dot(a.astype(BF16), b.astype(BF16))


def _mm_nt(a, b):
    return _dot_nt(a.astype(BF16), b.astype(BF16))


def _mm_tn(a, b):
    return lax.dot_general(a.astype(BF16), b.astype(BF16), (((0,), (0,)), ((), ())),
                           preferred_element_type=F32)


DN_INV_BASE = 8
DN_GROUP = 4


def _rms(x, w):
    return x * lax.rsqrt(jnp.mean(x * x, axis=-1, keepdims=True) + NORM_EPS) * w


def _in_proj_kernel(x_ref, nw_ref, w_ref, ws_ref, z_ref, zs_ref, zqk_ref, *, tn):
    hb = _rms(x_ref[...], nw_ref[...]).astype(BF16)
    zs_ref[...] = _dot(hb, ws_ref[...])
    for jb in range(Z_COLS // tn):
        cols = slice(jb * tn, (jb + 1) * tn)
        z = _dot(hb, w_ref[:, cols])
        z_ref[:, cols] = z.astype(ACT)
        if jb == COL_MQ // tn:
            zqk_ref[...] = z


def _in_proj(x, nw, w_main, w_small, layer, tm=256, tn=1024):
    t = x.shape[0]
    assert COL_MQ % tn == 0 and COL_MK == COL_MQ + tn // 2
    resident = dict(pipeline_mode=pl.Buffered(1))
    return pl.pallas_call(
        functools.partial(_in_proj_kernel, tn=tn),
        grid=(t // tm,),
        in_specs=[
            pl.BlockSpec((tm, D_MODEL), lambda i: (i, 0)),
            _layer_spec((1, D_MODEL), layer),
            _layer_spec((D_MODEL, Z_COLS), layer, **resident),
            _layer_spec((D_MODEL, LANES), layer, **resident),
        ],
        out_specs=[
            pl.BlockSpec((tm, Z_COLS), lambda i: (i, 0)),
            pl.BlockSpec((tm, LANES), lambda i: (i, 0)),
            pl.BlockSpec((tm, tn), lambda i: (i, 0)),
        ],
        out_shape=[jax.ShapeDtypeStruct((t, Z_COLS), ACT),
                   jax.ShapeDtypeStruct((t, LANES), F32),
                   jax.ShapeDtypeStruct((t, tn), F32)],
        compiler_params=_cparams(("parallel",)),
        name="in_proj",
    )(x, nw, w_main, w_small)


def _pool_mixer(u_ref, up_ref, w_ref, sc_ref, j, tp):
    halo = jnp.where(j == 0, 0.0, up_ref[...].astype(F32))
    t = j * tp + lax.broadcasted_iota(jnp.int32, (tp, POOL_GC), 0)
    out = []
    for g, win in enumerate(POOL_WINDOWS):
        cols = slice(g * POOL_GC, (g + 1) * POOL_GC)
        u = u_ref[:, cols].astype(F32)
        tot = jnp.concatenate([halo[:, cols], u], axis=0)
        width = 1
        while width < win:
            tot = tot + pltpu.roll(tot, width, 0)
            width *= 2
        cnt = jnp.minimum(t + 1, win).astype(F32)
        mixed = tot[POOL_HALO:, :] / cnt - u
        y = _dot(mixed.astype(BF16), w_ref[g])
        out.append((y * sc_ref[:, cols]).astype(BF16))
    return jnp.concatenate(out, axis=1)


def _dn_kernel(q_ref, k_ref, v_ref, zg_ref, zs_ref, cwq_ref, cwk_ref, cwv_ref, hp_ref, nw_ref, tri_ref,
               y_ref, halo_ref, state_ref, qs_ref, ks_ref, vs_ref, kb_ref, kg_ref, qd_ref,
               kd_ref, gs_ref, os_ref, *, tc):
    c = pl.program_id(1)

    @pl.when(c == 0)
    def _():
        halo_ref[...] = jnp.zeros_like(halo_ref)
        state_ref[...] = jnp.zeros_like(state_ref)

    for part, (src, cw, dst) in enumerate(((q_ref, cwq_ref, qs_ref), (k_ref, cwk_ref, ks_ref),
                                          (v_ref, cwv_ref, vs_ref))):
        x = src[...].astype(F32)
        ext = jnp.concatenate([halo_ref[part], x], axis=0)
        halo_ref[part] = x[tc - SUBLANES:tc, :]
        acc = cw[0:1, :] * ext
        for tap in range(1, DN_CONV):
            acc = cw[tap:tap + 1, :] * ext + pltpu.roll(acc, 1, 0)
        conv = acc[SUBLANES:, :]
        dst[...] = conv * jax.nn.sigmoid(conv)

    n_chunks = tc // DN_CHUNK
    zs = zs_ref[...]
    beta_c = jax.nn.sigmoid(zs)
    g_c = -jnp.exp(hp_ref[0:1, :]) * jnp.logaddexp(zs + hp_ref[1:2, :], 0.0)
    gc_c = sum(_dot(tri_ref[...], part) for part in _split3(g_c))
    g_last_c = jnp.concatenate(
        [jnp.broadcast_to(gc_c[(n + 1) * DN_CHUNK - 1:(n + 1) * DN_CHUNK, :], (DN_CHUNK, LANES))
         for n in range(n_chunks)], axis=0)
    egc_c = jnp.exp(gc_c)
    kdf_c = jnp.exp(g_last_c - gc_c)

    for h in range(DN_HEADS):
        cols = slice(h * DN_HEAD_DIM, (h + 1) * DN_HEAD_DIM)
        lane_bcast = lambda x, lane: jnp.broadcast_to(x[:, lane:lane + 1], (tc, LANES))
        beta = lane_bcast(beta_c, h)
        egc = lane_bcast(egc_c, DN_HEADS + h)
        q = qs_ref[:, cols]
        k = ks_ref[:, cols]
        q = q * lax.rsqrt(jnp.sum(q * q, axis=-1, keepdims=True) + NORM_EPS) * (DN_HEAD_DIM ** -0.5)
        k = k * lax.rsqrt(jnp.sum(k * k, axis=-1, keepdims=True) + NORM_EPS)
        kb = k * beta
        qs_ref[:, cols] = q
        ks_ref[:, cols] = k
        kb_ref[:, cols] = kb
        vs_ref[:, cols] = vs_ref[:, cols] * beta
        kg_ref[:, cols] = kb * egc
        qd_ref[:, cols] = q * egc
        kd_ref[:, cols] = k * lane_bcast(kdf_c, DN_HEADS + h)
        gs_ref[h] = lane_bcast(gc_c, DN_HEADS + h)

    hc = DN_HEADS * DN_CHUNK
    ri = lax.broadcasted_iota(jnp.int32, (hc, hc), 0)
    ci = lax.broadcasted_iota(jnp.int32, (hc, hc), 1)
    same_head = (ri // DN_CHUNK) == (ci // DN_CHUNK)
    incl = same_head & (ri >= ci)
    level_masks = [((ri // DN_INV_BASE) == (ci // DN_INV_BASE)) & (ri > ci)]
    size = DN_INV_BASE
    while size < DN_CHUNK:
        level_masks.append(((ri // (2 * size)) == (ci // (2 * size)))
                           & ((ri % (2 * size)) >= size) & ((ci % (2 * size)) < size))
        size *= 2
    eye_f = (ri == ci).astype(F32)

    def stack(ref, rows):
        return jnp.concatenate([ref[rows, h * DN_HEAD_DIM:(h + 1) * DN_HEAD_DIM]
                                for h in range(DN_HEADS)], axis=0)

    heads = range(DN_HEADS)
    last = lambda x, h: x[(h + 1) * DN_CHUNK - 1:(h + 1) * DN_CHUNK, :]
    state = [state_ref[h] for h in heads]
    for first in range(0, n_chunks, DN_GROUP):
        grp = range(first, min(first + DN_GROUP, n_chunks))
        rows = [slice(n * DN_CHUNK, (n + 1) * DN_CHUNK) for n in grp]
        q = [stack(qs_ref, r) for r in rows]
        k = [stack(ks_ref, r) for r in rows]
        kb = [stack(kb_ref, r) for r in rows]
        vb = [stack(vs_ref, r) for r in rows]
        kg = [stack(kg_ref, r) for r in rows]
        q_dec = [stack(qd_ref, r) for r in rows]
        k_dec = [stack(kd_ref, r) for r in rows]
        gc = [jnp.concatenate([gs_ref[h, r, :] for h in heads], axis=0) for r in rows]
        gc_t = [x.T for x in gc]
        decay = [jnp.exp(jnp.where(incl, jnp.concatenate([x, x], axis=1)
                                   - jnp.concatenate([xt, xt], axis=0), NEG_INF))
                 for x, xt in zip(gc, gc_t)]
        a_kk = [_mm_nt(x, y) * d for x, y, d in zip(kb, k, decay)]
        m = [-jnp.where(level_masks[0], a, 0.0) for a in a_kk]
        t_mat = [eye_f + x for x in m]
        for _ in range(2):
            m = [_mm(x, x) for x in m]
            t_mat = [t + _mm(t, x) for t, x in zip(t_mat, m)]
        for mask in level_masks[1:]:
            low = [_mm(t, jnp.where(mask, a, 0.0)) for t, a in zip(t_mat, a_kk)]
            t_mat = [t - _mm(x, t) for t, x in zip(t_mat, low)]
        uw = [_mm(t, jnp.concatenate([x, y], axis=1))
              for t, x, y in zip(t_mat, vb, kg)]
        a_qk = [_mm_nt(x, y) * d for x, y, d in zip(q, k, decay)]
        for i, r in enumerate(rows):
            u = uw[i][:, :DN_HEAD_DIM]
            w = uw[i][:, DN_HEAD_DIM:]
            ws, qs_ = [], []
            for h in heads:
                hr = slice(h * DN_CHUNK, (h + 1) * DN_CHUNK)
                both = _mm(jnp.concatenate([w[hr], q_dec[i][hr]], axis=0), state[h])
                ws.append(both[:DN_CHUNK])
                qs_.append(both[DN_CHUNK:])
            v_new = u - jnp.concatenate(ws, axis=0)
            o = jnp.concatenate(qs_, axis=0) + _mm(a_qk[i], v_new)
            for h in heads:
                hr = slice(h * DN_CHUNK, (h + 1) * DN_CHUNK)
                os_ref[r, h * DN_HEAD_DIM:(h + 1) * DN_HEAD_DIM] = o[hr]
                state[h] = (state[h] * jnp.exp(last(gc[i], h))
                            + _mm_tn(k_dec[i][hr], v_new[hr]))
    for h in heads:
        state_ref[h] = state[h]

    for h in range(DN_HEADS):
        cols = slice(h * DN_HEAD_DIM, (h + 1) * DN_HEAD_DIM)
        zg = zg_ref[:, cols].astype(F32)
        y_ref[:, cols] = (_rms(os_ref[:, cols], nw_ref[...]) * (zg * jax.nn.sigmoid(zg))).astype(ACT)


def _deltanet(z, zs, conv_w, head_params, norm_w, layer, bsz, seq, tc=256):
    nt = seq // tc
    wide = DN_HEADS * DN_HEAD_DIM

    def col(cb):
        return pl.BlockSpec((tc, wide), lambda b, c: (b * nt + c, cb))

    def cw(cb):
        return _layer_spec((DN_CONV, wide), layer, lambda b, c: (0, cb))

    pos = jnp.arange(tc, dtype=jnp.int32)
    tri = ((pos[:, None] // DN_CHUNK == pos[None, :] // DN_CHUNK) & (pos[:, None] >= pos[None, :])).astype(BF16)
    tile = pltpu.VMEM((tc, wide), F32)
    return pl.pallas_call(
        functools.partial(_dn_kernel, tc=tc),
        grid=(bsz, nt),
        in_specs=[
            col(COL_DNQ // wide), col(COL_DNK // wide), col(COL_DNV // wide), col(COL_DNZ // wide),
            pl.BlockSpec((tc, LANES), lambda b, c: (b * nt + c, 0)),
            cw(0), cw(1), cw(2),
            _layer_spec((SUBLANES, LANES), layer),
            _layer_spec((1, DN_HEAD_DIM), layer),
            pl.BlockSpec((tc, tc), lambda b, c: (0, 0)),
        ],
        out_specs=pl.BlockSpec((tc, wide), lambda b, c: (b * nt + c, 0)),
        out_shape=jax.ShapeDtypeStruct((bsz * seq, wide), ACT),
        scratch_shapes=[
            pltpu.VMEM((3, SUBLANES, wide), F32),
            pltpu.VMEM((DN_HEADS, DN_HEAD_DIM, DN_HEAD_DIM), F32),
            tile, tile, tile, tile, tile, tile, tile,
            pltpu.VMEM((DN_HEADS, tc, LANES), F32),
            tile,
        ],
        compiler_params=_cparams(("parallel", "arbitrary")),
        name="deltanet",
    )(z, z, z, z, zs, conv_w, conv_w, conv_w, head_params, norm_w, tri)


def _moba_kernel(q_ref, k_ref, v_ref, d_ref, o_ref, kb_ref, vt_ref, bown_ref, bprev_ref, *, nb):
    blk = MOBA_BLOCK
    half = MOBA_HEAD_DIM
    assert 2 * half == LANES

    kmeans = []
    for j in range(nb):
        kj = k_ref[j * blk:(j + 1) * blk, :]
        kmeans.append(jnp.mean(kj.astype(F32), axis=0, keepdims=True))
        kb_ref[j * blk:(j + 1) * blk, :] = kj.astype(BF16)
        vt_ref[j, 0:LANES, :] = v_ref[j * blk:(j + 1) * blk, :].astype(F32).T.astype(BF16)
        vt_ref[j, LANES:LANES + MOBA_ONES_ROWS, :] = jnp.ones((MOBA_ONES_ROWS, blk), BF16)
    kmean = jnp.concatenate(kmeans, axis=0)

    r = lax.broadcasted_iota(jnp.int32, (LANES, LANES), 0)
    c = lax.broadcasted_iota(jnp.int32, (LANES, LANES), 1)
    for hh in range(2):
        far = jnp.broadcast_to(d_ref[hh, 1:2, :], (LANES, LANES)) * LOG2E
        band = pltpu.roll(jnp.broadcast_to(d_ref[hh, 0:1, :], (LANES, LANES)), 0, 1,
                          stride=1, stride_axis=0) * LOG2E
        diag = jnp.where(c >= r, band, NEG_INF)
        off = jnp.where(c < r, band, far)
        lo = hh * blk
        bown_ref[0:LANES, lo:lo + LANES] = diag
        bown_ref[0:LANES, lo + LANES:lo + blk] = off
        bown_ref[LANES:blk, lo:lo + LANES] = jnp.full((LANES, LANES), NEG_INF, F32)
        bown_ref[LANES:blk, lo + LANES:lo + blk] = diag
        bprev_ref[0:LANES, lo:lo + LANES] = far
        bprev_ref[0:LANES, lo + LANES:lo + blk] = far
        bprev_ref[LANES:blk, lo:lo + LANES] = off
        bprev_ref[LANES:blk, lo + LANES:lo + blk] = far
    far_row = jnp.concatenate([jnp.broadcast_to(d_ref[0, 1:2, :], (1, LANES))] * 2
                              + [jnp.broadcast_to(d_ref[1, 1:2, :], (1, LANES))] * 2, axis=1) * LOG2E

    qt = MOBA_QTILE
    tiles_per_blk = blk // qt
    lane = lax.broadcasted_iota(jnp.int32, (qt, LANES), 1)
    blk_id = lax.broadcasted_iota(jnp.int32, (nb, 2 * qt), 0)
    row = lax.broadcasted_iota(jnp.int32, (LANES, qt), 0)
    far_tile = jnp.concatenate([far_row[:, :qt], far_row[:, blk:blk + qt]], axis=1)

    def tile_cols(ref, part, keys):
        return jnp.concatenate([ref[0:keys, part * qt:(part + 1) * qt],
                                ref[0:keys, blk + part * qt:blk + (part + 1) * qt]], axis=1)

    def score_stage(tile):
        i, part = divmod(tile, tiles_per_blk)
        rows = slice(tile * qt, (tile + 1) * qt)
        qf = q_ref[rows, :].astype(F32) * (MOBA_HEAD_DIM ** -0.5)
        q2 = jnp.concatenate([jnp.where(lane < half, qf, 0.0), jnp.where(lane >= half, qf, 0.0)],
                             axis=0)
        q2b = (q2 * LOG2E).astype(BF16)

        sel = None
        if i > MOBA_TOPK:
            gate = jnp.where(blk_id < i, _dot_nt(kmean, q2, HIGHEST), NEG_INF)
            sel = []
            for j in range(i):
                gj = gate[j:j + 1, :]
                ahead = (gate > gj) | ((gate == gj) & (blk_id < j))
                rank = jnp.sum(ahead.astype(F32), axis=0, keepdims=True)
                sel.append(rank < MOBA_TOPK)

        scores = []
        for j in range(i + 1):
            if j == i:
                keys = (part + 1) * qt
                s = _dot_nt(kb_ref[j * blk:j * blk + keys, :], q2b) + tile_cols(bown_ref, part, keys)
            else:
                s = _dot_nt(kb_ref[j * blk:(j + 1) * blk, :], q2b)
                if j == i - 1:
                    s = s + tile_cols(bprev_ref, part, blk)
                    if sel is not None:
                        s = s + jnp.where(sel[j], 0.0, NEG_INF)
                elif sel is not None:
                    s = s + jnp.where(sel[j], far_tile, NEG_INF)
                else:
                    s = s + far_tile
            scores.append(s)
        return scores

    def exp_stage(scores):
        m = functools.reduce(jnp.maximum, [jnp.max(s, axis=0, keepdims=True) for s in scores])
        return ([jnp.exp2(s - m).astype(BF16) for s in scores],)

    def pv_stage(tile, probs):
        acc = None
        for j, p in enumerate(probs):
            pv = _dot(vt_ref[j, :, 0:p.shape[0]], p)
            acc = pv if acc is None else acc + pv
        out_t = acc[:LANES, :] / acc[LANES:LANES + 1, :]
        out_t = jnp.where(row < half, out_t[:, :qt], out_t[:, qt:])
        o_ref[tile * qt:(tile + 1) * qt, :] = out_t.T.astype(ACT)

    sc_q, ex_q = None, None
    for tile in list(reversed(range(nb * tiles_per_blk))) + [None, None]:
        new_sc = (tile, score_stage(tile)) if tile is not None else None
        new_ex = (sc_q[0],) + exp_stage(sc_q[1]) if sc_q is not None else None
        if ex_q is not None:
            pv_stage(*ex_q)
        sc_q, ex_q = new_sc, new_ex


def _moba(z, zqk, dist_bias, bsz, seq):
    nb = seq // MOBA_BLOCK
    pairs = MOBA_HEADS // 2
    kcb = (COL_MK - COL_MQ) // LANES
    vcb = COL_MV // LANES
    return pl.pallas_call(
        functools.partial(_moba_kernel, nb=nb),
        grid=(bsz, pairs),
        in_specs=[
            pl.BlockSpec((seq, LANES), lambda b, p: (b, p)),
            pl.BlockSpec((seq, LANES), lambda b, p: (b, kcb + p)),
            pl.BlockSpec((seq, LANES), lambda b, p: (b, vcb + p)),
            pl.BlockSpec((2, SUBLANES, LANES), lambda b, p: (p, 0, 0)),
        ],
        out_specs=pl.BlockSpec((seq, LANES), lambda b, p: (b, p)),
        out_shape=jax.ShapeDtypeStruct((bsz * seq, MOBA_HEADS * MOBA_HEAD_DIM), ACT),
        scratch_shapes=[
            pltpu.VMEM((seq, LANES), BF16),
            pltpu.VMEM((nb, LANES + MOBA_ONES_ROWS, MOBA_BLOCK), BF16),
            pltpu.VMEM((MOBA_BLOCK, 2 * MOBA_BLOCK), F32),
            pltpu.VMEM((MOBA_BLOCK, 2 * MOBA_BLOCK), F32),
        ],
        compiler_params=_cparams(("parallel", "parallel")),
        name="moba",
    )(zqk, zqk, z, dist_bias)


def _t5_bucket(rel):
    n = jnp.maximum(-rel, 0)
    exact = REL_BUCKETS // 2
    nf = jnp.maximum(n, 1).astype(F32)
    large = exact + (jnp.log(nf / exact) / math.log(REL_MAX_DIST / exact)
                     * (REL_BUCKETS - exact)).astype(jnp.int32)
    large = jnp.minimum(large, REL_BUCKETS - 1)
    return jnp.where(n < exact, n, large)


def _moba_dist_bias(rel_bias):
    bias_ht = rel_bias.T.astype(F32)
    near = bias_ht[:, _t5_bucket(-jnp.arange(LANES, dtype=jnp.int32))]
    far = bias_ht[:, _t5_bucket(jnp.full((LANES,), -REL_MAX_DIST, jnp.int32))]
    rows = jnp.zeros((MOBA_HEADS, SUBLANES, LANES), F32)
    return rows.at[:, 0, :].set(near).at[:, 1, :].set(far)


def _merge_kernel(u_ref, up_ref, pw_ref, ps_ref, yb_ref, yc_ref, ga_ref, gb_ref, gc_ref, x_ref, wb_ref,
                  wo_ref, o_ref, *, tm):
    y_a = _pool_mixer(u_ref, up_ref, pw_ref, ps_ref, pl.program_id(1), tm)
    merged = None
    for n, (y, g_ref) in enumerate(((y_a, ga_ref), (yb_ref[...], gb_ref), (yc_ref[...], gc_ref))):
        proj = _dot(y, wb_ref[n])
        gate = jax.nn.sigmoid(g_ref[...].astype(F32))
        term = gate * proj
        merged = term if merged is None else merged + term
    o_ref[...] = x_ref[...] + _dot(merged.astype(BF16), wo_ref[...])


def _merge(yb, yc, z, x, pool_w, pool_scale, w_branch, w_out, layer, bsz, seq, tm=1024):
    nt = seq // tm
    hb = tm // POOL_HALO
    gcb = COL_GATE // D_MODEL
    row = lambda b, j: (b * nt + j, 0)
    return pl.pallas_call(
        functools.partial(_merge_kernel, tm=tm),
        grid=(bsz, nt),
        in_specs=[
            pl.BlockSpec((tm, BRANCH_WIDTH), row),
            pl.BlockSpec((POOL_HALO, BRANCH_WIDTH),
                         lambda b, j: (jnp.maximum((b * nt + j) * hb - 1, 0), 0)),
            _layer_spec((4, POOL_GC, POOL_GC), layer),
            _layer_spec((1, BRANCH_WIDTH), layer),
            pl.BlockSpec((tm, BRANCH_WIDTH), row),
            pl.BlockSpec((tm, BRANCH_WIDTH), row),
            pl.BlockSpec((tm, D_MODEL), lambda b, j: (b * nt + j, gcb)),
            pl.BlockSpec((tm, D_MODEL), lambda b, j: (b * nt + j, gcb + 1)),
            pl.BlockSpec((tm, D_MODEL), lambda b, j: (b * nt + j, gcb + 2)),
            pl.BlockSpec((tm, D_MODEL), row),
            _layer_spec((3, BRANCH_WIDTH, D_MODEL), layer),
            _layer_spec((D_MODEL, D_MODEL), layer),
        ],
        out_specs=pl.BlockSpec((tm, D_MODEL), row),
        out_shape=jax.ShapeDtypeStruct((bsz * seq, D_MODEL), F32),
        compiler_params=_cparams(("parallel", "parallel")),
        name="merge",
    )(z, z, pool_w, pool_scale, yb, yc, z, z, z, x, w_branch, w_out)


def _ffn_ple_kernel(x_ref, nw_ref, wu_ref, cw_ref, wd_ref, p_ref, pnw_ref, wg_ref, wp_ref, fw_ref, o_ref,
                    act_ref, carry_ref, pad_ref, *, tm, fc, final):
    j = pl.program_id(1)
    x = x_ref[...]
    h = _rms(x, nw_ref[...]).astype(BF16)
    for f in range(FFN_DIM // fc):
        conv = []
        for part in range(2):
            cols = slice(part * FFN_DIM + f * fc, part * FFN_DIM + (f + 1) * fc)
            buf = 2 * (f % 2) + part
            up = _dot(h, wu_ref[:, cols])
            pad_ref[buf, 0:SUBLANES, :] = jnp.where(j == 0, 0.0, carry_ref[:, cols])
            pad_ref[buf, SUBLANES:SUBLANES + tm, :] = up
            carry_ref[:, cols] = up[tm - SUBLANES:tm, :]
            y = cw_ref[FFN_CONV - 1:FFN_CONV, cols] * up
            for tap in range(FFN_CONV - 1):
                off = SUBLANES - (FFN_CONV - 1) + tap
                y = y + cw_ref[tap:tap + 1, cols] * pad_ref[buf, off:off + tm, :]
            conv.append(y)
        act_ref[:, f * fc:(f + 1) * fc] = (jax.nn.gelu(conv[0], approximate=True) * conv[1]).astype(BF16)
    x = x + _dot(act_ref[...], wd_ref[...])
    gate = jax.nn.sigmoid(_dot(_rms(x, pnw_ref[...]).astype(BF16), wg_ref[...]))
    x = x + _dot(p_ref[...].astype(BF16), wp_ref[...]) * gate
    if final:
        x = _rms(x, fw_ref[...])
    o_ref[...] = x


def _ffn_ple(x, nw, w_up, w_conv, w_down, p, ple_nw, w_gate, w_proj, final_w, layer, final, bsz, seq,
             tm=256, fc=256):
    nt = seq // tm
    resident = dict(pipeline_mode=pl.Buffered(1))
    row = lambda b, j: (b * nt + j, 0)
    return pl.pallas_call(
        functools.partial(_ffn_ple_kernel, tm=tm, fc=fc, final=final),
        grid=(bsz, nt),
        in_specs=[
            pl.BlockSpec((tm, D_MODEL), row),
            _layer_spec((1, D_MODEL), layer),
            _layer_spec((D_MODEL, 2 * FFN_DIM), layer, **resident),
            _layer_spec((FFN_CONV, 2 * FFN_DIM), layer),
            _layer_spec((FFN_DIM, D_MODEL), layer, **resident),
            _layer_spec((tm, PLE_DIM), layer, row),
            _layer_spec((1, D_MODEL), layer),
            _layer_spec((D_MODEL, D_MODEL), layer, **resident),
            _layer_spec((PLE_DIM, D_MODEL), layer, **resident),
            pl.BlockSpec((1, D_MODEL), lambda b, j: (0, 0)),
        ],
        out_specs=pl.BlockSpec((tm, D_MODEL), row),
        out_shape=jax.ShapeDtypeStruct((bsz * seq, D_MODEL), F32),
        scratch_shapes=[
            pltpu.VMEM((tm, FFN_DIM), BF16),
            pltpu.VMEM((SUBLANES, 2 * FFN_DIM), F32),
            pltpu.VMEM((4, SUBLANES + tm, fc), F32),
        ],
        compiler_params=_cparams(("parallel", "arbitrary")),
        name="ffn_ple",
    )(x, nw, w_up, w_conv, w_down, p, ple_nw, w_gate, w_proj, final_w)


def _split_w_in(w):
    w = w.astype(BF16)
    aligned = w[..., 0:2560]
    dn_ba = w[..., 2560:2568]
    mb_qk = w[..., 2568:3592]
    mb_v = w[..., 3592:4104]
    gates = w[..., 4104:7176]
    main = jnp.concatenate([aligned, mb_v, mb_qk, gates], axis=-1)
    small = jnp.pad(dn_ba, ((0, 0),) * (w.ndim - 1) + ((0, LANES - 2 * DN_HEADS),))
    return main, small


def kernel(x, p, rel_bias, norm_mix, w_in, pool_w, pool_scale, dn_conv, dn_a_log, dn_dt_bias,
           dn_norm, w_branch, w_out, norm_ffn, ffn_up, ffn_conv, ffn_down, norm_ple, ple_gate,
           ple_proj, norm_final):
    bsz, seq, d = x.shape
    depth = w_in.shape[0]
    t = bsz * seq
    xt = x.reshape(t, d)
    row = lambda a: a[:, None, :]
    w_main, w_small = _split_w_in(w_in)
    pool_w, w_branch, w_out = pool_w.astype(BF16), w_branch.astype(BF16), w_out.astype(BF16)
    ffn_up, ffn_down = ffn_up.astype(BF16), ffn_down.astype(BF16)
    ple_gate, ple_proj = ple_gate.astype(BF16), ple_proj.astype(BF16)
    p = p.reshape(depth, t, PLE_DIM)
    dist_bias = _moba_dist_bias(rel_bias)
    head_params = jnp.zeros((depth, SUBLANES, LANES), F32)
    head_params = head_params.at[:, 0, DN_HEADS:2 * DN_HEADS].set(dn_a_log)
    head_params = head_params.at[:, 1, DN_HEADS:2 * DN_HEADS].set(dn_dt_bias)
    for i in range(depth):
        z, zs, zqk = _in_proj(xt, row(norm_mix), w_main, w_small, i)
        y_b = _deltanet(z, zs, dn_conv, head_params, row(dn_norm), i, bsz, seq)
        y_c = _moba(z, zqk, dist_bias, bsz, seq)
        xt = _merge(y_b, y_c, z, xt, pool_w, row(pool_scale), w_branch, w_out, i, bsz, seq)
        xt = _ffn_ple(xt, row(norm_ffn), ffn_up, ffn_conv, ffn_down, p, row(norm_ple), ple_gate, ple_proj,
                      norm_final[None, :], i, i == depth - 1, bsz, seq)
    return xt.reshape(bsz, seq, d)
```

```python
import functools
import math

import jax
import jax.numpy as jnp
from jax import lax
from jax.experimental import pallas as pl
from jax.experimental.pallas import tpu as pltpu

F32 = jnp.float32
BF16 = jnp.bfloat16
HIGHEST = lax.Precision.HIGHEST
ACT = jnp.bfloat16

D_MODEL = 1024
BRANCH_WIDTH = 512
POOL_WINDOWS = (2, 4, 8, 16)
POOL_GC = 128
POOL_HALO = 16
DN_HEADS = 4
DN_HEAD_DIM = 128
DN_CONV = 4
DN_CHUNK = 64
MOBA_HEADS = 8
MOBA_HEAD_DIM = 64
MOBA_BLOCK = 256
MOBA_TOPK = 3
MOBA_ONES_ROWS = 16
MOBA_QTILE = 128
REL_BUCKETS = 32
REL_MAX_DIST = 128
FFN_DIM = 2816
FFN_CONV = 3
PLE_DIM = 256
NORM_EPS = 1e-6
NEG_INF = -1e30
LOG2E = math.log2(math.e)
LANES = 128
SUBLANES = 8

COL_POOL = 0
COL_DNQ = 512
COL_DNK = 1024
COL_DNV = 1536
COL_DNZ = 2048
COL_MV = 2560
COL_MQ = 3072
COL_MK = 3584
COL_GATE = 4096
Z_COLS = 7168

VMEM_LIMIT = 56 * 1024 * 1024


def _cparams(sem):
    return pltpu.CompilerParams(dimension_semantics=sem, vmem_limit_bytes=VMEM_LIMIT)


def _layer_spec(block, layer, index=None, **kwargs):
    def index_map(*grid):
        rest = index(*grid) if index is not None else (0,) * len(block)
        return (layer,) + tuple(rest)
    return pl.BlockSpec((None,) + tuple(block), index_map, **kwargs)


def _dot(a, b, precision=None):
    return lax.dot_general(a, b, (((1,), (0,)), ((), ())), precision=precision,
                           preferred_element_type=F32)


def _dot_nt(a, b, precision=None):
    return lax.dot_general(a, b, (((1,), (1,)), ((), ())), precision=precision,
                           preferred_element_type=F32)


def _split3(x):
    hi = x.astype(BF16)
    rest = x - hi.astype(F32)
    mid = rest.astype(BF16)
    lo = (rest - mid.astype(F32)).astype(BF16)
    return hi, mid, lo


def _mm(a, b):
    return _dot(a.astype(BF16), b.astype(BF16))


def _mm_nt(a, b):
    return _dot_nt(a.astype(BF16), b.astype(BF16))


def _mm_tn(a, b):
    return lax.dot_general(a.astype(BF16), b.astype(BF16), (((0,), (0,)), ((), ())),
                           preferred_element_type=F32)


DN_INV_BASE = 8
DN_GROUP = 4


def _rms(x, w):
    return x * lax.rsqrt(jnp.mean(x * x, axis=-1, keepdims=True) + NORM_EPS) * w


def _in_proj_kernel(x_ref, nw_ref, w_ref, ws_ref, z_ref, zs_ref, zqk_ref, *, tn):
    hb = _rms(x_ref[...], nw_ref[...]).astype(BF16)
    zs_ref[...] = _dot(hb, ws_ref[...])
    for jb in range(Z_COLS // tn):
        cols = slice(jb * tn, (jb + 1) * tn)
        z = _dot(hb, w_ref[:, cols])
        z_ref[:, cols] = z.astype(ACT)
        if jb == COL_MQ // tn:
            zqk_ref[...] = z


def _in_proj(x, nw, w_main, w_small, layer, tm=512, tn=1024):
    t = x.shape[0]
    assert COL_MQ % tn == 0 and COL_MK == COL_MQ + tn // 2
    resident = dict(pipeline_mode=pl.Buffered(1))
    return pl.pallas_call(
        functools.partial(_in_proj_kernel, tn=tn),
        grid=(t // tm,),
        in_specs=[
            pl.BlockSpec((tm, D_MODEL), lambda i: (i, 0)),
            _layer_spec((1, D_MODEL), layer),
            _layer_spec((D_MODEL, Z_COLS), layer, **resident),
            _layer_spec((D_MODEL, LANES), layer, **resident),
        ],
        out_specs=[
            pl.BlockSpec((tm, Z_COLS), lambda i: (i, 0)),
            pl.BlockSpec((tm, LANES), lambda i: (i, 0)),
            pl.BlockSpec((tm, tn), lambda i: (i, 0)),
        ],
        out_shape=[jax.ShapeDtypeStruct((t, Z_COLS), ACT),
                   jax.ShapeDtypeStruct((t, LANES), F32),
                   jax.ShapeDtypeStruct((t, tn), F32)],
        compiler_params=_cparams(("parallel",)),
        name="in_proj",
    )(x, nw, w_main, w_small)


def _pool_mixer(u_ref, up_ref, w_ref, sc_ref, j, tp):
    halo = jnp.where(j == 0, 0.0, up_ref[...].astype(F32))
    t = j * tp + lax.broadcasted_iota(jnp.int32, (tp, POOL_GC), 0)
    out = []
    for g, win in enumerate(POOL_WINDOWS):
        cols = slice(g * POOL_GC, (g + 1) * POOL_GC)
        u = u_ref[:, cols].astype(F32)
        tot = jnp.concatenate([halo[:, cols], u], axis=0)
        width = 1
        while width < win:
            tot = tot + pltpu.roll(tot, width, 0)
            width *= 2
        cnt = jnp.minimum(t + 1, win).astype(F32)
        mixed = tot[POOL_HALO:, :] / cnt - u
        y = _dot(mixed.astype(BF16), w_ref[g])
        out.append((y * sc_ref[:, cols]).astype(BF16))
    return jnp.concatenate(out, axis=1)


def _dn_kernel(q_ref, k_ref, v_ref, zg_ref, zs_ref, cwq_ref, cwk_ref, cwv_ref, hp_ref, nw_ref, tri_ref,
               y_ref, halo_ref, state_ref, qs_ref, ks_ref, vs_ref, kb_ref, kg_ref, qd_ref,
               kd_ref, gs_ref, os_ref, *, tc):
    c = pl.program_id(1)

    @pl.when(c == 0)
    def _():
        halo_ref[...] = jnp.zeros_like(halo_ref)
        state_ref[...] = jnp.zeros_like(state_ref)

    for part, (src, cw, dst) in enumerate(((q_ref, cwq_ref, qs_ref), (k_ref, cwk_ref, ks_ref),
                                          (v_ref, cwv_ref, vs_ref))):
        x = src[...].astype(F32)
        ext = jnp.concatenate([halo_ref[part], x], axis=0)
        halo_ref[part] = x[tc - SUBLANES:tc, :]
        acc = cw[0:1, :] * ext
        for tap in range(1, DN_CONV):
            acc = cw[tap:tap + 1, :] * ext + pltpu.roll(acc, 1, 0)
        conv = acc[SUBLANES:, :]
        dst[...] = conv * jax.nn.sigmoid(conv)

    n_chunks = tc // DN_CHUNK
    zs = zs_ref[...]
    beta_c = jax.nn.sigmoid(zs)
    g_c = -jnp.exp(hp_ref[0:1, :]) * jnp.logaddexp(zs + hp_ref[1:2, :], 0.0)
    gc_c = sum(_dot(tri_ref[...], part) for part in _split3(g_c))
    g_last_c = jnp.concatenate(
        [jnp.broadcast_to(gc_c[(n + 1) * DN_CHUNK - 1:(n + 1) * DN_CHUNK, :], (DN_CHUNK, LANES))
         for n in range(n_chunks)], axis=0)
    egc_c = jnp.exp(gc_c)
    kdf_c = jnp.exp(g_last_c - gc_c)

    for h in range(DN_HEADS):
        cols = slice(h * DN_HEAD_DIM, (h + 1) * DN_HEAD_DIM)
        lane_bcast = lambda x, lane: jnp.broadcast_to(x[:, lane:lane + 1], (tc, LANES))
        beta = lane_bcast(beta_c, h)
        egc = lane_bcast(egc_c, DN_HEADS + h)
        q = qs_ref[:, cols]
        k = ks_ref[:, cols]
        q = q * lax.rsqrt(jnp.sum(q * q, axis=-1, keepdims=True) + NORM_EPS) * (DN_HEAD_DIM ** -0.5)
        k = k * lax.rsqrt(jnp.sum(k * k, axis=-1, keepdims=True) + NORM_EPS)
        kb = k * beta
        qs_ref[:, cols] = q
        ks_ref[:, cols] = k
        kb_ref[:, cols] = kb
        vs_ref[:, cols] = vs_ref[:, cols] * beta
        kg_ref[:, cols] = kb * egc
        qd_ref[:, cols] = q * egc
        kd_ref[:, cols] = k * lane_bcast(kdf_c, DN_HEADS + h)
        gs_ref[h] = lane_bcast(gc_c, DN_HEADS + h)

    hc = DN_HEADS * DN_CHUNK
    ri = lax.broadcasted_iota(jnp.int32, (hc, hc), 0)
    ci = lax.broadcasted_iota(jnp.int32, (hc, hc), 1)
    same_head = (ri // DN_CHUNK) == (ci // DN_CHUNK)
    incl = same_head & (ri >= ci)
    level_masks = [((ri // DN_INV_BASE) == (ci // DN_INV_BASE)) & (ri > ci)]
    size = DN_INV_BASE
    while size < DN_CHUNK:
        level_masks.append(((ri // (2 * size)) == (ci // (2 * size)))
                           & ((ri % (2 * size)) >= size) & ((ci % (2 * size)) < size))
        size *= 2
    eye_f = (ri == ci).astype(F32)

    def stack(ref, rows):
        return jnp.concatenate([ref[rows, h * DN_HEAD_DIM:(h + 1) * DN_HEAD_DIM]
                                for h in range(DN_HEADS)], axis=0)

    heads = range(DN_HEADS)
    last = lambda x, h: x[(h + 1) * DN_CHUNK - 1:(h + 1) * DN_CHUNK, :]
    state = [state_ref[h] for h in heads]
    for first in range(0, n_chunks, DN_GROUP):
        grp = range(first, min(first + DN_GROUP, n_chunks))
        rows = [slice(n * DN_CHUNK, (n + 1) * DN_CHUNK) for n in grp]
        q = [stack(qs_ref, r) for r in rows]
        k = [stack(ks_ref, r) for r in rows]
        kb = [stack(kb_ref, r) for r in rows]
        vb = [stack(vs_ref, r) for r in rows]
        kg = [stack(kg_ref, r) for r in rows]
        q_dec = [stack(qd_ref, r) for r in rows]
        k_dec = [stack(kd_ref, r) for r in rows]
        gc = [jnp.concatenate([gs_ref[h, r, :] for h in heads], axis=0) for r in rows]
        gc_t = [x.T for x in gc]
        decay = [jnp.exp(jnp.where(incl, jnp.concatenate([x, x], axis=1)
                                   - jnp.concatenate([xt, xt], axis=0), NEG_INF))
                 for x, xt in zip(gc, gc_t)]
        a_kk = [_mm_nt(x, y) * d for x, y, d in zip(kb, k, decay)]
        m = [-jnp.where(level_masks[0], a, 0.0) for a in a_kk]
        t_mat = [eye_f + x for x in m]
        for _ in range(2):
            m = [_mm(x, x) for x in m]
            t_mat = [t + _mm(t, x) for t, x in zip(t_mat, m)]
        for mask in level_masks[1:]:
            low = [_mm(t, jnp.where(mask, a, 0.0)) for t, a in zip(t_mat, a_kk)]
            t_mat = [t - _mm(x, t) for t, x in zip(t_mat, low)]
        uw = [_mm(t, jnp.concatenate([x, y], axis=1))
              for t, x, y in zip(t_mat, vb, kg)]
        a_qk = [_mm_nt(x, y) * d for x, y, d in zip(q, k, decay)]
        for i, r in enumerate(rows):
            u = uw[i][:, :DN_HEAD_DIM]
            w = uw[i][:, DN_HEAD_DIM:]
            ws, qs_ = [], []
            for h in heads:
                hr = slice(h * DN_CHUNK, (h + 1) * DN_CHUNK)
                both = _mm(jnp.concatenate([w[hr], q_dec[i][hr]], axis=0), state[h])
                ws.append(both[:DN_CHUNK])
                qs_.append(both[DN_CHUNK:])
            v_new = u - jnp.concatenate(ws, axis=0)
            o = jnp.concatenate(qs_, axis=0) + _mm(a_qk[i], v_new)
            for h in heads:
                hr = slice(h * DN_CHUNK, (h + 1) * DN_CHUNK)
                os_ref[r, h * DN_HEAD_DIM:(h + 1) * DN_HEAD_DIM] = o[hr]
                state[h] = (state[h] * jnp.exp(last(gc[i], h))
                            + _mm_tn(k_dec[i][hr], v_new[hr]))
    for h in heads:
        state_ref[h] = state[h]

    for h in range(DN_HEADS):
        cols = slice(h * DN_HEAD_DIM, (h + 1) * DN_HEAD_DIM)
        zg = zg_ref[:, cols].astype(F32)
        y_ref[:, cols] = (_rms(os_ref[:, cols], nw_ref[...]) * (zg * jax.nn.sigmoid(zg))).astype(ACT)


def _deltanet(z, zs, conv_w, head_params, norm_w, layer, bsz, seq, tc=256):
    nt = seq // tc
    wide = DN_HEADS * DN_HEAD_DIM

    def col(cb):
        return pl.BlockSpec((tc, wide), lambda b, c: (b * nt + c, cb))

    def cw(cb):
        return _layer_spec((DN_CONV, wide), layer, lambda b, c: (0, cb))

    pos = jnp.arange(tc, dtype=jnp.int32)
    tri = ((pos[:, None] // DN_CHUNK == pos[None, :] // DN_CHUNK) & (pos[:, None] >= pos[None, :])).astype(BF16)
    tile = pltpu.VMEM((tc, wide), F32)
    return pl.pallas_call(
        functools.partial(_dn_kernel, tc=tc),
        grid=(bsz, nt),
        in_specs=[
            col(COL_DNQ // wide), col(COL_DNK // wide), col(COL_DNV // wide), col(COL_DNZ // wide),
            pl.BlockSpec((tc, LANES), lambda b, c: (b * nt + c, 0)),
            cw(0), cw(1), cw(2),
            _layer_spec((SUBLANES, LANES), layer),
            _layer_spec((1, DN_HEAD_DIM), layer),
            pl.BlockSpec((tc, tc), lambda b, c: (0, 0)),
        ],
        out_specs=pl.BlockSpec((tc, wide), lambda b, c: (b * nt + c, 0)),
        out_shape=jax.ShapeDtypeStruct((bsz * seq, wide), ACT),
        scratch_shapes=[
            pltpu.VMEM((3, SUBLANES, wide), F32),
            pltpu.VMEM((DN_HEADS, DN_HEAD_DIM, DN_HEAD_DIM), F32),
            tile, tile, tile, tile, tile, tile, tile,
            pltpu.VMEM((DN_HEADS, tc, LANES), F32),
            tile,
        ],
        compiler_params=_cparams(("parallel", "arbitrary")),
        name="deltanet",
    )(z, z, z, z, zs, conv_w, conv_w, conv_w, head_params, norm_w, tri)


def _moba_kernel(q_ref, k_ref, v_ref, d_ref, o_ref, kb_ref, vt_ref, bown_ref, bprev_ref, *, nb):
    blk = MOBA_BLOCK
    half = MOBA_HEAD_DIM
    assert 2 * half == LANES

    kmeans = []
    for j in range(nb):
        kj = k_ref[j * blk:(j + 1) * blk, :]
        kmeans.append(jnp.mean(kj.astype(F32), axis=0, keepdims=True))
        kb_ref[j * blk:(j + 1) * blk, :] = kj.astype(BF16)
        vt_ref[j, 0:LANES, :] = v_ref[j * blk:(j + 1) * blk, :].astype(F32).T.astype(BF16)
        vt_ref[j, LANES:LANES + MOBA_ONES_ROWS, :] = jnp.ones((MOBA_ONES_ROWS, blk), BF16)
    kmean = jnp.concatenate(kmeans, axis=0)

    r = lax.broadcasted_iota(jnp.int32, (LANES, LANES), 0)
    c = lax.broadcasted_iota(jnp.int32, (LANES, LANES), 1)
    for hh in range(2):
        far = jnp.broadcast_to(d_ref[hh, 1:2, :], (LANES, LANES)) * LOG2E
        band = pltpu.roll(jnp.broadcast_to(d_ref[hh, 0:1, :], (LANES, LANES)), 0, 1,
                          stride=1, stride_axis=0) * LOG2E
        diag = jnp.where(c >= r, band, NEG_INF)
        off = jnp.where(c < r, band, far)
        lo = hh * blk
        bown_ref[0:LANES, lo:lo + LANES] = diag
        bown_ref[0:LANES, lo + LANES:lo + blk] = off
        bown_ref[LANES:blk, lo:lo + LANES] = jnp.full((LANES, LANES), NEG_INF, F32)
        bown_ref[LANES:blk, lo + LANES:lo + blk] = diag
        bprev_ref[0:LANES, lo:lo + LANES] = far
        bprev_ref[0:LANES, lo + LANES:lo + blk] = far
        bprev_ref[LANES:blk, lo:lo + LANES] = off
        bprev_ref[LANES:blk, lo + LANES:lo + blk] = far
    far_row = jnp.concatenate([jnp.broadcast_to(d_ref[0, 1:2, :], (1, LANES))] * 2
                              + [jnp.broadcast_to(d_ref[1, 1:2, :], (1, LANES))] * 2, axis=1) * LOG2E

    qt = MOBA_QTILE
    tiles_per_blk = blk // qt
    lane = lax.broadcasted_iota(jnp.int32, (qt, LANES), 1)
    blk_id = lax.broadcasted_iota(jnp.int32, (nb, 2 * qt), 0)
    row = lax.broadcasted_iota(jnp.int32, (LANES, qt), 0)
    far_tile = jnp.concatenate([far_row[:, :qt], far_row[:, blk:blk + qt]], axis=1)

    def tile_cols(ref, part, keys):
        return jnp.concatenate([ref[0:keys, part * qt:(part + 1) * qt],
                                ref[0:keys, blk + part * qt:blk + (part + 1) * qt]], axis=1)

    def score_stage(tile):
        i, part = divmod(tile, tiles_per_blk)
        rows = slice(tile * qt, (tile + 1) * qt)
        qf = q_ref[rows, :].astype(F32) * (MOBA_HEAD_DIM ** -0.5)
        q2 = jnp.concatenate([jnp.where(lane < half, qf, 0.0), jnp.where(lane >= half, qf, 0.0)],
                             axis=0)
        q2b = (q2 * LOG2E).astype(BF16)

        sel = None
        if i > MOBA_TOPK:
            gate = jnp.where(blk_id < i, _dot_nt(kmean, q2, HIGHEST), NEG_INF)
            sel = []
            for j in range(i):
                gj = gate[j:j + 1, :]
                ahead = (gate > gj) | ((gate == gj) & (blk_id < j))
                rank = jnp.sum(ahead.astype(F32), axis=0, keepdims=True)
                sel.append(rank < MOBA_TOPK)

        scores = []
        for j in range(i + 1):
            if j == i:
                keys = (part + 1) * qt
                s = _dot_nt(kb_ref[j * blk:j * blk + keys, :], q2b) + tile_cols(bown_ref, part, keys)
            else:
                s = _dot_nt(kb_ref[j * blk:(j + 1) * blk, :], q2b)
                if j == i - 1:
                    s = s + tile_cols(bprev_ref, part, blk)
                    if sel is not None:
                        s = s + jnp.where(sel[j], 0.0, NEG_INF)
                elif sel is not None:
                    s = s + jnp.where(sel[j], far_tile, NEG_INF)
                else:
                    s = s + far_tile
            scores.append(s)
        return scores

    def exp_stage(scores):
        m = functools.reduce(jnp.maximum, [jnp.max(s, axis=0, keepdims=True) for s in scores])
        return ([jnp.exp2(s - m).astype(BF16) for s in scores],)

    def pv_stage(tile, probs):
        acc = None
        for j, p in enumerate(probs):
            pv = _dot(vt_ref[j, :, 0:p.shape[0]], p)
            acc = pv if acc is None else acc + pv
        out_t = acc[:LANES, :] / acc[LANES:LANES + 1, :]
        out_t = jnp.where(row < half, out_t[:, :qt], out_t[:, qt:])
        o_ref[tile * qt:(tile + 1) * qt, :] = out_t.T.astype(ACT)

    sc_q, ex_q = None, None
    for tile in list(reversed(range(nb * tiles_per_blk))) + [None, None]:
        new_sc = (tile, score_stage(tile)) if tile is not None else None
        new_ex = (sc_q[0],) + exp_stage(sc_q[1]) if sc_q is not None else None
        if ex_q is not None:
            pv_stage(*ex_q)
        sc_q, ex_q = new_sc, new_ex


def _moba(z, zqk, dist_bias, bsz, seq):
    nb = seq // MOBA_BLOCK
    pairs = MOBA_HEADS // 2
    kcb = (COL_MK - COL_MQ) // LANES
    vcb = COL_MV // LANES
    return pl.pallas_call(
        functools.partial(_moba_kernel, nb=nb),
        grid=(bsz, pairs),
        in_specs=[
            pl.BlockSpec((seq, LANES), lambda b, p: (b, p)),
            pl.BlockSpec((seq, LANES), lambda b, p: (b, kcb + p)),
            pl.BlockSpec((seq, LANES), lambda b, p: (b, vcb + p)),
            pl.BlockSpec((2, SUBLANES, LANES), lambda b, p: (p, 0, 0)),
        ],
        out_specs=pl.BlockSpec((seq, LANES), lambda b, p: (b, p)),
        out_shape=jax.ShapeDtypeStruct((bsz * seq, MOBA_HEADS * MOBA_HEAD_DIM), ACT),
        scratch_shapes=[
            pltpu.VMEM((seq, LANES), BF16),
            pltpu.VMEM((nb, LANES + MOBA_ONES_ROWS, MOBA_BLOCK), BF16),
            pltpu.VMEM((MOBA_BLOCK, 2 * MOBA_BLOCK), F32),
            pltpu.VMEM((MOBA_BLOCK, 2 * MOBA_BLOCK), F32),
        ],
        compiler_params=_cparams(("parallel", "parallel")),
        name="moba",
    )(zqk, zqk, z, dist_bias)


def _t5_bucket(rel):
    n = jnp.maximum(-rel, 0)
    exact = REL_BUCKETS // 2
    nf = jnp.maximum(n, 1).astype(F32)
    large = exact + (jnp.log(nf / exact) / math.log(REL_MAX_DIST / exact)
                     * (REL_BUCKETS - exact)).astype(jnp.int32)
    large = jnp.minimum(large, REL_BUCKETS - 1)
    return jnp.where(n < exact, n, large)


def _moba_dist_bias(rel_bias):
    bias_ht = rel_bias.T.astype(F32)
    near = bias_ht[:, _t5_bucket(-jnp.arange(LANES, dtype=jnp.int32))]
    far = bias_ht[:, _t5_bucket(jnp.full((LANES,), -REL_MAX_DIST, jnp.int32))]
    rows = jnp.zeros((MOBA_HEADS, SUBLANES, LANES), F32)
    return rows.at[:, 0, :].set(near).at[:, 1, :].set(far)


def _merge_kernel(u_ref, up_ref, pw_ref, ps_ref, yb_ref, yc_ref, ga_ref, gb_ref, gc_ref, x_ref, wb_ref,
                  wo_ref, o_ref, *, tm):
    y_a = _pool_mixer(u_ref, up_ref, pw_ref, ps_ref, pl.program_id(1), tm)
    merged = None
    for n, (y, g_ref) in enumerate(((y_a, ga_ref), (yb_ref[...], gb_ref), (yc_ref[...], gc_ref))):
        proj = _dot(y, wb_ref[n])
        gate = jax.nn.sigmoid(g_ref[...].astype(F32))
        term = gate * proj
        merged = term if merged is None else merged + term
    o_ref[...] = x_ref[...] + _dot(merged.astype(BF16), wo_ref[...])


def _merge(yb, yc, z, x, pool_w, pool_scale, w_branch, w_out, layer, bsz, seq, tm=1024):
    nt = seq // tm
    hb = tm // POOL_HALO
    gcb = COL_GATE // D_MODEL
    row = lambda b, j: (b * nt + j, 0)
    return pl.pallas_call(
        functools.partial(_merge_kernel, tm=tm),
        grid=(bsz, nt),
        in_specs=[
            pl.BlockSpec((tm, BRANCH_WIDTH), row),
            pl.BlockSpec((POOL_HALO, BRANCH_WIDTH),
                         lambda b, j: (jnp.maximum((b * nt + j) * hb - 1, 0), 0)),
            _layer_spec((4, POOL_GC, POOL_GC), layer),
            _layer_spec((1, BRANCH_WIDTH), layer),
            pl.BlockSpec((tm, BRANCH_WIDTH), row),
            pl.BlockSpec((tm, BRANCH_WIDTH), row),
            pl.BlockSpec((tm, D_MODEL), lambda b, j: (b * nt + j, gcb)),
            pl.BlockSpec((tm, D_MODEL), lambda b, j: (b * nt + j, gcb + 1)),
            pl.BlockSpec((tm, D_MODEL), lambda b, j: (b * nt + j, gcb + 2)),
            pl.BlockSpec((tm, D_MODEL), row),
            _layer_spec((3, BRANCH_WIDTH, D_MODEL), layer),
            _layer_spec((D_MODEL, D_MODEL), layer),
        ],
        out_specs=pl.BlockSpec((tm, D_MODEL), row),
        out_shape=jax.ShapeDtypeStruct((bsz * seq, D_MODEL), F32),
        compiler_params=_cparams(("parallel", "parallel")),
        name="merge",
    )(z, z, pool_w, pool_scale, yb, yc, z, z, z, x, w_branch, w_out)


def _ffn_ple_kernel(x_ref, nw_ref, wu_ref, cw_ref, wd_ref, p_ref, pnw_ref, wg_ref, wp_ref, fw_ref, o_ref,
                    act_ref, carry_ref, pad_ref, *, tm, fc, final):
    j = pl.program_id(1)
    x = x_ref[...]
    h = _rms(x, nw_ref[...]).astype(BF16)
    for f in range(FFN_DIM // fc):
        conv = []
        for part in range(2):
            cols = slice(part * FFN_DIM + f * fc, part * FFN_DIM + (f + 1) * fc)
            buf = 2 * (f % 2) + part
            up = _dot(h, wu_ref[:, cols])
            pad_ref[buf, 0:SUBLANES, :] = jnp.where(j == 0, 0.0, carry_ref[:, cols])
            pad_ref[buf, SUBLANES:SUBLANES + tm, :] = up
            carry_ref[:, cols] = up[tm - SUBLANES:tm, :]
            y = cw_ref[FFN_CONV - 1:FFN_CONV, cols] * up
            for tap in range(FFN_CONV - 1):
                off = SUBLANES - (FFN_CONV - 1) + tap
                y = y + cw_ref[tap:tap + 1, cols] * pad_ref[buf, off:off + tm, :]
            conv.append(y)
        act_ref[:, f * fc:(f + 1) * fc] = (jax.nn.gelu(conv[0], approximate=True) * conv[1]).astype(BF16)
    x = x + _dot(act_ref[...], wd_ref[...])
    gate = jax.nn.sigmoid(_dot(_rms(x, pnw_ref[...]).astype(BF16), wg_ref[...]))
    x = x + _dot(p_ref[...].astype(BF16), wp_ref[...]) * gate
    if final:
        x = _rms(x, fw_ref[...])
    o_ref[...] = x


def _ffn_ple(x, nw, w_up, w_conv, w_down, p, ple_nw, w_gate, w_proj, final_w, layer, final, bsz, seq,
             tm=256, fc=256):
    nt = seq // tm
    resident = dict(pipeline_mode=pl.Buffered(1))
    row = lambda b, j: (b * nt + j, 0)
    return pl.pallas_call(
        functools.partial(_ffn_ple_kernel, tm=tm, fc=fc, final=final),
        grid=(bsz, nt),
        in_specs=[
            pl.BlockSpec((tm, D_MODEL), row),
            _layer_spec((1, D_MODEL), layer),
            _layer_spec((D_MODEL, 2 * FFN_DIM), layer, **resident),
            _layer_spec((FFN_CONV, 2 * FFN_DIM), layer),
            _layer_spec((FFN_DIM, D_MODEL), layer, **resident),
            _layer_spec((tm, PLE_DIM), layer, row),
            _layer_spec((1, D_MODEL), layer),
            _layer_spec((D_MODEL, D_MODEL), layer, **resident),
            _layer_spec((PLE_DIM, D_MODEL), layer, **resident),
            pl.BlockSpec((1, D_MODEL), lambda b, j: (0, 0)),
        ],
        out_specs=pl.BlockSpec((tm, D_MODEL), row),
        out_shape=jax.ShapeDtypeStruct((bsz * seq, D_MODEL), F32),
        scratch_shapes=[
            pltpu.VMEM((tm, FFN_DIM), BF16),
            pltpu.VMEM((SUBLANES, 2 * FFN_DIM), F32),
            pltpu.VMEM((4, SUBLANES + tm, fc), F32),
        ],
        compiler_params=_cparams(("parallel", "arbitrary")),
        name="ffn_ple",
    )(x, nw, w_up, w_conv, w_down, p, ple_nw, w_gate, w_proj, final_w)


def _split_w_in(w):
    w = lax.optimization_barrier(w.astype(BF16))
    aligned = w[..., 0:2560]
    dn_ba = w[..., 2560:2568]
    mb_qk = w[..., 2568:3592]
    mb_v = w[..., 3592:4104]
    gates = w[..., 4104:7176]
    main = jnp.concatenate([aligned, mb_v, mb_qk, gates], axis=-1)
    small = jnp.pad(dn_ba, ((0, 0),) * (w.ndim - 1) + ((0, LANES - 2 * DN_HEADS),))
    return main, small


def kernel(x, p, rel_bias, norm_mix, w_in, pool_w, pool_scale, dn_conv, dn_a_log, dn_dt_bias,
           dn_norm, w_branch, w_out, norm_ffn, ffn_up, ffn_conv, ffn_down, norm_ple, ple_gate,
           ple_proj, norm_final):
    bsz, seq, d = x.shape
    depth = w_in.shape[0]
    t = bsz * seq
    xt = x.reshape(t, d)
    row = lambda a: a[:, None, :]
    w_main, w_small = _split_w_in(w_in)
    pool_w, w_branch, w_out = pool_w.astype(BF16), w_branch.astype(BF16), w_out.astype(BF16)
    ffn_up, ffn_down = ffn_up.astype(BF16), ffn_down.astype(BF16)
    ple_gate, ple_proj = ple_gate.astype(BF16), ple_proj.astype(BF16)
    p = p.reshape(depth, t, PLE_DIM)
    dist_bias = _moba_dist_bias(rel_bias)
    head_params = jnp.zeros((depth, SUBLANES, LANES), F32)
    head_params = head_params.at[:, 0, DN_HEADS:2 * DN_HEADS].set(dn_a_log)
    head_params = head_params.at[:, 1, DN_HEADS:2 * DN_HEADS].set(dn_dt_bias)
    for i in range(depth):
        z, zs, zqk = _in_proj(xt, row(norm_mix), w_main, w_small, i)
        y_b = _deltanet(z, zs, dn_conv, head_params, row(dn_norm), i, bsz, seq)
        y_c = _moba(z, zqk, dist_bias, bsz, seq)
        xt = _merge(y_b, y_c, z, xt, pool_w, row(pool_scale), w_branch, w_out, i, bsz, seq)
        xt = _ffn_ple(xt, row(norm_ffn), ffn_up, ffn_conv, ffn_down, p, row(norm_ple), ple_gate, ple_proj,
                      norm_final[None, :], i, i == depth - 1, bsz, seq)
    return xt.reshape(bsz, seq, d)
```

```python
import functools
import math

import jax
import jax.numpy as jnp
from jax import lax
from jax.experimental import pallas as pl
from jax.experimental.pallas import tpu as pltpu

F32 = jnp.float32
BF16 = jnp.bfloat16
HIGHEST = lax.Precision.HIGHEST
ACT = jnp.bfloat16

D_MODEL = 1024
BRANCH_WIDTH = 512
POOL_WINDOWS = (2, 4, 8, 16)
POOL_GC = 128
POOL_HALO = 16
DN_HEADS = 4
DN_HEAD_DIM = 128
DN_CONV = 4
DN_CHUNK = 64
MOBA_HEADS = 8
MOBA_HEAD_DIM = 64
MOBA_BLOCK = 256
MOBA_TOPK = 3
MOBA_ONES_ROWS = 16
MOBA_QTILE = 128
REL_BUCKETS = 32
REL_MAX_DIST = 128
FFN_DIM = 2816
FFN_CONV = 3
PLE_DIM = 256
NORM_EPS = 1e-6
NEG_INF = -1e30
LOG2E = math.log2(math.e)
LANES = 128
SUBLANES = 8

COL_POOL = 0
COL_DNQ = 512
COL_DNK = 1024
COL_DNV = 1536
COL_DNZ = 2048
COL_MV = 2560
COL_MQ = 3072
COL_MK = 3584
COL_GATE = 4096
Z_COLS = 7168

VMEM_LIMIT = 56 * 1024 * 1024


def _cparams(sem):
    return pltpu.CompilerParams(dimension_semantics=sem, vmem_limit_bytes=VMEM_LIMIT)


def _layer_spec(block, layer, index=None, **kwargs):
    def index_map(*grid):
        rest = index(*grid) if index is not None else (0,) * len(block)
        return (layer,) + tuple(rest)
    return pl.BlockSpec((None,) + tuple(block), index_map, **kwargs)


def _dot(a, b, precision=None):
    return lax.dot_general(a, b, (((1,), (0,)), ((), ())), precision=precision,
                           preferred_element_type=F32)


def _dot_nt(a, b, precision=None):
    return lax.dot_general(a, b, (((1,), (1,)), ((), ())), precision=precision,
                           preferred_element_type=F32)


def _split3(x):
    hi = x.astype(BF16)
    rest = x - hi.astype(F32)
    mid = rest.astype(BF16)
    lo = (rest - mid.astype(F32)).astype(BF16)
    return hi, mid, lo


def _mm(a, b):
    return _dot(a.astype(BF16), b.astype(BF16))


def _mm_nt(a, b):
    return _dot_nt(a.astype(BF16), b.astype(BF16))


def _mm_tn(a, b):
    return lax.dot_general(a.astype(BF16), b.astype(BF16), (((0,), (0,)), ((), ())),
                           preferred_element_type=F32)


DN_INV_BASE = 8
DN_GROUP = 4


def _rms(x, w):
    return x * lax.rsqrt(jnp.mean(x * x, axis=-1, keepdims=True) + NORM_EPS) * w


def _in_proj_kernel(x_ref, nw_ref, w_ref, ws_ref, z_ref, zs_ref, zqk_ref, *, tn):
    hb = _rms(x_ref[...], nw_ref[...]).astype(BF16)
    zs_ref[...] = _dot(hb, ws_ref[...])
    for jb in range(Z_COLS // tn):
        cols = slice(jb * tn, (jb + 1) * tn)
        z = _dot(hb, w_ref[:, cols])
        z_ref[:, cols] = z.astype(ACT)
        if jb == COL_MQ // tn:
            zqk_ref[...] = z


def _in_proj(x, nw, w_main, w_small, layer, tm=512, tn=1024):
    t = x.shape[0]
    assert COL_MQ % tn == 0 and COL_MK == COL_MQ + tn // 2
    resident = dict(pipeline_mode=pl.Buffered(1))
    return pl.pallas_call(
        functools.partial(_in_proj_kernel, tn=tn),
        grid=(t // tm,),
        in_specs=[
            pl.BlockSpec((tm, D_MODEL), lambda i: (i, 0)),
            _layer_spec((1, D_MODEL), layer),
            _layer_spec((D_MODEL, Z_COLS), layer, **resident),
            _layer_spec((D_MODEL, LANES), layer, **resident),
        ],
        out_specs=[
            pl.BlockSpec((tm, Z_COLS), lambda i: (i, 0)),
            pl.BlockSpec((tm, LANES), lambda i: (i, 0)),
            pl.BlockSpec((tm, tn), lambda i: (i, 0)),
        ],
        out_shape=[jax.ShapeDtypeStruct((t, Z_COLS), ACT),
                   jax.ShapeDtypeStruct((t, LANES), F32),
                   jax.ShapeDtypeStruct((t, tn), F32)],
        compiler_params=_cparams(("parallel",)),
        name="in_proj",
    )(x, nw, w_main, w_small)


def _pool_mixer(u_ref, up_ref, w_ref, sc_ref, j, tp):
    halo = jnp.where(j == 0, 0.0, up_ref[...].astype(F32))
    t = j * tp + lax.broadcasted_iota(jnp.int32, (tp, POOL_GC), 0)
    out = []
    for g, win in enumerate(POOL_WINDOWS):
        cols = slice(g * POOL_GC, (g + 1) * POOL_GC)
        u = u_ref[:, cols].astype(F32)
        tot = jnp.concatenate([halo[:, cols], u], axis=0)
        width = 1
        while width < win:
            tot = tot + pltpu.roll(tot, width, 0)
            width *= 2
        cnt = jnp.minimum(t + 1, win).astype(F32)
        mixed = tot[POOL_HALO:, :] / cnt - u
        y = _dot(mixed.astype(BF16), w_ref[g])
        out.append((y * sc_ref[:, cols]).astype(BF16))
    return jnp.concatenate(out, axis=1)


def _dn_kernel(q_ref, k_ref, v_ref, zg_ref, zs_ref, cwq_ref, cwk_ref, cwv_ref, hp_ref, nw_ref, tri_ref,
               y_ref, halo_ref, state_ref, qs_ref, ks_ref, vs_ref, kb_ref, kg_ref, qd_ref,
               kd_ref, gs_ref, os_ref, *, tc):
    c = pl.program_id(1)

    @pl.when(c == 0)
    def _():
        halo_ref[...] = jnp.zeros_like(halo_ref)
        state_ref[...] = jnp.zeros_like(state_ref)

    for part, (src, cw, dst) in enumerate(((q_ref, cwq_ref, qs_ref), (k_ref, cwk_ref, ks_ref),
                                          (v_ref, cwv_ref, vs_ref))):
        x = src[...].astype(F32)
        ext = jnp.concatenate([halo_ref[part], x], axis=0)
        halo_ref[part] = x[tc - SUBLANES:tc, :]
        acc = cw[0:1, :] * ext
        for tap in range(1, DN_CONV):
            acc = cw[tap:tap + 1, :] * ext + pltpu.roll(acc, 1, 0)
        conv = acc[SUBLANES:, :]
        dst[...] = conv * jax.nn.sigmoid(conv)

    n_chunks = tc // DN_CHUNK
    zs = zs_ref[...]
    beta_c = jax.nn.sigmoid(zs)
    g_c = -jnp.exp(hp_ref[0:1, :]) * jnp.logaddexp(zs + hp_ref[1:2, :], 0.0)
    gc_c = sum(_dot(tri_ref[...], part) for part in _split3(g_c))
    g_last_c = jnp.concatenate(
        [jnp.broadcast_to(gc_c[(n + 1) * DN_CHUNK - 1:(n + 1) * DN_CHUNK, :], (DN_CHUNK, LANES))
         for n in range(n_chunks)], axis=0)
    egc_c = jnp.exp(gc_c)
    kdf_c = jnp.exp(g_last_c - gc_c)

    for h in range(DN_HEADS):
        cols = slice(h * DN_HEAD_DIM, (h + 1) * DN_HEAD_DIM)
        lane_bcast = lambda x, lane: jnp.broadcast_to(x[:, lane:lane + 1], (tc, LANES))
        beta = lane_bcast(beta_c, h)
        egc = lane_bcast(egc_c, DN_HEADS + h)
        q = qs_ref[:, cols]
        k = ks_ref[:, cols]
        q = q * lax.rsqrt(jnp.sum(q * q, axis=-1, keepdims=True) + NORM_EPS) * (DN_HEAD_DIM ** -0.5)
        k = k * lax.rsqrt(jnp.sum(k * k, axis=-1, keepdims=True) + NORM_EPS)
        kb = k * beta
        qs_ref[:, cols] = q
        ks_ref[:, cols] = k
        kb_ref[:, cols] = kb
        vs_ref[:, cols] = vs_ref[:, cols] * beta
        kg_ref[:, cols] = kb * egc
        qd_ref[:, cols] = q * egc
        kd_ref[:, cols] = k * lane_bcast(kdf_c, DN_HEADS + h)
        gs_ref[h] = lane_bcast(gc_c, DN_HEADS + h)

    hc = DN_HEADS * DN_CHUNK
    ri = lax.broadcasted_iota(jnp.int32, (hc, hc), 0)
    ci = lax.broadcasted_iota(jnp.int32, (hc, hc), 1)
    same_head = (ri // DN_CHUNK) == (ci // DN_CHUNK)
    incl = same_head & (ri >= ci)
    level_masks = [((ri // DN_INV_BASE) == (ci // DN_INV_BASE)) & (ri > ci)]
    size = DN_INV_BASE
    while size < DN_CHUNK:
        level_masks.append(((ri // (2 * size)) == (ci // (2 * size)))
                           & ((ri % (2 * size)) >= size) & ((ci % (2 * size)) < size))
        size *= 2
    eye_f = (ri == ci).astype(F32)

    def stack(ref, rows):
        return jnp.concatenate([ref[rows, h * DN_HEAD_DIM:(h + 1) * DN_HEAD_DIM]
                                for h in range(DN_HEADS)], axis=0)

    heads = range(DN_HEADS)
    last = lambda x, h: x[(h + 1) * DN_CHUNK - 1:(h + 1) * DN_CHUNK, :]
    state = [state_ref[h] for h in heads]
    for first in range(0, n_chunks, DN_GROUP):
        grp = range(first, min(first + DN_GROUP, n_chunks))
        rows = [slice(n * DN_CHUNK, (n + 1) * DN_CHUNK) for n in grp]
        q = [stack(qs_ref, r) for r in rows]
        k = [stack(ks_ref, r) for r in rows]
        kb = [stack(kb_ref, r) for r in rows]
        vb = [stack(vs_ref, r) for r in rows]
        kg = [stack(kg_ref, r) for r in rows]
        q_dec = [stack(qd_ref, r) for r in rows]
        k_dec = [stack(kd_ref, r) for r in rows]
        gc = [jnp.concatenate([gs_ref[h, r, :] for h in heads], axis=0) for r in rows]
        gc_t = [x.T for x in gc]
        decay = [jnp.exp(jnp.where(incl, jnp.concatenate([x, x], axis=1)
                                   - jnp.concatenate([xt, xt], axis=0), NEG_INF))
                 for x, xt in zip(gc, gc_t)]
        a_kk = [_mm_nt(x, y) * d for x, y, d in zip(kb, k, decay)]
        m = [-jnp.where(level_masks[0], a, 0.0) for a in a_kk]
        t_mat = [eye_f + x for x in m]
        for _ in range(2):
            m = [_mm(x, x) for x in m]
            t_mat = [t + _mm(t, x) for t, x in zip(t_mat, m)]
        for mask in level_masks[1:]:
            low = [_mm(t, jnp.where(mask, a, 0.0)) for t, a in zip(t_mat, a_kk)]
            t_mat = [t - _mm(x, t) for t, x in zip(t_mat, low)]
        uw = [_mm(t, jnp.concatenate([x, y], axis=1))
              for t, x, y in zip(t_mat, vb, kg)]
        a_qk = [_mm_nt(x, y) * d for x, y, d in zip(q, k, decay)]
        for i, r in enumerate(rows):
            u = uw[i][:, :DN_HEAD_DIM]
            w = uw[i][:, DN_HEAD_DIM:]
            ws, qs_ = [], []
            for h in heads:
                hr = slice(h * DN_CHUNK, (h + 1) * DN_CHUNK)
                both = _mm(jnp.concatenate([w[hr], q_dec[i][hr]], axis=0), state[h])
                ws.append(both[:DN_CHUNK])
                qs_.append(both[DN_CHUNK:])
            v_new = u - jnp.concatenate(ws, axis=0)
            o = jnp.concatenate(qs_, axis=0) + _mm(a_qk[i], v_new)
            for h in heads:
                hr = slice(h * DN_CHUNK, (h + 1) * DN_CHUNK)
                os_ref[r, h * DN_HEAD_DIM:(h + 1) * DN_HEAD_DIM] = o[hr]
                state[h] = (state[h] * jnp.exp(last(gc[i], h))
                            + _mm_tn(k_dec[i][hr], v_new[hr]))
    for h in heads:
        state_ref[h] = state[h]

    for h in range(DN_HEADS):
        cols = slice(h * DN_HEAD_DIM, (h + 1) * DN_HEAD_DIM)
        zg = zg_ref[:, cols].astype(F32)
        y_ref[:, cols] = (_rms(os_ref[:, cols], nw_ref[...]) * (zg * jax.nn.sigmoid(zg))).astype(ACT)


def _deltanet(z, zs, conv_w, head_params, norm_w, layer, bsz, seq, tc=256):
    nt = seq // tc
    wide = DN_HEADS * DN_HEAD_DIM

    def col(cb):
        return pl.BlockSpec((tc, wide), lambda b, c: (b * nt + c, cb))

    def cw(cb):
        return _layer_spec((DN_CONV, wide), layer, lambda b, c: (0, cb))

    pos = jnp.arange(tc, dtype=jnp.int32)
    tri = ((pos[:, None] // DN_CHUNK == pos[None, :] // DN_CHUNK) & (pos[:, None] >= pos[None, :])).astype(BF16)
    tile = pltpu.VMEM((tc, wide), F32)
    return pl.pallas_call(
        functools.partial(_dn_kernel, tc=tc),
        grid=(bsz, nt),
        in_specs=[
            col(COL_DNQ // wide), col(COL_DNK // wide), col(COL_DNV // wide), col(COL_DNZ // wide),
            pl.BlockSpec((tc, LANES), lambda b, c: (b * nt + c, 0)),
            cw(0), cw(1), cw(2),
            _layer_spec((SUBLANES, LANES), layer),
            _layer_spec((1, DN_HEAD_DIM), layer),
            pl.BlockSpec((tc, tc), lambda b, c: (0, 0)),
        ],
        out_specs=pl.BlockSpec((tc, wide), lambda b, c: (b * nt + c, 0)),
        out_shape=jax.ShapeDtypeStruct((bsz * seq, wide), ACT),
        scratch_shapes=[
            pltpu.VMEM((3, SUBLANES, wide), F32),
            pltpu.VMEM((DN_HEADS, DN_HEAD_DIM, DN_HEAD_DIM), F32),
            tile, tile, tile, tile, tile, tile, tile,
            pltpu.VMEM((DN_HEADS, tc, LANES), F32),
            tile,
        ],
        compiler_params=_cparams(("parallel", "arbitrary")),
        name="deltanet",
    )(z, z, z, z, zs, conv_w, conv_w, conv_w, head_params, norm_w, tri)


def _moba_kernel(q_ref, k_ref, v_ref, d_ref, o_ref, kb_ref, vt_ref, bown_ref, bprev_ref, *, nb):
    blk = MOBA_BLOCK
    half = MOBA_HEAD_DIM
    assert 2 * half == LANES

    kmeans = []
    for j in range(nb):
        kj = k_ref[j * blk:(j + 1) * blk, :]
        kmeans.append(jnp.mean(kj.astype(F32), axis=0, keepdims=True))
        kb_ref[j * blk:(j + 1) * blk, :] = kj.astype(BF16)
        vt_ref[j, 0:LANES, :] = v_ref[j * blk:(j + 1) * blk, :].astype(F32).T.astype(BF16)
        vt_ref[j, LANES:LANES + MOBA_ONES_ROWS, :] = jnp.ones((MOBA_ONES_ROWS, blk), BF16)
    kmean = jnp.concatenate(kmeans, axis=0)

    r = lax.broadcasted_iota(jnp.int32, (LANES, LANES), 0)
    c = lax.broadcasted_iota(jnp.int32, (LANES, LANES), 1)
    for hh in range(2):
        far = jnp.broadcast_to(d_ref[hh, 1:2, :], (LANES, LANES)) * LOG2E
        band = pltpu.roll(jnp.broadcast_to(d_ref[hh, 0:1, :], (LANES, LANES)), 0, 1,
                          stride=1, stride_axis=0) * LOG2E
        diag = jnp.where(c >= r, band, NEG_INF)
        off = jnp.where(c < r, band, far)
        lo = hh * blk
        bown_ref[0:LANES, lo:lo + LANES] = diag
        bown_ref[0:LANES, lo + LANES:lo + blk] = off
        bown_ref[LANES:blk, lo:lo + LANES] = jnp.full((LANES, LANES), NEG_INF, F32)
        bown_ref[LANES:blk, lo + LANES:lo + blk] = diag
        bprev_ref[0:LANES, lo:lo + LANES] = far
        bprev_ref[0:LANES, lo + LANES:lo + blk] = far
        bprev_ref[LANES:blk, lo:lo + LANES] = off
        bprev_ref[LANES:blk, lo + LANES:lo + blk] = far
    far_row = jnp.concatenate([jnp.broadcast_to(d_ref[0, 1:2, :], (1, LANES))] * 2
                              + [jnp.broadcast_to(d_ref[1, 1:2, :], (1, LANES))] * 2, axis=1) * LOG2E

    qt = MOBA_QTILE
    tiles_per_blk = blk // qt
    lane = lax.broadcasted_iota(jnp.int32, (qt, LANES), 1)
    blk_id = lax.broadcasted_iota(jnp.int32, (nb, 2 * qt), 0)
    row = lax.broadcasted_iota(jnp.int32, (LANES, qt), 0)
    far_tile = jnp.concatenate([far_row[:, :qt], far_row[:, blk:blk + qt]], axis=1)

    def tile_cols(ref, part, keys):
        return jnp.concatenate([ref[0:keys, part * qt:(part + 1) * qt],
                                ref[0:keys, blk + part * qt:blk + (part + 1) * qt]], axis=1)

    def score_stage(tile):
        i, part = divmod(tile, tiles_per_blk)
        rows = slice(tile * qt, (tile + 1) * qt)
        qf = q_ref[rows, :].astype(F32) * (MOBA_HEAD_DIM ** -0.5)
        q2 = jnp.concatenate([jnp.where(lane < half, qf, 0.0), jnp.where(lane >= half, qf, 0.0)],
                             axis=0)
        q2b = (q2 * LOG2E).astype(BF16)

        sel = None
        if i > MOBA_TOPK:
            gate = jnp.where(blk_id < i, _dot_nt(kmean, q2, HIGHEST), NEG_INF)
            sel = []
            for j in range(i):
                gj = gate[j:j + 1, :]
                ahead = (gate > gj) | ((gate == gj) & (blk_id < j))
                rank = jnp.sum(ahead.astype(F32), axis=0, keepdims=True)
                sel.append(rank < MOBA_TOPK)

        scores = []
        for j in range(i + 1):
            if j == i:
                keys = (part + 1) * qt
                s = _dot_nt(kb_ref[j * blk:j * blk + keys, :], q2b) + tile_cols(bown_ref, part, keys)
            else:
                s = _dot_nt(kb_ref[j * blk:(j + 1) * blk, :], q2b)
                if j == i - 1:
                    s = s + tile_cols(bprev_ref, part, blk)
                    if sel is not None:
                        s = s + jnp.where(sel[j], 0.0, NEG_INF)
                elif sel is not None:
                    s = s + jnp.where(sel[j], far_tile, NEG_INF)
                else:
                    s = s + far_tile
            scores.append(s)
        return scores

    def exp_stage(scores):
        m = functools.reduce(jnp.maximum, [jnp.max(s, axis=0, keepdims=True) for s in scores])
        return ([jnp.exp2(s - m).astype(BF16) for s in scores],)

    def pv_stage(tile, probs):
        acc = None
        for j, p in enumerate(probs):
            pv = _dot(vt_ref[j, :, 0:p.shape[0]], p)
            acc = pv if acc is None else acc + pv
        out_t = acc[:LANES, :] / acc[LANES:LANES + 1, :]
        out_t = jnp.where(row < half, out_t[:, :qt], out_t[:, qt:])
        o_ref[tile * qt:(tile + 1) * qt, :] = out_t.T.astype(ACT)

    sc_q, ex_q = None, None
    for tile in list(reversed(range(nb * tiles_per_blk))) + [None, None]:
        new_sc = (tile, score_stage(tile)) if tile is not None else None
        new_ex = (sc_q[0],) + exp_stage(sc_q[1]) if sc_q is not None else None
        if ex_q is not None:
            pv_stage(*ex_q)
        sc_q, ex_q = new_sc, new_ex


def _moba(z, zqk, dist_bias, bsz, seq):
    nb = seq // MOBA_BLOCK
    pairs = MOBA_HEADS // 2
    kcb = (COL_MK - COL_MQ) // LANES
    vcb = COL_MV // LANES
    return pl.pallas_call(
        functools.partial(_moba_kernel, nb=nb),
        grid=(bsz, pairs),
        in_specs=[
            pl.BlockSpec((seq, LANES), lambda b, p: (b, p)),
            pl.BlockSpec((seq, LANES), lambda b, p: (b, kcb + p)),
            pl.BlockSpec((seq, LANES), lambda b, p: (b, vcb + p)),
            pl.BlockSpec((2, SUBLANES, LANES), lambda b, p: (p, 0, 0)),
        ],
        out_specs=pl.BlockSpec((seq, LANES), lambda b, p: (b, p)),
        out_shape=jax.ShapeDtypeStruct((bsz * seq, MOBA_HEADS * MOBA_HEAD_DIM), ACT),
        scratch_shapes=[
            pltpu.VMEM((seq, LANES), BF16),
            pltpu.VMEM((nb, LANES + MOBA_ONES_ROWS, MOBA_BLOCK), BF16),
            pltpu.VMEM((MOBA_BLOCK, 2 * MOBA_BLOCK), F32),
            pltpu.VMEM((MOBA_BLOCK, 2 * MOBA_BLOCK), F32),
        ],
        compiler_params=_cparams(("parallel", "parallel")),
        name="moba",
    )(zqk, zqk, z, dist_bias)


def _t5_bucket(rel):
    n = jnp.maximum(-rel, 0)
    exact = REL_BUCKETS // 2
    nf = jnp.maximum(n, 1).astype(F32)
    large = exact + (jnp.log(nf / exact) / math.log(REL_MAX_DIST / exact)
                     * (REL_BUCKETS - exact)).astype(jnp.int32)
    large = jnp.minimum(large, REL_BUCKETS - 1)
    return jnp.where(n < exact, n, large)


def _moba_dist_bias(rel_bias):
    bias_ht = rel_bias.T.astype(F32)
    near = bias_ht[:, _t5_bucket(-jnp.arange(LANES, dtype=jnp.int32))]
    far = bias_ht[:, _t5_bucket(jnp.full((LANES,), -REL_MAX_DIST, jnp.int32))]
    rows = jnp.zeros((MOBA_HEADS, SUBLANES, LANES), F32)
    return rows.at[:, 0, :].set(near).at[:, 1, :].set(far)


def _merge_core(u_ref, up_ref, pw_ref, ps_ref, yb_ref, yc_ref, ga_ref, gb_ref, gc_ref, x_ref, wb_ref, wo_ref,
                j, tm):
    y_a = _pool_mixer(u_ref, up_ref, pw_ref, ps_ref, j, tm)
    merged = None
    for n, (y, g_ref) in enumerate(((y_a, ga_ref), (yb_ref[...], gb_ref), (yc_ref[...], gc_ref))):
        proj = _dot(y, wb_ref[n])
        gate = jax.nn.sigmoid(g_ref[...].astype(F32))
        term = gate * proj
        merged = term if merged is None else merged + term
    return x_ref[...] + _dot(merged.astype(BF16), wo_ref[...])


def _post_kernel(u_ref, up_ref, pw_ref, ps_ref, yb_ref, yc_ref, ga_ref, gb_ref, gc_ref, x_ref, wb_ref, wo_ref,
                 nw_ref, wu_ref, cw_ref, wd_ref, p_ref, pnw_ref, wg_ref, wp_ref, fw_ref, o_ref,
                 act_ref, carry_ref, pad_ref, *, tm, fc, final):
    j = pl.program_id(1)
    x = _merge_core(u_ref, up_ref, pw_ref, ps_ref, yb_ref, yc_ref, ga_ref, gb_ref, gc_ref, x_ref, wb_ref,
                    wo_ref, j, tm)
    h = _rms(x, nw_ref[...]).astype(BF16)
    for f in range(FFN_DIM // fc):
        conv = []
        for part in range(2):
            cols = slice(part * FFN_DIM + f * fc, part * FFN_DIM + (f + 1) * fc)
            buf = 2 * (f % 2) + part
            up = _dot(h, wu_ref[:, cols])
            pad_ref[buf, 0:SUBLANES, :] = jnp.where(j == 0, 0.0, carry_ref[:, cols])
            pad_ref[buf, SUBLANES:SUBLANES + tm, :] = up
            carry_ref[:, cols] = up[tm - SUBLANES:tm, :]
            y = cw_ref[FFN_CONV - 1:FFN_CONV, cols] * up
            for tap in range(FFN_CONV - 1):
                off = SUBLANES - (FFN_CONV - 1) + tap
                y = y + cw_ref[tap:tap + 1, cols] * pad_ref[buf, off:off + tm, :]
            conv.append(y)
        act_ref[:, f * fc:(f + 1) * fc] = (jax.nn.gelu(conv[0], approximate=True) * conv[1]).astype(BF16)
    x = x + _dot(act_ref[...], wd_ref[...])
    gate = jax.nn.sigmoid(_dot(_rms(x, pnw_ref[...]).astype(BF16), wg_ref[...]))
    x = x + _dot(p_ref[...].astype(BF16), wp_ref[...]) * gate
    if final:
        x = _rms(x, fw_ref[...])
    o_ref[...] = x


def _post(yb, yc, z, x, pool_w, pool_scale, w_branch, w_out, nw, w_up, w_conv, w_down, p, ple_nw, w_gate,
          w_proj, final_w, layer, final, bsz, seq, tm=256, fc=256):
    nt = seq // tm
    hb = tm // POOL_HALO
    gcb = COL_GATE // D_MODEL
    resident = dict(pipeline_mode=pl.Buffered(1))
    row = lambda b, j: (b * nt + j, 0)
    return pl.pallas_call(
        functools.partial(_post_kernel, tm=tm, fc=fc, final=final),
        grid=(bsz, nt),
        in_specs=[
            pl.BlockSpec((tm, BRANCH_WIDTH), row),
            pl.BlockSpec((POOL_HALO, BRANCH_WIDTH),
                         lambda b, j: (jnp.maximum((b * nt + j) * hb - 1, 0), 0)),
            _layer_spec((4, POOL_GC, POOL_GC), layer),
            _layer_spec((1, BRANCH_WIDTH), layer),
            pl.BlockSpec((tm, BRANCH_WIDTH), row),
            pl.BlockSpec((tm, BRANCH_WIDTH), row),
            pl.BlockSpec((tm, D_MODEL), lambda b, j: (b * nt + j, gcb)),
            pl.BlockSpec((tm, D_MODEL), lambda b, j: (b * nt + j, gcb + 1)),
            pl.BlockSpec((tm, D_MODEL), lambda b, j: (b * nt + j, gcb + 2)),
            pl.BlockSpec((tm, D_MODEL), row),
            _layer_spec((3, BRANCH_WIDTH, D_MODEL), layer, **resident),
            _layer_spec((D_MODEL, D_MODEL), layer, **resident),
            _layer_spec((1, D_MODEL), layer),
            _layer_spec((D_MODEL, 2 * FFN_DIM), layer, **resident),
            _layer_spec((FFN_CONV, 2 * FFN_DIM), layer),
            _layer_spec((FFN_DIM, D_MODEL), layer, **resident),
            _layer_spec((tm, PLE_DIM), layer, row),
            _layer_spec((1, D_MODEL), layer),
            _layer_spec((D_MODEL, D_MODEL), layer, **resident),
            _layer_spec((PLE_DIM, D_MODEL), layer, **resident),
            pl.BlockSpec((1, D_MODEL), lambda b, j: (0, 0)),
        ],
        out_specs=pl.BlockSpec((tm, D_MODEL), row),
        out_shape=jax.ShapeDtypeStruct((bsz * seq, D_MODEL), F32),
        scratch_shapes=[
            pltpu.VMEM((tm, FFN_DIM), BF16),
            pltpu.VMEM((SUBLANES, 2 * FFN_DIM), F32),
            pltpu.VMEM((4, SUBLANES + tm, fc), F32),
        ],
        compiler_params=_cparams(("parallel", "arbitrary")),
        name="post",
    )(z, z, pool_w, pool_scale, yb, yc, z, z, z, x, w_branch, w_out, nw, w_up, w_conv, w_down, p, ple_nw,
      w_gate, w_proj, final_w)


def _split_w_in(w):
    w = w.astype(BF16)
    aligned = w[..., 0:2560]
    dn_ba = w[..., 2560:2568]
    mb_qk = w[..., 2568:3592]
    mb_v = w[..., 3592:4104]
    gates = w[..., 4104:7176]
    main = jnp.concatenate([aligned, mb_v, mb_qk, gates], axis=-1)
    small = jnp.pad(dn_ba, ((0, 0),) * (w.ndim - 1) + ((0, LANES - 2 * DN_HEADS),))
    return main, small


def kernel(x, p, rel_bias, norm_mix, w_in, pool_w, pool_scale, dn_conv, dn_a_log, dn_dt_bias,
           dn_norm, w_branch, w_out, norm_ffn, ffn_up, ffn_conv, ffn_down, norm_ple, ple_gate,
           ple_proj, norm_final):
    bsz, seq, d = x.shape
    depth = w_in.shape[0]
    t = bsz * seq
    xt = x.reshape(t, d)
    row = lambda a: a[:, None, :]
    w_main, w_small = _split_w_in(w_in)
    pool_w, w_branch, w_out = pool_w.astype(BF16), w_branch.astype(BF16), w_out.astype(BF16)
    ffn_up, ffn_down = ffn_up.astype(BF16), ffn_down.astype(BF16)
    ple_gate, ple_proj = ple_gate.astype(BF16), ple_proj.astype(BF16)
    p = p.reshape(depth, t, PLE_DIM)
    dist_bias = _moba_dist_bias(rel_bias)
    head_params = jnp.zeros((depth, SUBLANES, LANES), F32)
    head_params = head_params.at[:, 0, DN_HEADS:2 * DN_HEADS].set(dn_a_log)
    head_params = head_params.at[:, 1, DN_HEADS:2 * DN_HEADS].set(dn_dt_bias)
    for i in range(depth):
        z, zs, zqk = _in_proj(xt, row(norm_mix), w_main, w_small, i)
        y_b = _deltanet(z, zs, dn_conv, head_params, row(dn_norm), i, bsz, seq)
        y_c = _moba(z, zqk, dist_bias, bsz, seq)
        xt = _post(y_b, y_c, z, xt, pool_w, row(pool_scale), w_branch, w_out, row(norm_ffn), ffn_up, ffn_conv,
                   ffn_down, p, row(norm_ple), ple_gate, ple_proj, norm_final[None, :], i, i == depth - 1,
                   bsz, seq)
    return xt.reshape(bsz, seq, d)
```
